```python
import jax, jax.numpy as jnp
from jax import lax
import numpy as np

D_MODEL = 4096
BATCH = 2
SEQ = 4096
DEPTH = 4
DEC_BATCH = 16
DEC_SEQ = 16
PAST_LEN = 1024

CHUNK = 64
HEAD_DIM = 128
N_HEADS = D_MODEL // HEAD_DIM
H_A = N_HEADS // 2
H_B = N_HEADS - H_A
A_LEFT_CHUNKS = 8
A_BAND = (A_LEFT_CHUNKS + 1) * CHUNK
REL_CLIP = 128
H_C = N_HEADS
C_KV_HEADS = N_HEADS // 4
C_GROUP = H_C // C_KV_HEADS
IDX_HEADS = 32
IDX_DIM = 64
TOPK_MAX = 256
N_MEM = 256
MEM_HEADS = 4
MEM_W = MEM_HEADS * HEAD_DIM
D_FF = ((8 * D_MODEL // 3) + 127) // 128 * 128
CONV_W = 3
Q_BLOCK = 128
N_EVEN = (DEPTH + 1) // 2
N_ODD = DEPTH // 2
ALPHA = (2.0 * DEPTH) ** 0.25
BETA = (8.0 * DEPTH) ** -0.25
LN_EPS = 1e-5
ATTN_SCALE = HEAD_DIM ** -0.5
IDX_SCALE = IDX_DIM ** -0.5
IDX_W_SCALE = IDX_HEADS ** -0.5
NEG_INF = -1e30
EVEN_SIZES = (H_A * HEAD_DIM,) * 3 + (H_B * HEAD_DIM,) * 3
ODD_SIZES = (H_C * HEAD_DIM, C_KV_HEADS * HEAD_DIM, C_KV_HEADS * HEAD_DIM,
             IDX_HEADS * IDX_DIM, IDX_DIM, IDX_HEADS)
EVEN_IN = sum(EVEN_SIZES)
ODD_IN = sum(ODD_SIZES)

kernel_name = "hybrid_streaming_encoder_step"


def _layernorm(x, g, b):
    xf = x.astype(jnp.float32)
    mu = xf.mean(-1, keepdims=True)
    var = jnp.square(xf - mu).mean(-1, keepdims=True)
    return ((xf - mu) * lax.rsqrt(var + LN_EPS) * g + b).astype(x.dtype)


def _split(h, sizes):
    return jnp.split(h, np.cumsum(sizes)[:-1].tolist(), axis=-1)


def _alibi_slopes(n):
    return jnp.asarray(2.0 ** (-8.0 * np.arange(1, n + 1) / n), jnp.float32)


def _band_attn_prompt(q, k, v, rel_bias):
    B, S, H, hd = q.shape
    nc = S // CHUNK
    left = A_LEFT_CHUNKS * CHUNK
    pad = ((0, 0), (left, 0), (0, 0), (0, 0))
    qc = q.reshape(B, nc, CHUNK, H, hd)
    kp = jnp.pad(k, pad).reshape(B, nc + A_LEFT_CHUNKS, CHUNK, H, hd)
    vp = jnp.pad(v, pad).reshape(B, nc + A_LEFT_CHUNKS, CHUNK, H, hd)
    kb = jnp.concatenate([kp[:, j:j + nc] for j in range(A_LEFT_CHUNKS + 1)], axis=2)
    vb = jnp.concatenate([vp[:, j:j + nc] for j in range(A_LEFT_CHUNKS + 1)], axis=2)
    qi = jnp.arange(CHUNK)
    kk = jnp.arange(A_BAND)
    rel = qi[:, None] + left - kk[None, :]
    bias = rel_bias[:, jnp.clip(rel, -REL_CLIP, REL_CLIP) + REL_CLIP]
    kpos = (jnp.arange(nc)[:, None] - A_LEFT_CHUNKS) * CHUNK + kk[None, :]
    s = jnp.einsum('bcqhd,bckhd->bchqk', qc, kb, preferred_element_type=jnp.float32) * ATTN_SCALE + bias
    s = jnp.where((kpos >= 0)[None, :, None, None, :], s, NEG_INF)
    p = jax.nn.softmax(s, axis=-1).astype(v.dtype)
    o = jnp.einsum('bchqk,bckhd->bcqhd', p, vb)
    return o.reshape(B, S, H * hd)


def _band_attn_sample(q, k_all, v_all, rel_bias):
    B, n, H, hd = q.shape
    W = k_all.shape[1] - n
    rel = jnp.arange(n)[:, None] + W - jnp.arange(W + n)[None, :]
    bias = rel_bias[:, jnp.clip(rel, -REL_CLIP, REL_CLIP) + REL_CLIP]
    s = jnp.einsum('bqhd,bkhd->bhqk', q, k_all, preferred_element_type=jnp.float32) * ATTN_SCALE + bias
    p = jax.nn.softmax(s, axis=-1).astype(v_all.dtype)
    return jnp.einsum('bhqk,bkhd->bqhd', p, v_all).reshape(B, n, H * hd)


def _stick_break(z, mask, v):
    log_1m = jnp.where(mask, jax.nn.log_sigmoid(-z), 0.0)
    suffix = lax.cumsum(log_1m, axis=3, reverse=True) - log_1m
    w = jnp.where(mask, jnp.exp(jax.nn.log_sigmoid(z) + suffix), 0.0)
    return jnp.einsum('bhqk,bkhd->bqhd', w.astype(v.dtype), v)


def _stick_breaking_prompt(q, k, v):
    B, S, H, hd = q.shape
    nb = S // Q_BLOCK
    qb = q.reshape(B, nb, Q_BLOCK, H, hd).swapaxes(0, 1)
    kpos = jnp.arange(S)

    def block(args):
        qi, bi = args
        qpos = bi * Q_BLOCK + jnp.arange(Q_BLOCK)
        z = jnp.einsum('bqhd,bkhd->bhqk', qi, k, preferred_element_type=jnp.float32) * ATTN_SCALE
        return _stick_break(z, kpos[None, :] < qpos[:, None], v)

    o = lax.map(block, (qb, jnp.arange(nb)))
    return o.swapaxes(0, 1).reshape(B, S, H * hd)


def _stick_breaking_sample(q, k_all, v_all):
    B, n, H, hd = q.shape
    L = k_all.shape[1]
    qpos = (L - n) + jnp.arange(n)
    z = jnp.einsum('bqhd,bkhd->bhqk', q, k_all, preferred_element_type=jnp.float32) * ATTN_SCALE
    return _stick_break(z, jnp.arange(L)[None, :] < qpos[:, None], v_all).reshape(B, n, H * hd)


def _dsa_attend(q, q_idx, w_idx, qpos, k, v, k_idx, topk, slopes):
    B, Q, H, hd = q.shape
    L = k.shape[1]
    qchunk = qpos // CHUNK
    score = jnp.einsum('bqhe,bse->bqhs', q_idx, k_idx, preferred_element_type=jnp.float32) * IDX_SCALE
    index = jnp.einsum('bqhs,bqh->bqs', jax.nn.relu(score), w_idx.astype(jnp.float32))
    admissible = (jnp.arange(L) // CHUNK)[None, :] <= qchunk[:, None]
    index = jnp.where(admissible[None], index, NEG_INF)
    _, sel = lax.top_k(index, topk)
    ok = (sel // CHUNK) <= qchunk[None, :, None]
    kg = jax.vmap(lambda kb, ib: kb[ib])(k, sel)
    vg = jax.vmap(lambda vb, ib: vb[ib])(v, sel)
    qg = q.reshape(B, Q, C_KV_HEADS, C_GROUP, hd)
    s = jnp.einsum('bqngd,bqknd->bqngk', qg, kg, preferred_element_type=jnp.float32) * ATTN_SCALE
    dist = jnp.abs(qpos[None, :, None] - sel).astype(jnp.float32)
    s = s - slopes.reshape(C_KV_HEADS, C_GROUP)[None, None, :, :, None] * dist[:, :, None, None, :]
    s = jnp.where(ok[:, :, None, None, :], s, NEG_INF)
    p = jax.nn.softmax(s, axis=-1).astype(v.dtype)
    o = jnp.einsum('bqngk,bqknd->bqngd', p, vg)
    return o.reshape(B, Q, H * hd)


def _dsa_prompt(q, k, v, q_idx, k_idx, w_idx, slopes):
    B, S = q.shape[:2]
    nb = S // Q_BLOCK
    topk = min(TOPK_MAX, S // 4)

    def to_blocks(a):
        return a.reshape((B, nb, Q_BLOCK) + a.shape[2:]).swapaxes(0, 1)

    def block(args):
        qi, qii, wi, bi = args
        qpos = bi * Q_BLOCK + jnp.arange(Q_BLOCK)
        return _dsa_attend(qi, qii, wi, qpos, k, v, k_idx, topk, slopes)

    o = lax.map(block, (to_blocks(q), to_blocks(q_idx), to_blocks(w_idx), jnp.arange(nb)))
    return o.swapaxes(0, 1).reshape(B, S, -1)


def _even_proj(x, w_in):
    B, T, _ = x.shape
    parts = _split(x @ w_in, EVEN_SIZES)
    heads = (H_A, H_A, H_A, H_B, H_B, H_B)
    return [p.reshape(B, T, n, HEAD_DIM) for p, n in zip(parts, heads)]


def _odd_proj(x, w_in):
    B, T, _ = x.shape
    q, k, v, qi, ki, wi = _split(x @ w_in, ODD_SIZES)
    return (q.reshape(B, T, H_C, HEAD_DIM), k.reshape(B, T, C_KV_HEADS, HEAD_DIM),
            v.reshape(B, T, C_KV_HEADS, HEAD_DIM), qi.reshape(B, T, IDX_HEADS, IDX_DIM),
            ki, wi * IDX_W_SCALE)


def _mem_kv(mem, w_kv):
    B, M, _ = mem.shape
    kv = (mem @ w_kv).reshape(B, M, 2, MEM_HEADS, HEAD_DIM)
    return kv[:, :, 0], kv[:, :, 1]


def _mem_attn(x, mk, mv, w_q, w_o):
    B, T, _ = x.shape
    q = (x @ w_q).reshape(B, T, MEM_HEADS, HEAD_DIM)
    s = jnp.einsum('bthd,bmhd->bhtm', q, mk, preferred_element_type=jnp.float32) * ATTN_SCALE
    p = jax.nn.softmax(s, axis=-1).astype(mv.dtype)
    o = jnp.einsum('bhtm,bmhd->bthd', p, mv).reshape(B, T, MEM_W)
    return o @ w_o


def _conv_ffn(x, prev, w_up, conv_w, conv_b, w_down):
    T = x.shape[1]
    up = jnp.concatenate([prev, x @ w_up], axis=1)
    h = sum((up[:, j:j + T] * conv_w[j] for j in range(CONV_W)), conv_b)
    a, g = jnp.split(h, 2, axis=-1)
    y = (a * jax.nn.gelu(g, approximate=False)) @ w_down
    return y, up[:, -(CONV_W - 1):]


def setup_inputs(seed: int = 0) -> dict:
    key = jax.random.key(seed)
    ks = iter(jax.random.split(key, 40))
    f32 = jnp.float32

    def nrm(shape, scale=1.0):
        return jax.random.normal(next(ks), shape, f32) * scale

    a_win = min(A_LEFT_CHUNKS * CHUNK, PAST_LEN)
    ds = D_MODEL ** -0.5
    hw_a, hw_b = H_A * HEAD_DIM, H_B * HEAD_DIM
    even_scale = jnp.concatenate([jnp.ones((2 * hw_a,), f32), jnp.full((hw_a,), BETA, f32),
                                  jnp.ones((2 * hw_b,), f32), jnp.full((hw_b,), BETA, f32)])
    kvw = C_KV_HEADS * HEAD_DIM
    odd_scale = jnp.concatenate([jnp.ones((H_C * HEAD_DIM + kvw,), f32), jnp.full((kvw,), BETA, f32),
                                 jnp.ones((ODD_IN - H_C * HEAD_DIM - 2 * kvw,), f32)])
    mem_scale = jnp.concatenate([jnp.ones((MEM_W,), f32), jnp.full((MEM_W,), BETA, f32)])
    return {
        "x_prompt": nrm((BATCH, SEQ, D_MODEL)),
        "x_sample": nrm((DEC_BATCH, DEC_SEQ, D_MODEL)),
        "cache_a_k": nrm((N_EVEN, DEC_BATCH, a_win, H_A, HEAD_DIM)),
        "cache_a_v": nrm((N_EVEN, DEC_BATCH, a_win, H_A, HEAD_DIM)),
        "cache_b_k": nrm((N_EVEN, DEC_BATCH, PAST_LEN, H_B, HEAD_DIM)),
        "cache_b_v": nrm((N_EVEN, DEC_BATCH, PAST_LEN, H_B, HEAD_DIM)),
        "cache_c_k": nrm((N_ODD, DEC_BATCH, PAST_LEN, C_KV_HEADS, HEAD_DIM)),
        "cache_c_v": nrm((N_ODD, DEC_BATCH, PAST_LEN, C_KV_HEADS, HEAD_DIM)),
        "cache_c_idx": nrm((N_ODD, DEC_BATCH, PAST_LEN, IDX_DIM)),
        "cache_mem_k": nrm((DEPTH, DEC_BATCH, N_MEM, MEM_HEADS, HEAD_DIM)),
        "cache_mem_v": nrm((DEPTH, DEC_BATCH, N_MEM, MEM_HEADS, HEAD_DIM)),
        "state_ffn_conv": nrm((DEPTH, DEC_BATCH, CONV_W - 1, 2 * D_FF)),
        "mem_prompt": nrm((BATCH, N_MEM, D_MODEL)),
        "w_in_even": nrm((N_EVEN, D_MODEL, EVEN_IN), ds) * even_scale,
        "w_out_even": nrm((N_EVEN, D_MODEL, D_MODEL), ds * BETA),
        "rel_bias_a": nrm((N_EVEN, H_A, 2 * REL_CLIP + 1), 0.1),
        "w_in_odd": nrm((N_ODD, D_MODEL, ODD_IN), ds) * odd_scale,
        "w_out_odd": nrm((N_ODD, D_MODEL, D_MODEL), ds * BETA),
        "w_mem_q": nrm((DEPTH, D_MODEL, MEM_W), ds),
        "w_mem_kv": nrm((DEPTH, D_MODEL, 2 * MEM_W), ds) * mem_scale,
        "w_mem_o": nrm((DEPTH, MEM_W, D_MODEL), MEM_W ** -0.5 * BETA),
        "w_up": nrm((DEPTH, D_MODEL, 2 * D_FF), ds),
        "conv_w": nrm((DEPTH, CONV_W, 2 * D_FF), CONV_W ** -0.5),
        "conv_b": nrm((DEPTH, 2 * D_FF), 0.02),
        "w_down": nrm((DEPTH, D_FF, D_MODEL), D_FF ** -0.5 * BETA),
        "ln_g": 1.0 + nrm((DEPTH, 3, D_MODEL), 0.05),
        "ln_b": nrm((DEPTH, 3, D_MODEL), 0.02),
    }


def reference(x_prompt, x_sample, cache_a_k, cache_a_v, cache_b_k, cache_b_v, cache_c_k, cache_c_v,
              cache_c_idx, cache_mem_k, cache_mem_v, state_ffn_conv, mem_prompt,
              w_in_even, w_out_even, rel_bias_a, w_in_odd, w_out_odd, w_mem_q, w_mem_kv, w_mem_o,
              w_up, conv_w, conv_b, w_down, ln_g, ln_b):
    slopes = _alibi_slopes(H_C)
    xp, xs = x_prompt, x_sample
    Bp, Sp = xp.shape[0], xp.shape[1]
    n_new = xs.shape[1]
    a_win_s = cache_a_k.shape[2]
    a_win_p = min(A_LEFT_CHUNKS * CHUNK, Sp)
    a_k_p, a_v_p, b_k_p, b_v_p, c_k_p, c_v_p, c_i_p, m_k_p, m_v_p, f_p = ([] for _ in range(10))
    a_k_s, a_v_s, b_k_s, b_v_s, c_k_s, c_v_s, c_i_s, f_s = ([] for _ in range(8))
    for l in range(DEPTH):
        i = l // 2
        if l % 2 == 0:
            qa, ka, va, qb, kb, vb = _even_proj(xp, w_in_even[i])
            oa = _band_attn_prompt(qa, ka, va, rel_bias_a[i])
            ob = _stick_breaking_prompt(qb, kb, vb)
            mix_p = jnp.concatenate([oa, ob], axis=-1) @ w_out_even[i]
            a_k_p.append(ka[:, -a_win_p:])
            a_v_p.append(va[:, -a_win_p:])
            b_k_p.append(kb)
            b_v_p.append(vb)
            qa, ka, va, qb, kb, vb = _even_proj(xs, w_in_even[i])
            ka_all = jnp.concatenate([cache_a_k[i], ka], axis=1)
            va_all = jnp.concatenate([cache_a_v[i], va], axis=1)
            kb_all = jnp.concatenate([cache_b_k[i], kb], axis=1)
            vb_all = jnp.concatenate([cache_b_v[i], vb], axis=1)
            oa = _band_attn_sample(qa, ka_all, va_all, rel_bias_a[i])
            ob = _stick_breaking_sample(qb, kb_all, vb_all)
            mix_s = jnp.concatenate([oa, ob], axis=-1) @ w_out_even[i]
            a_k_s.append(ka_all[:, -a_win_s:])
            a_v_s.append(va_all[:, -a_win_s:])
            b_k_s.append(kb)
            b_v_s.append(vb)
        else:
            q, k, v, qi, ki, wi = _odd_proj(xp, w_in_odd[i])
            mix_p = _dsa_prompt(q, k, v, qi, ki, wi, slopes) @ w_out_odd[i]
            c_k_p.append(k)
            c_v_p.append(v)
            c_i_p.append(ki)
            q, k, v, qi, ki, wi = _odd_proj(xs, w_in_odd[i])
            k_all = jnp.concatenate([cache_c_k[i], k], axis=1)
            v_all = jnp.concatenate([cache_c_v[i], v], axis=1)
            ki_all = jnp.concatenate([cache_c_idx[i], ki], axis=1)
            L = k_all.shape[1]
            qpos = (L - n_new) + jnp.arange(n_new)
            mix_s = _dsa_attend(q, qi, wi, qpos, k_all, v_all, ki_all,
                                min(TOPK_MAX, L // 4), slopes) @ w_out_odd[i]
            c_k_s.append(k)
            c_v_s.append(v)
            c_i_s.append(ki)
        xp = _layernorm(ALPHA * xp + mix_p, ln_g[l, 0], ln_b[l, 0])
        xs = _layernorm(ALPHA * xs + mix_s, ln_g[l, 0], ln_b[l, 0])
        mk_p, mv_p = _mem_kv(mem_prompt, w_mem_kv[l])
        m_k_p.append(mk_p)
        m_v_p.append(mv_p)
        xp = _layernorm(ALPHA * xp + _mem_attn(xp, mk_p, mv_p, w_mem_q[l], w_mem_o[l]), ln_g[l, 1], ln_b[l, 1])
        xs = _layernorm(ALPHA * xs + _mem_attn(xs, cache_mem_k[l], cache_mem_v[l], w_mem_q[l], w_mem_o[l]),
                        ln_g[l, 1], ln_b[l, 1])
        prev_p = jnp.zeros((Bp, CONV_W - 1, 2 * D_FF), xp.dtype)
        yp, st_p = _conv_ffn(xp, prev_p, w_up[l], conv_w[l], conv_b[l], w_down[l])
        ys, st_s = _conv_ffn(xs, state_ffn_conv[l], w_up[l], conv_w[l], conv_b[l], w_down[l])
        f_p.append(st_p)
        f_s.append(st_s)
        xp = _layernorm(ALPHA * xp + yp, ln_g[l, 2], ln_b[l, 2])
        xs = _layernorm(ALPHA * xs + ys, ln_g[l, 2], ln_b[l, 2])
    st = jnp.stack
    return (xp, xs,
            st(a_k_p), st(a_v_p), st(b_k_p), st(b_v_p), st(c_k_p), st(c_v_p), st(c_i_p),
            st(m_k_p), st(m_v_p), st(f_p),
            st(a_k_s), st(a_v_s), st(b_k_s), st(b_v_s), st(c_k_s), st(c_v_s), st(c_i_s), st(f_s))
```

```python
import functools

import numpy as np
import jax
import jax.numpy as jnp
from jax import lax
from jax.experimental import pallas as pl
from jax.experimental.pallas import tpu as pltpu

BF16 = jnp.bfloat16
F32 = jnp.float32

CHUNK = 64
HEAD_DIM = 128
A_LEFT_CHUNKS = 8
REL_CLIP = 128
C_GROUP = 4
IDX_HEADS = 32
IDX_DIM = 64
TOPK_MAX = 256
MEM_HEADS = 4
CONV_W = 3
DEPTH = 4
ALPHA = (2.0 * DEPTH) ** 0.25
LN_EPS = 1e-5
ATTN_SCALE = HEAD_DIM ** -0.5
IDX_SCALE = IDX_DIM ** -0.5
IDX_W_SCALE = IDX_HEADS ** -0.5
NEG_INF = -1e30

V7X_VMEM_BYTES = 64 * 1024 * 1024
LANES = 128
SB_EXIT = 120.0
INT_MIN = -(2 ** 31)


def _cparams(vmem_bytes):
    limit = int(min(max(vmem_bytes * 1.25 + (4 << 20), 32 << 20), V7X_VMEM_BYTES - (6 << 20)))
    return pltpu.CompilerParams(vmem_limit_bytes=limit)


def _nt_dot(a, b):
    return lax.dot_general(a, b, (((1,), (1,)), ((), ())), preferred_element_type=F32)


def _mm_body(x_ref, w_ref, *refs, n_k):
    out_refs, acc_ref = refs[:-1], refs[-1]
    k = pl.program_id(2)

    @pl.when(k == 0)
    def _():
        acc_ref[...] = jnp.zeros_like(acc_ref)

    acc_ref[...] += jnp.dot(x_ref[...].astype(BF16), w_ref[...], preferred_element_type=F32)

    @pl.when(k == n_k - 1)
    def _():
        for o in out_refs:
            o[...] = acc_ref[...].astype(o.dtype)


def _matmul(x, w, out_dtypes, tm, tn, tk):
    m, kd = x.shape
    n = w.shape[1]
    tm, tn, tk = min(tm, m), min(tn, n), min(tk, kd)
    assert m % tm == 0 and n % tn == 0 and kd % tk == 0, (x.shape, w.shape, tm, tn, tk)
    n_k = kd // tk
    vmem = 2 * (tm * tk * x.dtype.itemsize + tk * tn * 2) + tm * tn * 4
    vmem += sum(2 * tm * tn * jnp.dtype(d).itemsize for d in out_dtypes)
    return pl.pallas_call(
        functools.partial(_mm_body, n_k=n_k),
        grid=(m // tm, n // tn, n_k),
        in_specs=[pl.BlockSpec((tm, tk), lambda i, j, k: (i, k)),
                  pl.BlockSpec((tk, tn), lambda i, j, k: (k, j))],
        out_specs=[pl.BlockSpec((tm, tn), lambda i, j, k: (i, j)) for _ in out_dtypes],
        out_shape=[jax.ShapeDtypeStruct((m, n), d) for d in out_dtypes],
        scratch_shapes=[pltpu.VMEM((tm, tn), F32)],
        compiler_params=_cparams(vmem),
        name="matmul",
    )(x, w)


LN_ROWS = 32


def _residual_ln_rows(y_ref, res_ref, g_ref, b_ref, o_ref, ob_ref, rows):
    g = g_ref[...]
    b = b_ref[...]

    def chunk(c, carry):
        sl = pl.ds(pl.multiple_of(c * LN_ROWS, LN_ROWS), LN_ROWS)
        y = ALPHA * res_ref[sl, :] + y_ref[sl, :]
        mu = jnp.mean(y, axis=-1, keepdims=True)
        d = y - mu
        var = jnp.mean(d * d, axis=-1, keepdims=True)
        out = d * lax.rsqrt(var + LN_EPS) * g + b
        o_ref[sl, :] = out
        ob_ref[sl, :] = out.astype(BF16)
        return carry

    lax.fori_loop(0, rows // LN_ROWS, chunk, 0)


def _mm_ln_body(x_ref, w_ref, res_ref, g_ref, b_ref, o_ref, ob_ref, *, n_k, tm):
    k = pl.program_id(1)
    part = jnp.dot(x_ref[...], w_ref[...], preferred_element_type=F32)

    @pl.when(k == 0)
    def _():
        o_ref[...] = part

    @pl.when(k > 0)
    def _():
        o_ref[...] += part

    @pl.when(k == n_k - 1)
    def _():
        _residual_ln_rows(o_ref, res_ref, g_ref, b_ref, o_ref, ob_ref, tm)


def _matmul_residual_ln(x, w, res, g, b, tm, tk):
    m, kd = x.shape
    n = w.shape[1]
    tm, tk = min(tm, m), min(tk, kd)
    assert m % tm == 0 and kd % tk == 0 and tm % LN_ROWS == 0
    n_k = kd // tk
    vmem = 2 * (tm * tk * 2 + tk * n * 2 + tm * n * 4 + tm * n * 4 + tm * n * 2)
    return pl.pallas_call(
        functools.partial(_mm_ln_body, n_k=n_k, tm=tm),
        grid=(m // tm, n_k),
        in_specs=[pl.BlockSpec((tm, tk), lambda i, k: (i, k)),
                  pl.BlockSpec((tk, n), lambda i, k: (k, 0)),
                  pl.BlockSpec((tm, n), lambda i, k: (i, 0)),
                  pl.BlockSpec((1, n), lambda i, k: (0, 0)),
                  pl.BlockSpec((1, n), lambda i, k: (0, 0))],
        out_specs=[pl.BlockSpec((tm, n), lambda i, k: (i, 0)),
                   pl.BlockSpec((tm, n), lambda i, k: (i, 0))],
        out_shape=[jax.ShapeDtypeStruct((m, n), F32), jax.ShapeDtypeStruct((m, n), BF16)],
        compiler_params=_cparams(vmem),
        name="matmul_residual_ln",
    )(x, w, res, g.reshape(1, n), b.reshape(1, n))


BAND_TQ = 128
BAND_KB = 5


def _band_prompt_body(q_ref, k_ref, v_ref, bias_ref, o_ref):
    t = pl.program_id(2)
    q = q_ref[...]
    scores = []
    for jb in range(BAND_KB):
        kb = t - (BAND_KB - 1) + jb
        start = pl.multiple_of(jnp.maximum(kb, 0) * BAND_TQ, BAND_TQ)
        s = _nt_dot(q, k_ref[pl.ds(start, BAND_TQ), :]) * ATTN_SCALE
        s = s + bias_ref[0, :, jb * BAND_TQ:(jb + 1) * BAND_TQ]
        scores.append(jnp.where(kb >= 0, s, NEG_INF))
    m = functools.reduce(jnp.maximum, [jnp.max(s, axis=-1, keepdims=True) for s in scores])
    l = jnp.zeros((BAND_TQ, 1), F32)
    acc = jnp.zeros((BAND_TQ, HEAD_DIM), F32)
    for jb in range(BAND_KB):
        kb = t - (BAND_KB - 1) + jb
        start = pl.multiple_of(jnp.maximum(kb, 0) * BAND_TQ, BAND_TQ)
        p = jnp.exp(scores[jb] - m)
        l = l + jnp.sum(p, axis=-1, keepdims=True)
        acc = acc + jnp.dot(p.astype(BF16), v_ref[pl.ds(start, BAND_TQ), :], preferred_element_type=F32)
    o_ref[...] = (acc / l).astype(o_ref.dtype)


def _band_bias_prompt(rel_bias):
    left = A_LEFT_CHUNKS * CHUNK
    qi = np.arange(BAND_TQ)[:, None]
    kj = np.arange(BAND_KB * BAND_TQ)[None, :]
    rel = qi + left - kj
    in_band = (kj // CHUNK >= qi // CHUNK) & (kj // CHUNK <= qi // CHUNK + A_LEFT_CHUNKS)
    bias = rel_bias[:, np.clip(rel, -REL_CLIP, REL_CLIP) + REL_CLIP]
    return jnp.where(jnp.asarray(in_band)[None], bias, NEG_INF).astype(F32)


def _band_attn_prompt(qkv, rel_bias, batch, seq, n_heads, q_col, k_col, v_col):
    nt = seq // BAND_TQ
    bias = _band_bias_prompt(rel_bias)
    vmem = 2 * (2 * seq * HEAD_DIM * 2 + BAND_TQ * BAND_KB * BAND_TQ * 4) + (4 << 20)
    return pl.pallas_call(
        _band_prompt_body,
        grid=(batch, n_heads, nt),
        in_specs=[pl.BlockSpec((BAND_TQ, HEAD_DIM), lambda b, h, t: (b * nt + t, q_col + h)),
                  pl.BlockSpec((seq, HEAD_DIM), lambda b, h, t: (b, k_col + h)),
                  pl.BlockSpec((seq, HEAD_DIM), lambda b, h, t: (b, v_col + h)),
                  pl.BlockSpec((1, BAND_TQ, BAND_KB * BAND_TQ), lambda b, h, t: (h, 0, 0))],
        out_specs=pl.BlockSpec((BAND_TQ, HEAD_DIM), lambda b, h, t: (b * nt + t, h)),
        out_shape=jax.ShapeDtypeStruct((batch * seq, n_heads * HEAD_DIM), BF16),
        compiler_params=_cparams(vmem),
        name="band_attn_prompt",
    )(qkv, qkv, qkv, bias)


def _bias_attn_body(q_ref, k_ref, v_ref, bias_ref, o_ref):
    s = _nt_dot(q_ref[...], k_ref[...]) * ATTN_SCALE + bias_ref[0]
    m = jnp.max(s, axis=-1, keepdims=True)
    p = jnp.exp(s - m)
    l = jnp.sum(p, axis=-1, keepdims=True)
    acc = jnp.dot(p.astype(BF16), v_ref[...], preferred_element_type=F32)
    o_ref[...] = (acc / l).astype(o_ref.dtype)


def _band_attn_sample(q2d, k_all, v_all, rel_bias, batch, n_new, n_heads, q_col):
    L = k_all.shape[0] // batch
    w = L - n_new
    rel = np.arange(n_new)[:, None] + w - np.arange(L)[None, :]
    bias = rel_bias[:, np.clip(rel, -REL_CLIP, REL_CLIP) + REL_CLIP].astype(F32)
    return pl.pallas_call(
        _bias_attn_body,
        grid=(batch, n_heads),
        in_specs=[pl.BlockSpec((n_new, HEAD_DIM), lambda b, h: (b, q_col + h)),
                  pl.BlockSpec((L, HEAD_DIM), lambda b, h: (b, h)),
                  pl.BlockSpec((L, HEAD_DIM), lambda b, h: (b, h)),
                  pl.BlockSpec((1, n_new, L), lambda b, h: (h, 0, 0))],
        out_specs=pl.BlockSpec((n_new, HEAD_DIM), lambda b, h: (b, h)),
        out_shape=jax.ShapeDtypeStruct((batch * n_new, n_heads * HEAD_DIM), BF16),
        name="band_attn_sample",
    )(q2d, k_all, v_all, bias)


SB_TK = 128


def _split3(x):
    hi = x.astype(BF16)
    r = x - hi.astype(F32)
    mid = r.astype(BF16)
    lo = (r - mid.astype(F32)).astype(BF16)
    return hi, mid, lo


def _stick_break_body(q_ref, k_ref, v_ref, u_ref, o_ref, *, tq, qpos_base):
    qt = pl.program_id(2)
    qpos0 = qpos_base + qt * tq
    q = q_ref[...]
    u = u_ref[...]
    rowpos = qpos0 + lax.broadcasted_iota(jnp.int32, (tq, 1), 0)
    kb0 = (qpos0 + tq - 1) // SB_TK

    def cond(c):
        kb, top = c[0], c[1]
        return jnp.logical_and(kb >= 0, top > -SB_EXIT)

    def body(c):
        kb, _, carry, acc = c
        start = pl.multiple_of(kb * SB_TK, SB_TK)
        z = _nt_dot(q, k_ref[pl.ds(start, SB_TK), :]) * ATTN_SCALE
        sp = jnp.maximum(z, 0.0) + jnp.log(1.0 + jnp.exp(-jnp.abs(z)))
        kpos = start + lax.broadcasted_iota(jnp.int32, (1, SB_TK), 1)
        mask = kpos < rowpos
        log_1m = jnp.where(mask, -sp, 0.0)
        suffix = sum(jnp.dot(piece, u, preferred_element_type=F32) for piece in _split3(log_1m))
        w = jnp.where(mask, jnp.exp(z - sp + suffix + carry), 0.0)
        acc = acc + jnp.dot(w.astype(BF16), v_ref[pl.ds(start, SB_TK), :], preferred_element_type=F32)
        carry = carry + jnp.sum(log_1m, axis=-1, keepdims=True)
        return kb - 1, jnp.max(carry), carry, acc

    init = (kb0, jnp.float32(0.0), jnp.zeros((tq, 1), F32), jnp.zeros((tq, HEAD_DIM), F32))
    acc = lax.while_loop(cond, body, init)[3]
    o_ref[...] = acc.astype(o_ref.dtype)


def _stick_break_attn(q2d, k2d, v2d, batch, n_q, n_k, n_heads, tq, qpos_base, q_col, k_col, v_col):
    assert n_q % tq == 0 and n_k % SB_TK == 0
    nt = n_q // tq
    j = np.arange(SB_TK)
    u = jnp.asarray(j[:, None] > j[None, :], BF16)
    vmem = 2 * (2 * n_k * HEAD_DIM * 2) + (4 << 20)
    return pl.pallas_call(
        functools.partial(_stick_break_body, tq=tq, qpos_base=qpos_base),
        grid=(batch, n_heads, nt),
        in_specs=[pl.BlockSpec((tq, HEAD_DIM), lambda b, h, t: (b * nt + t, q_col + h)),
                  pl.BlockSpec((n_k, HEAD_DIM), lambda b, h, t: (b, k_col + h)),
                  pl.BlockSpec((n_k, HEAD_DIM), lambda b, h, t: (b, v_col + h)),
                  pl.BlockSpec((SB_TK, SB_TK), lambda b, h, t: (0, 0))],
        out_specs=pl.BlockSpec((tq, HEAD_DIM), lambda b, h, t: (b * nt + t, h)),
        out_shape=jax.ShapeDtypeStruct((batch * n_q, n_heads * HEAD_DIM), BF16),
        compiler_params=_cparams(vmem),
        name="stick_break_attn",
    )(q2d, k2d, v2d, u)


def _count(pred):
    return jnp.sum(pred.astype(F32), axis=-1, keepdims=True)


def _dsa_index_body(qi_ref, wi_ref, kz_ref, bias_ref, idx_ref, *, tq, tk, n_real, qpos_base, topk):
    n_pad = idx_ref.shape[1]
    qt = pl.program_id(1)
    qpos0 = qpos_base + qt * tq
    rowpos = qpos0 + lax.broadcasted_iota(jnp.int32, (tq, 1), 0)
    qchunk = rowpos // CHUNK
    n_kb = jnp.minimum((((qpos0 + tq - 1) // CHUNK + 1) * CHUNK + tk - 1) // tk, n_pad // tk)

    idx_ref[...] = jnp.full(idx_ref.shape, NEG_INF, F32)
    w = wi_ref[...] * (IDX_SCALE * IDX_W_SCALE)

    def kb_body(kb, carry):
        start = pl.multiple_of(kb * tk, tk)
        kz = kz_ref[pl.ds(start, tk), :]
        acc = jnp.zeros((tq, tk), F32)
        for pair in range(IDX_HEADS // 2):
            qp = qi_ref[:, pair * LANES:(pair + 1) * LANES]
            for half in range(2):
                h = 2 * pair + half
                sc = _nt_dot(qp, kz[:, half * LANES:(half + 1) * LANES])
                acc = acc + jnp.maximum(sc, 0.0) * w[:, h:h + 1]
        kpos = start + lax.broadcasted_iota(jnp.int32, (1, tk), 1)
        idx_ref[:, pl.ds(start, tk)] = jnp.where(kpos // CHUNK <= qchunk, acc, NEG_INF)
        return carry

    lax.fori_loop(0, n_kb, kb_body, 0)

    bits = pltpu.bitcast(idx_ref[...], jnp.int32)
    skey = jnp.where(bits < 0, bits ^ jnp.int32(0x7FFFFFFF), bits)
    pos = lax.broadcasted_iota(jnp.int32, (1, n_pad), 1)
    skey = jnp.where(pos < n_real, skey, jnp.int32(INT_MIN))
    kf = jnp.float32(topk)

    def thr_step(i, ans_u):
        cand_u = ans_u | lax.shift_left(jnp.int32(1), 31 - i)
        cnt = _count(skey >= (cand_u ^ jnp.int32(INT_MIN)))
        return jnp.where(cnt >= kf, cand_u, ans_u)

    thr = lax.fori_loop(0, 32, thr_step, jnp.zeros((tq, 1), jnp.int32)) ^ jnp.int32(INT_MIN)
    gt = skey > thr
    eq = skey == thr
    need = kf - _count(gt)

    n_bits = int(np.ceil(np.log2(n_pad)))

    def pos_step(i, lo):
        cand = lo + lax.shift_left(jnp.int32(1), n_bits - 1 - i)
        cnt = _count(jnp.logical_and(eq, pos < cand))
        return jnp.where(cnt < need, cand, lo)

    last = lax.fori_loop(0, n_bits, pos_step, jnp.zeros((tq, 1), jnp.int32))
    sel = jnp.logical_or(gt, jnp.logical_and(eq, pos <= last))
    ok = jnp.logical_and(jnp.logical_and(sel, pos // CHUNK <= qchunk), pos < n_real)
    bias_ref[0] = jnp.where(ok, 0.0, NEG_INF).astype(bias_ref.dtype)


def _dsa_index(qi2d, wi, kz, batch, n_q, n_k, n_real, tq, tk, qpos_base, qi_col):
    nt = n_q // tq
    topk = min(TOPK_MAX, n_real // 4)
    vmem = 2 * (tq * 2048 * 2 + n_k * 256 * 2 + tq * n_k * 2) + 6 * tq * n_k * 4 + (4 << 20)
    return pl.pallas_call(
        functools.partial(_dsa_index_body, tq=tq, tk=tk, n_real=n_real, qpos_base=qpos_base, topk=topk),
        grid=(batch, nt),
        in_specs=[pl.BlockSpec((tq, IDX_HEADS * IDX_DIM), lambda b, t: (b * nt + t, qi_col)),
                  pl.BlockSpec((tq, IDX_HEADS), lambda b, t: (b * nt + t, 0)),
                  pl.BlockSpec((n_k, 2 * LANES), lambda b, t: (b, 0))],
        out_specs=pl.BlockSpec((1, tq, n_k), lambda b, t: (b, t, 0)),
        out_shape=jax.ShapeDtypeStruct((batch, n_q, n_k), BF16),
        scratch_shapes=[pltpu.VMEM((tq, n_k), F32)],
        compiler_params=_cparams(vmem),
        name="dsa_index",
    )(qi2d, wi, kz)


def _dsa_attn_body(q_ref, k_ref, v_ref, bias_ref, slope_ref, o_ref, *, tq, tk, qpos_base):
    n_pad = k_ref.shape[0]
    qt = pl.program_id(2)
    qpos0 = qpos_base + qt * tq
    rows = C_GROUP * tq
    qs = jnp.concatenate([q_ref[:, g * HEAD_DIM:(g + 1) * HEAD_DIM] for g in range(C_GROUP)], axis=0)
    slope = slope_ref[0]
    rowpos = qpos0 + lax.broadcasted_iota(jnp.int32, (rows, 1), 0) % tq
    n_kb = jnp.minimum((((qpos0 + tq - 1) // CHUNK + 1) * CHUNK + tk - 1) // tk, n_pad // tk)

    def kb_body(kb, c):
        m, l, acc = c
        start = pl.multiple_of(kb * tk, tk)
        s = _nt_dot(qs, k_ref[pl.ds(start, tk), :]) * ATTN_SCALE
        kpos = start + lax.broadcasted_iota(jnp.int32, (1, tk), 1)
        dist = jnp.abs(rowpos - kpos).astype(F32)
        bias = bias_ref[0, :, pl.ds(start, tk)].astype(F32)
        s = s - slope * dist + jnp.concatenate([bias] * C_GROUP, axis=0)
        m_new = jnp.maximum(m, jnp.max(s, axis=-1, keepdims=True))
        a = jnp.exp(m - m_new)
        p = jnp.exp(s - m_new)
        l = a * l + jnp.sum(p, axis=-1, keepdims=True)
        acc = a * acc + jnp.dot(p.astype(BF16), v_ref[pl.ds(start, tk), :], preferred_element_type=F32)
        return m_new, l, acc

    init = (jnp.full((rows, 1), 0.5 * NEG_INF, F32), jnp.zeros((rows, 1), F32),
            jnp.zeros((rows, HEAD_DIM), F32))
    _, l, acc = lax.fori_loop(0, n_kb, kb_body, init)
    o = (acc / l).astype(o_ref.dtype)
    for g in range(C_GROUP):
        o_ref[:, g * HEAD_DIM:(g + 1) * HEAD_DIM] = o[g * tq:(g + 1) * tq]


def _dsa_attn(q2d, k2d, v2d, bias, batch, n_q, n_k, n_kv, tq, tk, qpos_base, k_col, v_col):
    nt = n_q // tq
    gw = C_GROUP * HEAD_DIM
    n_heads = n_kv * C_GROUP
    slopes = 2.0 ** (-8.0 * np.arange(1, n_heads + 1) / n_heads)
    slope_rows = jnp.asarray(np.repeat(slopes.reshape(n_kv, C_GROUP), tq, axis=1)[..., None], F32)
    vmem = 2 * (2 * n_k * HEAD_DIM * 2 + tq * n_k * 2 + 2 * tq * gw * 2) + 8 * C_GROUP * tq * tk * 4 + (4 << 20)
    return pl.pallas_call(
        functools.partial(_dsa_attn_body, tq=tq, tk=tk, qpos_base=qpos_base),
        grid=(batch, n_kv, nt),
        in_specs=[pl.BlockSpec((tq, gw), lambda b, n, t: (b * nt + t, n)),
                  pl.BlockSpec((n_k, HEAD_DIM), lambda b, n, t: (b, k_col + n)),
                  pl.BlockSpec((n_k, HEAD_DIM), lambda b, n, t: (b, v_col + n)),
                  pl.BlockSpec((1, tq, n_k), lambda b, n, t: (b, t, 0)),
                  pl.BlockSpec((1, C_GROUP * tq, 1), lambda b, n, t: (n, 0, 0))],
        out_specs=pl.BlockSpec((tq, gw), lambda b, n, t: (b * nt + t, n)),
        out_shape=jax.ShapeDtypeStruct((batch * n_q, n_heads * HEAD_DIM), BF16),
        compiler_params=_cparams(vmem),
        name="dsa_attn",
    )(q2d, k2d, v2d, bias, slope_rows)


def _mem_attn_body(x_ref, wq_ref, mk_ref, mv_ref, wo_ref, g_ref, b_ref, o_ref, ob_ref, y_ref, *,
                   tm, rows_per_seq, n_mem):
    q = jnp.dot(x_ref[...].astype(BF16), wq_ref[...], preferred_element_type=F32).astype(BF16)
    for s_i in range(tm // rows_per_seq):
        rsl = slice(s_i * rows_per_seq, (s_i + 1) * rows_per_seq)
        msl = slice(s_i * n_mem, (s_i + 1) * n_mem)
        heads = []
        for h in range(MEM_HEADS):
            hsl = slice(h * HEAD_DIM, (h + 1) * HEAD_DIM)
            s = _nt_dot(q[rsl, hsl], mk_ref[msl, hsl].astype(BF16)) * ATTN_SCALE
            m = jnp.max(s, axis=-1, keepdims=True)
            p = jnp.exp(s - m)
            l = jnp.sum(p, axis=-1, keepdims=True)
            o = jnp.dot(p.astype(BF16), mv_ref[msl, hsl].astype(BF16), preferred_element_type=F32)
            heads.append((o / l).astype(BF16))
        o_all = jnp.concatenate(heads, axis=-1)
        y_ref[rsl, :] = jnp.dot(o_all, wo_ref[...], preferred_element_type=F32)
    _residual_ln_rows(y_ref, x_ref, g_ref, b_ref, o_ref, ob_ref, tm)


def _mem_attn_ln(x, wq, mk, mv, wo, g, b, tm, rows_per_seq, seq_rows):
    m, d = x.shape
    mw = wq.shape[1]
    n_seq_tile = tm // rows_per_seq
    n_mem = mk.shape[0] // (m // seq_rows)
    if n_seq_tile == 1:
        mem_map = lambda i: ((i * tm) // seq_rows, 0)
    else:
        assert rows_per_seq == seq_rows
        mem_map = lambda i: (i, 0)
    vmem = 2 * (tm * d * 4 * 2 + tm * d * 2 + 2 * d * mw * 2 + 2 * n_seq_tile * n_mem * mw * 4) + tm * d * 4
    return pl.pallas_call(
        functools.partial(_mem_attn_body, tm=tm, rows_per_seq=rows_per_seq, n_mem=n_mem),
        grid=(m // tm,),
        in_specs=[pl.BlockSpec((tm, d), lambda i: (i, 0)),
                  pl.BlockSpec((d, mw), lambda i: (0, 0)),
                  pl.BlockSpec((n_seq_tile * n_mem, mw), mem_map),
                  pl.BlockSpec((n_seq_tile * n_mem, mw), mem_map),
                  pl.BlockSpec((mw, d), lambda i: (0, 0)),
                  pl.BlockSpec((1, d), lambda i: (0, 0)),
                  pl.BlockSpec((1, d), lambda i: (0, 0))],
        out_specs=[pl.BlockSpec((tm, d), lambda i: (i, 0)),
                   pl.BlockSpec((tm, d), lambda i: (i, 0))],
        out_shape=[jax.ShapeDtypeStruct((m, d), F32), jax.ShapeDtypeStruct((m, d), BF16)],
        scratch_shapes=[pltpu.VMEM((tm, d), F32)],
        compiler_params=_cparams(vmem),
        name="mem_attn_ln",
    )(x, wq, mk, mv, wo, g.reshape(1, d), b.reshape(1, d))


FFN_TN = 256
HALO = 8


def _gelu(x):
    return 0.5 * x * (1.0 + lax.erf(x * np.float32(2.0 ** -0.5)))


def _ffn_up_body(x_ref, wa_ref, wg_ref, cwa_ref, cwg_ref, cba_ref, cbg_ref, *rest,
                 tm, rows_per_seq, tiles_per_seq, carried, nj):
    if carried:
        act_ref, sta_ref, stg_ref, carry_ref, ext_ref = rest
    else:
        pa_ref, pg_ref, act_ref, sta_ref, stg_ref, ext_ref = rest
    i = pl.program_id(0)
    j = pl.program_id(1)
    n_seq = tm // rows_per_seq
    stride = rows_per_seq + HALO

    def half(part, w_ref, cw_ref, cb_ref, st_ref):
        up = jnp.dot(x_ref[...], w_ref[...], preferred_element_type=F32)
        pieces = []
        for s_i in range(n_seq):
            base = s_i * stride + HALO
            last2 = slice(base + rows_per_seq - 2, base + rows_per_seq)
            if carried:
                @pl.when(i % tiles_per_seq == 0)
                def _():
                    carry_ref[part, j] = jnp.zeros((2, FFN_TN), F32)

                prev = carry_ref[part, j]
            else:
                prev = (pa_ref, pg_ref)[part][s_i]
            ext_ref[part, base - 2:base, :] = prev
            ext_ref[part, base:base + rows_per_seq, :] = up[s_i * rows_per_seq:(s_i + 1) * rows_per_seq]
            if carried:
                carry_ref[part, j] = ext_ref[part, last2, :]
            st_ref[s_i] = ext_ref[part, last2, :]
            pieces.append(ext_ref[part, base:base + rows_per_seq, :] * cw_ref[2:3, :]
                          + ext_ref[part, base - 1:base - 1 + rows_per_seq, :] * cw_ref[1:2, :]
                          + ext_ref[part, base - 2:base - 2 + rows_per_seq, :] * cw_ref[0:1, :]
                          + cb_ref[...])
        return pieces[0] if n_seq == 1 else jnp.concatenate(pieces, axis=0)

    @pl.when(j < nj)
    def _():
        a = half(0, wa_ref, cwa_ref, cba_ref, sta_ref)
        g = half(1, wg_ref, cwg_ref, cbg_ref, stg_ref)
        act_ref[...] = (a * _gelu(g)).astype(act_ref.dtype)

    @pl.when(j >= nj)
    def _():
        act_ref[...] = jnp.zeros_like(act_ref)


def _ffn_up(x, w_up, conv_w, conv_b, prev, tm, rows_per_seq, seq_rows, n_act_cols):
    m, d = x.shape
    f2 = w_up.shape[1]
    ff = f2 // 2
    assert ff % FFN_TN == 0 and n_act_cols % FFN_TN == 0
    nj = ff // FFN_TN
    nj_pad = n_act_cols // FFN_TN
    carried = prev is None
    n_seq = tm // rows_per_seq
    tiles_per_seq = max(seq_rows // tm, 1)
    n_tiles = m // tm
    cb = conv_b.reshape(1, f2)
    col_a = lambda i, j: (0, jnp.minimum(j, nj - 1))
    col_g = lambda i, j: (0, jnp.minimum(j, nj - 1) + nj)
    in_specs = [pl.BlockSpec((tm, d), lambda i, j: (i, 0)),
                pl.BlockSpec((d, FFN_TN), col_a),
                pl.BlockSpec((d, FFN_TN), col_g),
                pl.BlockSpec((CONV_W, FFN_TN), col_a),
                pl.BlockSpec((CONV_W, FFN_TN), col_g),
                pl.BlockSpec((1, FFN_TN), col_a),
                pl.BlockSpec((1, FFN_TN), col_g)]
    args = [x, w_up, w_up, conv_w, conv_w, cb, cb]
    scratch = []
    if carried:
        assert n_seq == 1
        scratch.append(pltpu.VMEM((2, nj, 2, FFN_TN), F32))
    else:
        in_specs += [pl.BlockSpec((n_seq, 2, FFN_TN), lambda i, j: (i, 0, jnp.minimum(j, nj - 1))),
                     pl.BlockSpec((n_seq, 2, FFN_TN), lambda i, j: (i, 0, jnp.minimum(j, nj - 1) + nj))]
        args += [prev, prev]
    scratch.append(pltpu.VMEM((2, n_seq * (rows_per_seq + HALO), FFN_TN), F32))
    st_shape = jax.ShapeDtypeStruct((n_tiles * n_seq, 2, ff), F32)
    st_spec = pl.BlockSpec((n_seq, 2, FFN_TN), lambda i, j: (i, 0, jnp.minimum(j, nj - 1)))
    vmem = 2 * (tm * d * 2 + 2 * d * FFN_TN * 2 + tm * FFN_TN * 2) + 10 * tm * FFN_TN * 4 + (4 << 20)
    act, st_a, st_g = pl.pallas_call(
        functools.partial(_ffn_up_body, tm=tm, rows_per_seq=rows_per_seq, tiles_per_seq=tiles_per_seq,
                          carried=carried, nj=nj),
        grid=(n_tiles, nj_pad),
        in_specs=in_specs,
        out_specs=[pl.BlockSpec((tm, FFN_TN), lambda i, j: (i, j)), st_spec, st_spec],
        out_shape=[jax.ShapeDtypeStruct((m, n_act_cols), BF16), st_shape, st_shape],
        scratch_shapes=scratch,
        compiler_params=_cparams(vmem),
        name="ffn_up_conv_gate",
    )(*args)
    return act, jnp.concatenate([st_a, st_g], axis=-1)


def _pad_rows(a, n_rows):
    return jnp.pad(a, ((0, 0), (0, n_rows - a.shape[1]), (0, 0)))


def kernel(x_prompt, x_sample, cache_a_k, cache_a_v, cache_b_k, cache_b_v, cache_c_k, cache_c_v, cache_c_idx, cache_mem_k, cache_mem_v, state_ffn_conv, mem_prompt, w_in_even, w_out_even, rel_bias_a, w_in_odd, w_out_odd, w_mem_q, w_mem_kv, w_mem_o, w_up, conv_w, conv_b, w_down, ln_g, ln_b):
    bp, sp, d = x_prompt.shape
    bs, n_new, _ = x_sample.shape
    depth = w_up.shape[0]
    h_a = cache_a_k.shape[3]
    h_b = cache_b_k.shape[3]
    n_kv = cache_c_k.shape[3]
    h_c = n_kv * C_GROUP
    hw_a, hw_b, kvw = h_a * HEAD_DIM, h_b * HEAD_DIM, n_kv * HEAD_DIM
    qiw = IDX_HEADS * IDX_DIM
    a_win_s = cache_a_k.shape[2]
    a_win_p = min(A_LEFT_CHUNKS * CHUNK, sp)
    past = cache_b_k.shape[2]
    n_mem = mem_prompt.shape[1]
    mem_w = w_mem_q.shape[2]
    d_ff = w_down.shape[1]
    mp, ms = bp * sp, bs * n_new
    l_s = past + n_new
    l_s_pad = -(-l_s // LANES) * LANES
    main_odd = h_c * HEAD_DIM + 2 * kvw + qiw
    tail_pad = LANES

    xp = x_prompt.reshape(mp, d)
    xs = x_sample.reshape(ms, d)
    xp_b, xs_b = xp, xs
    mem2d = mem_prompt.reshape(bp * n_mem, d)

    outs = {k: [] for k in ("akp", "avp", "bkp", "bvp", "ckp", "cvp", "cip", "mkp", "mvp", "fp",
                            "aks", "avs", "bks", "bvs", "cks", "cvs", "cis", "fs")}

    for l in range(depth):
        i = l // 2
        if l % 2 == 0:
            w_in = w_in_even[i].astype(BF16)
            w_out = w_out_even[i].astype(BF16)
            pf, pb = _matmul(xp_b, w_in, (F32, BF16), 1024, 1024, 1024)
            hq = HEAD_DIM
            oa = _band_attn_prompt(pb, rel_bias_a[i], bp, sp, h_a, 0, hw_a // hq, 2 * hw_a // hq)
            cb0 = 3 * hw_a // hq
            ob = _stick_break_attn(pb, pb, pb, bp, sp, sp, h_b, 128, 0,
                                   cb0, cb0 + h_b, cb0 + 2 * h_b)
            mix_p = jnp.concatenate([oa, ob], axis=-1)
            pf4 = pf.reshape(bp, sp, -1)
            outs["akp"].append(pf4[:, sp - a_win_p:, hw_a:2 * hw_a].reshape(bp, a_win_p, h_a, HEAD_DIM))
            outs["avp"].append(pf4[:, sp - a_win_p:, 2 * hw_a:3 * hw_a].reshape(bp, a_win_p, h_a, HEAD_DIM))
            o0 = 3 * hw_a
            outs["bkp"].append(pf4[:, :, o0 + hw_b:o0 + 2 * hw_b].reshape(bp, sp, h_b, HEAD_DIM))
            outs["bvp"].append(pf4[:, :, o0 + 2 * hw_b:o0 + 3 * hw_b].reshape(bp, sp, h_b, HEAD_DIM))
            sf, sb = _matmul(xs_b, w_in, (F32, BF16), 256, 1024, 4096)
            sf4 = sf.reshape(bs, n_new, -1)
            ka_all = jnp.concatenate([cache_a_k[i].reshape(bs, a_win_s, hw_a), sf4[:, :, hw_a:2 * hw_a]], axis=1)
            va_all = jnp.concatenate([cache_a_v[i].reshape(bs, a_win_s, hw_a), sf4[:, :, 2 * hw_a:3 * hw_a]], axis=1)
            la = a_win_s + n_new
            oa_s = _band_attn_sample(sb, ka_all.astype(BF16).reshape(bs * la, hw_a),
                                     va_all.astype(BF16).reshape(bs * la, hw_a),
                                     rel_bias_a[i], bs, n_new, h_a, 0)
            kb_new = sf4[:, :, o0 + hw_b:o0 + 2 * hw_b]
            vb_new = sf4[:, :, o0 + 2 * hw_b:o0 + 3 * hw_b]
            kb_all = _pad_rows(jnp.concatenate([cache_b_k[i].reshape(bs, past, hw_b), kb_new], axis=1), l_s_pad)
            vb_all = _pad_rows(jnp.concatenate([cache_b_v[i].reshape(bs, past, hw_b), vb_new], axis=1), l_s_pad)
            ob_s = _stick_break_attn(sb, kb_all.astype(BF16).reshape(bs * l_s_pad, hw_b),
                                     vb_all.astype(BF16).reshape(bs * l_s_pad, hw_b),
                                     bs, n_new, l_s_pad, h_b, n_new, past, cb0, 0, 0)
            mix_s = jnp.concatenate([oa_s, ob_s], axis=-1)
            outs["aks"].append(ka_all[:, la - a_win_s:].reshape(bs, a_win_s, h_a, HEAD_DIM))
            outs["avs"].append(va_all[:, la - a_win_s:].reshape(bs, a_win_s, h_a, HEAD_DIM))
            outs["bks"].append(kb_new.reshape(bs, n_new, h_b, HEAD_DIM))
            outs["bvs"].append(vb_new.reshape(bs, n_new, h_b, HEAD_DIM))
        else:
            w_in = w_in_odd[i]
            w_main = w_in[:, :main_odd].astype(BF16)
            w_tail = jnp.pad(w_in[:, main_odd:], ((0, 0), (0, tail_pad - (IDX_DIM + IDX_HEADS)))).astype(BF16)
            w_out = w_out_odd[i].astype(BF16)
            kc0 = h_c
            vc0 = h_c + n_kv
            qic = (h_c * HEAD_DIM + 2 * kvw) // qiw
            zeros_idx = jnp.zeros((1, 1, IDX_DIM), F32)

            def kz_of(ki):
                z = jnp.broadcast_to(zeros_idx, ki.shape)
                return jnp.concatenate([ki, z, z, ki], axis=-1).astype(BF16).reshape(-1, 2 * LANES)

            pf, pb = _matmul(xp_b, w_main, (F32, BF16), 1024, 1024, 1024)
            (pt,) = _matmul(xp_b, w_tail, (F32,), 1024, tail_pad, 4096)
            ki_p = pt[:, :IDX_DIM].reshape(bp, sp, IDX_DIM)
            wi_p = pt[:, IDX_DIM:IDX_DIM + IDX_HEADS]
            sel_p = _dsa_index(pb, wi_p, kz_of(ki_p), bp, sp, sp, sp, 128, 512, 0, qic)
            mix_p = _dsa_attn(pb, pb, pb, sel_p, bp, sp, sp, n_kv, 128, 512, 0, kc0, vc0)
            pf4 = pf.reshape(bp, sp, -1)
            q_w = h_c * HEAD_DIM
            outs["ckp"].append(pf4[:, :, q_w:q_w + kvw].reshape(bp, sp, n_kv, HEAD_DIM))
            outs["cvp"].append(pf4[:, :, q_w + kvw:q_w + 2 * kvw].reshape(bp, sp, n_kv, HEAD_DIM))
            outs["cip"].append(ki_p)
            sf, sb = _matmul(xs_b, w_main, (F32, BF16), 256, 1024, 4096)
            (stl,) = _matmul(xs_b, w_tail, (F32,), 256, tail_pad, 4096)
            sf4 = sf.reshape(bs, n_new, -1)
            k_new = sf4[:, :, q_w:q_w + kvw]
            v_new = sf4[:, :, q_w + kvw:q_w + 2 * kvw]
            ki_new = stl[:, :IDX_DIM].reshape(bs, n_new, IDX_DIM)
            wi_s = stl[:, IDX_DIM:IDX_DIM + IDX_HEADS]
            k_all = _pad_rows(jnp.concatenate([cache_c_k[i].reshape(bs, past, kvw), k_new], axis=1), l_s_pad)
            v_all = _pad_rows(jnp.concatenate([cache_c_v[i].reshape(bs, past, kvw), v_new], axis=1), l_s_pad)
            ki_all = _pad_rows(jnp.concatenate([cache_c_idx[i], ki_new], axis=1), l_s_pad)
            sel_s = _dsa_index(sb, wi_s, kz_of(ki_all), bs, n_new, l_s_pad, l_s, n_new, l_s_pad, past, qic)
            mix_s = _dsa_attn(sb, k_all.astype(BF16).reshape(bs * l_s_pad, kvw),
                              v_all.astype(BF16).reshape(bs * l_s_pad, kvw), sel_s,
                              bs, n_new, l_s_pad, n_kv, n_new, l_s_pad, past, 0, 0)
            outs["cks"].append(k_new.reshape(bs, n_new, n_kv, HEAD_DIM))
            outs["cvs"].append(v_new.reshape(bs, n_new, n_kv, HEAD_DIM))
            outs["cis"].append(ki_new)

        xp, xp_b = _matmul_residual_ln(mix_p, w_out, xp, ln_g[l, 0], ln_b[l, 0], 512, 512)
        xs, xs_b = _matmul_residual_ln(mix_s, w_out, xs, ln_g[l, 0], ln_b[l, 0], 256, 512)

        wq = w_mem_q[l].astype(BF16)
        wo = w_mem_o[l].astype(BF16)
        (mkv,) = _matmul(mem2d, w_mem_kv[l].astype(BF16), (F32,), 512, 1024, 4096)
        mk_p, mv_p = mkv[:, :mem_w], mkv[:, mem_w:]
        outs["mkp"].append(mk_p.reshape(bp, n_mem, MEM_HEADS, HEAD_DIM))
        outs["mvp"].append(mv_p.reshape(bp, n_mem, MEM_HEADS, HEAD_DIM))
        xp, xp_b = _mem_attn_ln(xp, wq, mk_p, mv_p, wo, ln_g[l, 1], ln_b[l, 1], 256, 256, sp)
        xs, xs_b = _mem_attn_ln(xs, wq, cache_mem_k[l].reshape(bs * n_mem, mem_w).astype(BF16),
                                cache_mem_v[l].reshape(bs * n_mem, mem_w).astype(BF16), wo,
                                ln_g[l, 1], ln_b[l, 1], min(ms, 128), n_new, n_new)

        wu = w_up[l].astype(BF16)
        ff_pad = -(-d_ff // 1024) * 1024
        wd = jnp.pad(w_down[l], ((0, ff_pad - d_ff), (0, 0))).astype(BF16)
        act_p, st_p = _ffn_up(xp_b, wu, conv_w[l], conv_b[l], None, 1024, 1024, sp, ff_pad)
        act_s, st_s = _ffn_up(xs_b, wu, conv_w[l], conv_b[l], state_ffn_conv[l], ms, n_new, n_new, ff_pad)
        tiles_per_seq = sp // 1024
        outs["fp"].append(st_p[tiles_per_seq - 1::tiles_per_seq])
        outs["fs"].append(st_s)
        xp, xp_b = _matmul_residual_ln(act_p, wd, xp, ln_g[l, 2], ln_b[l, 2], 512, 512)
        xs, xs_b = _matmul_residual_ln(act_s, wd, xs, ln_g[l, 2], ln_b[l, 2], 256, 1024)

    st = jnp.stack
    o = outs
    return (xp.reshape(bp, sp, d), xs.reshape(bs, n_new, d),
            st(o["akp"]), st(o["avp"]), st(o["bkp"]), st(o["bvp"]), st(o["ckp"]), st(o["cvp"]), st(o["cip"]),
            st(o["mkp"]), st(o["mvp"]), st(o["fp"]),
            st(o["aks"]), st(o["avs"]), st(o["bks"]), st(o["bvs"]), st(o["cks"]), st(o["cvs"]), st(o["cis"]),
            st(o["fs"]))
```

```python
import functools

import numpy as np
import jax
import jax.numpy as jnp
from jax import lax
from jax.experimental import pallas as pl
from jax.experimental.pallas import tpu as pltpu

BF16 = jnp.bfloat16
F32 = jnp.float32

CHUNK = 64
HEAD_DIM = 128
A_LEFT_CHUNKS = 8
REL_CLIP = 128
C_GROUP = 4
IDX_HEADS = 32
IDX_DIM = 64
TOPK_MAX = 256
MEM_HEADS = 4
CONV_W = 3
DEPTH = 4
ALPHA = (2.0 * DEPTH) ** 0.25
LN_EPS = 1e-5
ATTN_SCALE = HEAD_DIM ** -0.5
IDX_SCALE = IDX_DIM ** -0.5
IDX_W_SCALE = IDX_HEADS ** -0.5
NEG_INF = -1e30

V7X_VMEM_BYTES = 64 * 1024 * 1024
LANES = 128
SUBLANES = 8
SB_EXIT = 104.0
INT_MIN = -(2 ** 31)
NEG_KEY = int(np.array(NEG_INF, np.float32).view(np.int32)) ^ 0x7FFFFFFF


def _cparams(vmem_bytes):
    limit = int(min(max(vmem_bytes * 1.25 + (4 << 20), 32 << 20), V7X_VMEM_BYTES - (6 << 20)))
    return pltpu.CompilerParams(vmem_limit_bytes=limit)


def _nt_dot(a, b):
    return lax.dot_general(a, b, (((1,), (1,)), ((), ())), preferred_element_type=F32)


def _mm_body(x_ref, w_ref, *out_refs):
    r = jnp.dot(x_ref[...].astype(BF16), w_ref[...], preferred_element_type=F32)
    for o in out_refs:
        o[...] = r.astype(o.dtype)


def _matmul(x, w, l, out_dtypes, tm, tn):
    m, kd = x.shape
    n = w.shape[2]
    tm, tn = min(tm, m), min(tn, n)
    assert m % tm == 0 and n % tn == 0, (x.shape, w.shape, tm, tn)
    vmem = 2 * (tm * kd * x.dtype.itemsize + kd * tn * 2) + 2 * tm * tn * 4
    vmem += sum(2 * tm * tn * jnp.dtype(d).itemsize for d in out_dtypes)
    return pl.pallas_call(
        _mm_body,
        grid=(m // tm, n // tn),
        in_specs=[pl.BlockSpec((tm, kd), lambda i, j: (i, 0)),
                  pl.BlockSpec((None, kd, tn), lambda i, j: (l, 0, j))],
        out_specs=[pl.BlockSpec((tm, tn), lambda i, j: (i, j)) for _ in out_dtypes],
        out_shape=[jax.ShapeDtypeStruct((m, n), d) for d in out_dtypes],
        compiler_params=_cparams(vmem),
        name="matmul",
    )(x, w)


LN_ROWS = 32


def _residual_ln_rows(y_ref, res_ref, g_ref, b_ref, o_ref, ob_ref, rows):
    g = g_ref[...]
    b = b_ref[...]

    def chunk(c, carry):
        sl = pl.ds(pl.multiple_of(c * LN_ROWS, LN_ROWS), LN_ROWS)
        y = ALPHA * res_ref[sl, :] + y_ref[sl, :]
        mu = jnp.mean(y, axis=-1, keepdims=True)
        d = y - mu
        var = jnp.mean(d * d, axis=-1, keepdims=True)
        out = d * lax.rsqrt(var + LN_EPS) * g + b
        o_ref[sl, :] = out
        ob_ref[sl, :] = out.astype(BF16)
        return carry

    lax.fori_loop(0, rows // LN_ROWS, chunk, 0)


MM_LN_CHUNK = 512


def _mm_ln_body(x_ref, w_ref, res_ref, g_ref, b_ref, o_ref, ob_ref, *, n_k, tm):
    k = pl.program_id(1)
    x = x_ref[...]
    n = o_ref.shape[1]
    chunks = [slice(c, c + MM_LN_CHUNK) for c in range(0, n, MM_LN_CHUNK)]

    @pl.when(k == 0)
    def _():
        for sl in chunks:
            o_ref[:, sl] = jnp.dot(x, w_ref[:, sl], preferred_element_type=F32)

    @pl.when(k > 0)
    def _():
        for sl in chunks:
            o_ref[:, sl] += jnp.dot(x, w_ref[:, sl], preferred_element_type=F32)

    @pl.when(k == n_k - 1)
    def _():
        _residual_ln_rows(o_ref, res_ref, g_ref, b_ref, o_ref, ob_ref, tm)


def _matmul_residual_ln(x, w, l, res, g, b, tm, tk):
    m, kd = x.shape
    n = w.shape[2]
    tm, tk = min(tm, m), min(tk, kd)
    assert m % tm == 0 and kd % tk == 0 and tm % LN_ROWS == 0 and n % MM_LN_CHUNK == 0
    n_k = kd // tk
    vmem = 2 * (tm * tk * 2 + tk * n * 2 + tm * n * 4 + tm * n * 4 + tm * n * 2)
    return pl.pallas_call(
        functools.partial(_mm_ln_body, n_k=n_k, tm=tm),
        grid=(m // tm, n_k),
        in_specs=[pl.BlockSpec((tm, tk), lambda i, k: (i, k)),
                  pl.BlockSpec((None, tk, n), lambda i, k: (l, k, 0)),
                  pl.BlockSpec((tm, n), lambda i, k: (i, 0)),
                  pl.BlockSpec((1, n), lambda i, k: (0, 0)),
                  pl.BlockSpec((1, n), lambda i, k: (0, 0))],
        out_specs=[pl.BlockSpec((tm, n), lambda i, k: (i, 0)),
                   pl.BlockSpec((tm, n), lambda i, k: (i, 0))],
        out_shape=[jax.ShapeDtypeStruct((m, n), F32), jax.ShapeDtypeStruct((m, n), BF16)],
        compiler_params=_cparams(vmem),
        name="matmul_residual_ln",
    )(x, w, res, g.reshape(1, n), b.reshape(1, n))


BAND_TQ = 128
BAND_KB = 5
ATTN_HB = 4


def _band_prompt_body(q_ref, k_ref, v_ref, bias_ref, o_ref):
    t = pl.program_id(2)
    for h in range(ATTN_HB):
        hs = slice(h * HEAD_DIM, (h + 1) * HEAD_DIM)
        q = q_ref[:, hs]
        scores = []
        for jb in range(BAND_KB):
            kb = t - (BAND_KB - 1) + jb
            start = pl.multiple_of(jnp.maximum(kb, 0) * BAND_TQ, BAND_TQ)
            s = _nt_dot(q, k_ref[pl.ds(start, BAND_TQ), hs]) * ATTN_SCALE
            s = s + bias_ref[h, :, jb * BAND_TQ:(jb + 1) * BAND_TQ]
            scores.append(jnp.where(kb >= 0, s, NEG_INF))
        m = functools.reduce(jnp.maximum, [jnp.max(s, axis=-1, keepdims=True) for s in scores])
        l = jnp.zeros((BAND_TQ, 1), F32)
        acc = jnp.zeros((BAND_TQ, HEAD_DIM), F32)
        for jb in range(BAND_KB):
            kb = t - (BAND_KB - 1) + jb
            start = pl.multiple_of(jnp.maximum(kb, 0) * BAND_TQ, BAND_TQ)
            p = jnp.exp(scores[jb] - m)
            l = l + jnp.sum(p, axis=-1, keepdims=True)
            acc = acc + jnp.dot(p.astype(BF16), v_ref[pl.ds(start, BAND_TQ), hs], preferred_element_type=F32)
        o_ref[:, hs] = (acc / l).astype(o_ref.dtype)


def _toeplitz_bias(rel_bias, n_rows, n_cols, offset):
    m = np.arange(n_rows + n_cols - 1) - (n_rows - 1)
    diag = rel_bias[:, np.clip(offset - m, -REL_CLIP, REL_CLIP) + REL_CLIP]
    rows = [diag[:, n_rows - 1 - i:n_rows - 1 - i + n_cols] for i in range(n_rows)]
    return jnp.stack(rows, axis=1).astype(F32)


def _band_attn_prompt(qkv, rel_bias, batch, seq, n_heads, q_col, k_col, v_col):
    assert n_heads % ATTN_HB == 0 and q_col % ATTN_HB == 0 and k_col % ATTN_HB == 0 and v_col % ATTN_HB == 0
    nt = seq // BAND_TQ
    hw = ATTN_HB * HEAD_DIM
    bias = _toeplitz_bias(rel_bias, BAND_TQ, BAND_KB * BAND_TQ, A_LEFT_CHUNKS * CHUNK)
    qi = np.arange(BAND_TQ)[:, None] // CHUNK
    kj = np.arange(BAND_KB * BAND_TQ)[None, :] // CHUNK
    in_band = (kj >= qi) & (kj <= qi + A_LEFT_CHUNKS)
    bias = jnp.where(jnp.asarray(in_band)[None], bias, NEG_INF)
    vmem = 2 * (2 * seq * hw * 2 + ATTN_HB * BAND_TQ * BAND_KB * BAND_TQ * 4) + (8 << 20)
    return pl.pallas_call(
        _band_prompt_body,
        grid=(batch, n_heads // ATTN_HB, nt),
        in_specs=[pl.BlockSpec((BAND_TQ, hw), lambda b, h, t: (b * nt + t, q_col // ATTN_HB + h)),
                  pl.BlockSpec((seq, hw), lambda b, h, t: (b, k_col // ATTN_HB + h)),
                  pl.BlockSpec((seq, hw), lambda b, h, t: (b, v_col // ATTN_HB + h)),
                  pl.BlockSpec((ATTN_HB, BAND_TQ, BAND_KB * BAND_TQ), lambda b, h, t: (h, 0, 0))],
        out_specs=pl.BlockSpec((BAND_TQ, hw), lambda b, h, t: (b * nt + t, h)),
        out_shape=jax.ShapeDtypeStruct((batch * seq, n_heads * HEAD_DIM), BF16),
        compiler_params=_cparams(vmem),
        name="band_attn_prompt",
    )(qkv, qkv, qkv, bias)


def _bias_attn_body(q_ref, k_ref, v_ref, bias_ref, o_ref, *, n_heads):
    for h in range(n_heads):
        hs = slice(h * HEAD_DIM, (h + 1) * HEAD_DIM)
        s = _nt_dot(q_ref[:, hs], k_ref[:, hs]) * ATTN_SCALE + bias_ref[h]
        m = jnp.max(s, axis=-1, keepdims=True)
        p = jnp.exp(s - m)
        l = jnp.sum(p, axis=-1, keepdims=True)
        acc = jnp.dot(p.astype(BF16), v_ref[:, hs], preferred_element_type=F32)
        o_ref[:, hs] = (acc / l).astype(o_ref.dtype)


def _band_attn_sample(q2d, k_all, v_all, rel_bias, batch, n_new, n_heads):
    L = k_all.shape[0] // batch
    hw = n_heads * HEAD_DIM
    bias = _toeplitz_bias(rel_bias, n_new, L, L - n_new)
    return pl.pallas_call(
        functools.partial(_bias_attn_body, n_heads=n_heads),
        grid=(batch,),
        in_specs=[pl.BlockSpec((n_new, hw), lambda b: (b, 0)),
                  pl.BlockSpec((L, hw), lambda b: (b, 0)),
                  pl.BlockSpec((L, hw), lambda b: (b, 0)),
                  pl.BlockSpec((n_heads, n_new, L), lambda b: (0, 0, 0))],
        out_specs=pl.BlockSpec((n_new, hw), lambda b: (b, 0)),
        out_shape=jax.ShapeDtypeStruct((batch * n_new, hw), BF16),
        name="band_attn_sample",
    )(q2d, k_all, v_all, bias)


SB_TK = 256


def _split2(x):
    hi = x.astype(BF16)
    return hi, (x - hi.astype(F32)).astype(BF16)


def _stick_break_body(q_ref, k_ref, v_ref, u_ref, o_ref, *, tq, qpos_base):
    qt = pl.program_id(2)
    qpos0 = qpos_base + qt * tq
    u = u_ref[...]
    rowpos = qpos0 + lax.broadcasted_iota(jnp.int32, (tq, 1), 0)
    top = (qpos0 + tq + LANES - 1) // LANES * LANES

    def cond(c):
        limit, floor = c[0], c[1]
        return jnp.logical_and(limit > 0, floor > -SB_EXIT)

    def body(c):
        limit = c[0]
        start = pl.multiple_of(jnp.maximum(limit - SB_TK, 0), LANES)
        kpos = start + lax.broadcasted_iota(jnp.int32, (1, SB_TK), 1)
        mask = jnp.logical_and(kpos < rowpos, kpos < limit)
        state = []
        floor = None
        for h in range(ATTN_HB):
            carry, acc = c[2 + 2 * h], c[3 + 2 * h]
            hs = slice(h * HEAD_DIM, (h + 1) * HEAD_DIM)
            z = _nt_dot(q_ref[:, hs], k_ref[pl.ds(start, SB_TK), hs]) * ATTN_SCALE
            sp = jnp.maximum(z, 0.0) + jnp.log(1.0 + jnp.exp(-jnp.abs(z)))
            log_1m = jnp.where(mask, -sp, 0.0)
            suffix = sum(jnp.dot(piece, u, preferred_element_type=F32) for piece in _split2(log_1m))
            w = jnp.where(mask, jnp.exp(z - sp + suffix + carry), 0.0)
            acc = acc + jnp.dot(w.astype(BF16), v_ref[pl.ds(start, SB_TK), hs], preferred_element_type=F32)
            carry = carry + jnp.sum(log_1m, axis=-1, keepdims=True)
            top_h = jnp.max(carry)
            floor = top_h if floor is None else jnp.maximum(floor, top_h)
            state += [carry, acc]
        return (limit - SB_TK, floor, *state)

    init = [top, jnp.float32(0.0)]
    for _ in range(ATTN_HB):
        init += [jnp.zeros((tq, 1), F32), jnp.zeros((tq, HEAD_DIM), F32)]
    out = lax.while_loop(cond, body, tuple(init))
    for h in range(ATTN_HB):
        o_ref[:, h * HEAD_DIM:(h + 1) * HEAD_DIM] = out[3 + 2 * h].astype(o_ref.dtype)


def _stick_break_attn(q2d, k2d, v2d, batch, n_q, n_k, n_heads, tq, qpos_base, q_col, k_col, v_col):
    assert n_q % tq == 0 and n_k % LANES == 0 and n_k >= SB_TK and n_heads % ATTN_HB == 0
    assert q_col % ATTN_HB == 0 and k_col % ATTN_HB == 0 and v_col % ATTN_HB == 0
    assert -(-(qpos_base + n_q) // LANES) * LANES <= n_k
    nt = n_q // tq
    hw = ATTN_HB * HEAD_DIM
    j = np.arange(SB_TK)
    u = jnp.asarray(j[:, None] > j[None, :], BF16)
    vmem = 2 * (2 * n_k * hw * 2) + (8 << 20)
    return pl.pallas_call(
        functools.partial(_stick_break_body, tq=tq, qpos_base=qpos_base),
        grid=(batch, n_heads // ATTN_HB, nt),
        in_specs=[pl.BlockSpec((tq, hw), lambda b, h, t: (b * nt + t, q_col // ATTN_HB + h)),
                  pl.BlockSpec((n_k, hw), lambda b, h, t: (b, k_col // ATTN_HB + h)),
                  pl.BlockSpec((n_k, hw), lambda b, h, t: (b, v_col // ATTN_HB + h)),
                  pl.BlockSpec((SB_TK, SB_TK), lambda b, h, t: (0, 0))],
        out_specs=pl.BlockSpec((tq, hw), lambda b, h, t: (b * nt + t, h)),
        out_shape=jax.ShapeDtypeStruct((batch * n_q, n_heads * HEAD_DIM), BF16),
        compiler_params=_cparams(vmem),
        name="stick_break_attn",
    )(q2d, k2d, v2d, u)


def _sortable(x):
    bits = pltpu.bitcast(x, jnp.int32)
    return jnp.where(bits < 0, bits ^ jnp.int32(0x7FFFFFFF), bits)


def _dsa_index_body(qi_ref, wi_ref, kz_ref, bias_ref, key_ref, *, tq, tk, n_real, qpos_base, topk):
    n_pad = key_ref.shape[1]
    qt = pl.program_id(1)
    qpos0 = qpos_base + qt * tq
    rowpos = qpos0 + lax.broadcasted_iota(jnp.int32, (tq, 1), 0)
    qchunk = rowpos // CHUNK
    n_kb = jnp.minimum((((qpos0 + tq - 1) // CHUNK + 1) * CHUNK + tk - 1) // tk, n_pad // tk)
    n_tail = jnp.maximum(n_real - n_kb * tk, 0)
    tail0 = n_kb * tk

    w = wi_ref[...] * (IDX_SCALE * IDX_W_SCALE)

    def block_pos(kb):
        start = pl.multiple_of(kb * tk, tk)
        return start, start + lax.broadcasted_iota(jnp.int32, (1, tk), 1)

    def score_block(kb, carry):
        start, kpos = block_pos(kb)
        kz = kz_ref[pl.ds(start, tk), :]
        acc = jnp.zeros((tq, tk), F32)
        for pair in range(IDX_HEADS // 2):
            qp = qi_ref[:, pair * LANES:(pair + 1) * LANES]
            for half in range(2):
                h = 2 * pair + half
                sc = _nt_dot(qp, kz[:, half * LANES:(half + 1) * LANES])
                acc = acc + jnp.maximum(sc, 0.0) * w[:, h:h + 1]
        key = _sortable(jnp.where(kpos // CHUNK <= qchunk, acc, NEG_INF))
        key_ref[:, pl.ds(start, tk)] = jnp.where(kpos < n_real, key, jnp.int32(INT_MIN))
        return carry

    lax.fori_loop(0, n_kb, score_block, 0)

    def count(pred):
        def blk(kb, acc):
            start, kpos = block_pos(kb)
            hit = pred(key_ref[:, pl.ds(start, tk)], kpos).astype(F32)
            return acc + sum(hit[:, c:c + LANES] for c in range(0, tk, LANES))
        lanes = lax.fori_loop(0, n_kb, blk, jnp.zeros((tq, LANES), F32))
        return jnp.sum(lanes, axis=-1, keepdims=True)

    kf = jnp.float32(topk)
    tail_f = n_tail.astype(F32)
    neg_key = jnp.int32(NEG_KEY)

    def thr_step(i, ans_u):
        cand_u = ans_u | lax.shift_left(jnp.int32(1), 31 - i)
        cand = cand_u ^ jnp.int32(INT_MIN)
        cnt = count(lambda key, kpos: key >= cand) + jnp.where(neg_key >= cand, tail_f, 0.0)
        return jnp.where(cnt >= kf, cand_u, ans_u)

    thr = lax.fori_loop(0, 32, thr_step, jnp.zeros((tq, 1), jnp.int32)) ^ jnp.int32(INT_MIN)
    tail_gt = jnp.where(neg_key > thr, tail_f, 0.0)
    n_gt = count(lambda key, kpos: key > thr) + tail_gt
    n_eq = count(lambda key, kpos: key == thr) + jnp.where(neg_key == thr, tail_f, 0.0)
    need = kf - n_gt

    n_bits = int(np.ceil(np.log2(n_pad)))

    def tie_search():
        def pos_step(i, lo):
            cand = lo + lax.shift_left(jnp.int32(1), n_bits - 1 - i)
            in_tail = jnp.clip(cand - tail0, 0, n_tail).astype(F32)
            cnt = (count(lambda key, kpos: jnp.logical_and(key == thr, kpos < cand))
                   + jnp.where(neg_key == thr, in_tail, 0.0))
            return jnp.where(cnt < need, cand, lo)
        return lax.fori_loop(0, n_bits, pos_step, jnp.zeros((tq, 1), jnp.int32))

    has_ties = jnp.max(jnp.where(n_eq > need, 1.0, 0.0)) > 0.0
    last = lax.cond(has_ties, tie_search, lambda: jnp.full((tq, 1), n_pad, jnp.int32))

    bias_ref[0] = jnp.full(bias_ref.shape[1:], NEG_INF, bias_ref.dtype)

    def write_block(kb, carry):
        start, kpos = block_pos(kb)
        key = key_ref[:, pl.ds(start, tk)]
        sel = jnp.logical_or(key > thr, jnp.logical_and(key == thr, kpos <= last))
        ok = jnp.logical_and(jnp.logical_and(sel, kpos // CHUNK <= qchunk), kpos < n_real)
        bias_ref[0, :, pl.ds(start, tk)] = jnp.where(ok, 0.0, NEG_INF).astype(bias_ref.dtype)
        return carry

    lax.fori_loop(0, n_kb, write_block, 0)


def _dsa_index(qi2d, wi, kz, batch, n_q, n_k, n_real, tq, tk, qpos_base, qi_col):
    nt = n_q // tq
    topk = min(TOPK_MAX, n_real // 4)
    vmem = 2 * (tq * 2048 * 2 + n_k * 256 * 2 + tq * n_k * 2) + tq * n_k * 4 + 8 * tq * tk * 4 + (4 << 20)
    return pl.pallas_call(
        functools.partial(_dsa_index_body, tq=tq, tk=tk, n_real=n_real, qpos_base=qpos_base, topk=topk),
        grid=(batch, nt),
        in_specs=[pl.BlockSpec((tq, IDX_HEADS * IDX_DIM), lambda b, t: (b * nt + t, qi_col)),
                  pl.BlockSpec((tq, IDX_HEADS), lambda b, t: (b * nt + t, 0)),
                  pl.BlockSpec((n_k, 2 * LANES), lambda b, t: (b, 0))],
        out_specs=pl.BlockSpec((1, tq, n_k), lambda b, t: (b, t, 0)),
        out_shape=jax.ShapeDtypeStruct((batch, n_q, n_k), BF16),
        scratch_shapes=[pltpu.VMEM((tq, n_k), jnp.int32)],
        compiler_params=_cparams(vmem),
        name="dsa_index",
    )(qi2d, wi, kz)


def _dsa_attn_body(q_ref, k_ref, v_ref, bias_ref, slope_ref, o_ref, *, tq, tk, qpos_base):
    n_pad = k_ref.shape[0]
    qt = pl.program_id(2)
    qpos0 = qpos_base + qt * tq
    tile_end = qpos0 + tq
    rowpos = qpos0 + lax.broadcasted_iota(jnp.int32, (tq, 1), 0)
    n_kb = jnp.minimum((((qpos0 + tq - 1) // CHUNK + 1) * CHUNK + tk - 1) // tk, n_pad // tk)
    slopes = [slope_ref[0, g:g + 1, 0:1] for g in range(C_GROUP)]

    def block(kb, state, has_later_keys):
        start = pl.multiple_of(kb * tk, tk)
        kpos = start + lax.broadcasted_iota(jnp.int32, (1, tk), 1)
        back = (tile_end - kpos).astype(F32)
        bias = bias_ref[0, :, pl.ds(start, tk)].astype(F32)
        k = k_ref[pl.ds(start, tk), :]
        v = v_ref[pl.ds(start, tk), :]
        if has_later_keys:
            ahead = jnp.where(kpos > rowpos, (rowpos - kpos).astype(F32), 0.0)
        out = []
        for g in range(C_GROUP):
            m, l, acc = state[3 * g:3 * g + 3]
            s = _nt_dot(q_ref[:, g * HEAD_DIM:(g + 1) * HEAD_DIM], k) * ATTN_SCALE + (bias - slopes[g] * back)
            if has_later_keys:
                s = s + (2.0 * slopes[g]) * ahead
            m_new = jnp.maximum(m, jnp.max(s, axis=-1, keepdims=True))
            a = jnp.exp(m - m_new)
            p = jnp.exp(s - m_new)
            l = a * l + jnp.sum(p, axis=-1, keepdims=True)
            acc = a * acc + jnp.dot(p.astype(BF16), v, preferred_element_type=F32)
            out += [m_new, l, acc]
        return tuple(out)

    init = (jnp.full((tq, 1), 0.5 * NEG_INF, F32), jnp.zeros((tq, 1), F32),
            jnp.zeros((tq, HEAD_DIM), F32)) * C_GROUP
    state = lax.fori_loop(0, n_kb - 1, lambda kb, st: block(kb, st, False), init)
    state = block(n_kb - 1, state, True)
    for g in range(C_GROUP):
        l, acc = state[3 * g + 1], state[3 * g + 2]
        o_ref[:, g * HEAD_DIM:(g + 1) * HEAD_DIM] = (acc / l).astype(o_ref.dtype)


def _dsa_attn(q2d, k2d, v2d, bias, batch, n_q, n_k, n_kv, tq, tk, qpos_base, k_col, v_col):
    assert tk % tq == 0 and qpos_base % tq == 0 and n_k % tk == 0
    nt = n_q // tq
    gw = C_GROUP * HEAD_DIM
    n_heads = n_kv * C_GROUP
    slopes = 2.0 ** (-8.0 * np.arange(1, n_heads + 1) / n_heads)
    table = np.zeros((n_kv, SUBLANES, LANES), np.float32)
    table[:, :C_GROUP, :] = slopes.reshape(n_kv, C_GROUP, 1)
    vmem = 2 * (2 * n_k * HEAD_DIM * 2 + tq * n_k * 2 + 2 * tq * gw * 2) + 12 * tq * tk * 4 + (4 << 20)
    return pl.pallas_call(
        functools.partial(_dsa_attn_body, tq=tq, tk=tk, qpos_base=qpos_base),
        grid=(batch, n_kv, nt),
        in_specs=[pl.BlockSpec((tq, gw), lambda b, n, t: (b * nt + t, n)),
                  pl.BlockSpec((n_k, HEAD_DIM), lambda b, n, t: (b, k_col + n)),
                  pl.BlockSpec((n_k, HEAD_DIM), lambda b, n, t: (b, v_col + n)),
                  pl.BlockSpec((1, tq, n_k), lambda b, n, t: (b, t, 0)),
                  pl.BlockSpec((1, SUBLANES, LANES), lambda b, n, t: (n, 0, 0))],
        out_specs=pl.BlockSpec((tq, gw), lambda b, n, t: (b * nt + t, n)),
        out_shape=jax.ShapeDtypeStruct((batch * n_q, n_heads * HEAD_DIM), BF16),
        compiler_params=_cparams(vmem),
        name="dsa_attn",
    )(q2d, k2d, v2d, bias, jnp.asarray(table))


def _mem_attn_body(x_ref, wq_ref, mk_ref, mv_ref, wo_ref, g_ref, b_ref, o_ref, ob_ref, y_ref, *,
                   tm, rows_per_seq, n_mem):
    q = jnp.dot(x_ref[...].astype(BF16), wq_ref[...], preferred_element_type=F32).astype(BF16)
    for s_i in range(tm // rows_per_seq):
        rsl = slice(s_i * rows_per_seq, (s_i + 1) * rows_per_seq)
        msl = slice(s_i * n_mem, (s_i + 1) * n_mem)
        heads = []
        for h in range(MEM_HEADS):
            hsl = slice(h * HEAD_DIM, (h + 1) * HEAD_DIM)
            s = _nt_dot(q[rsl, hsl], mk_ref[msl, hsl].astype(BF16)) * ATTN_SCALE
            m = jnp.max(s, axis=-1, keepdims=True)
            p = jnp.exp(s - m)
            l = jnp.sum(p, axis=-1, keepdims=True)
            o = jnp.dot(p.astype(BF16), mv_ref[msl, hsl].astype(BF16), preferred_element_type=F32)
            heads.append((o / l).astype(BF16))
        o_all = jnp.concatenate(heads, axis=-1)
        y_ref[rsl, :] = jnp.dot(o_all, wo_ref[...], preferred_element_type=F32)
    _residual_ln_rows(y_ref, x_ref, g_ref, b_ref, o_ref, ob_ref, tm)


def _mem_attn_ln(x, wq, wo, l, mk, mv, g, b, tm, rows_per_seq, seq_rows):
    m, d = x.shape
    mw = wq.shape[2]
    n_seq_tile = tm // rows_per_seq
    n_mem = mk.shape[0] // (m // seq_rows)
    if n_seq_tile == 1:
        mem_map = lambda i: ((i * tm) // seq_rows, 0)
    else:
        assert rows_per_seq == seq_rows
        mem_map = lambda i: (i, 0)
    vmem = 2 * (tm * d * 4 * 2 + tm * d * 2 + 2 * d * mw * 2 + 2 * n_seq_tile * n_mem * mw * 4) + tm * d * 4
    return pl.pallas_call(
        functools.partial(_mem_attn_body, tm=tm, rows_per_seq=rows_per_seq, n_mem=n_mem),
        grid=(m // tm,),
        in_specs=[pl.BlockSpec((tm, d), lambda i: (i, 0)),
                  pl.BlockSpec((None, d, mw), lambda i: (l, 0, 0)),
                  pl.BlockSpec((n_seq_tile * n_mem, mw), mem_map),
                  pl.BlockSpec((n_seq_tile * n_mem, mw), mem_map),
                  pl.BlockSpec((None, mw, d), lambda i: (l, 0, 0)),
                  pl.BlockSpec((1, d), lambda i: (0, 0)),
                  pl.BlockSpec((1, d), lambda i: (0, 0))],
        out_specs=[pl.BlockSpec((tm, d), lambda i: (i, 0)),
                   pl.BlockSpec((tm, d), lambda i: (i, 0))],
        out_shape=[jax.ShapeDtypeStruct((m, d), F32), jax.ShapeDtypeStruct((m, d), BF16)],
        scratch_shapes=[pltpu.VMEM((tm, d), F32)],
        compiler_params=_cparams(vmem),
        name="mem_attn_ln",
    )(x, wq, mk, mv, wo, g.reshape(1, d), b.reshape(1, d))


FFN_TN = 256
FFN_SUB = 256
HALO = 8


def _interleave_halves(a):
    lead, f2 = a.shape[:-1], a.shape[-1]
    nj = f2 // (2 * FFN_TN)
    return a.reshape(*lead, 2, nj, FFN_TN).swapaxes(-3, -2).reshape(*lead, f2)


def _deinterleave_halves(a):
    lead, f2 = a.shape[:-1], a.shape[-1]
    nj = f2 // (2 * FFN_TN)
    return a.reshape(*lead, nj, 2, FFN_TN).swapaxes(-3, -2).reshape(*lead, f2)


def _gelu(x):
    return 0.5 * x * (1.0 + lax.erf(x * np.float32(2.0 ** -0.5)))


def _ffn_up_body(x_ref, w_ref, cw_ref, cb_ref, *rest, tm, rows_per_seq, tiles_per_seq, carried, nj):
    if carried:
        act_ref, st_ref, carry_ref, ext_ref = rest
    else:
        prev_ref, act_ref, st_ref, ext_ref = rest
    i = pl.program_id(0)
    j = pl.program_id(1)

    def conv_gate(base, rows):
        h = (ext_ref[base:base + rows, :] * cw_ref[2:3, :]
             + ext_ref[base - 1:base - 1 + rows, :] * cw_ref[1:2, :]
             + ext_ref[base - 2:base - 2 + rows, :] * cw_ref[0:1, :]
             + cb_ref[...])
        return (h[:, :FFN_TN] * _gelu(h[:, FFN_TN:])).astype(act_ref.dtype)

    @pl.when(j < nj)
    def _():
        if carried:
            @pl.when(i % tiles_per_seq == 0)
            def _():
                carry_ref[j] = jnp.zeros((2, 2 * FFN_TN), F32)

            ext_ref[HALO - 2:HALO, :] = carry_ref[j]
            sub = min(FFN_SUB, tm)
            for r in range(0, tm + sub, sub):
                if r < tm:
                    ext_ref[HALO + r:HALO + r + sub, :] = jnp.dot(x_ref[r:r + sub, :], w_ref[...],
                                                                  preferred_element_type=F32)
                if r >= sub:
                    act_ref[r - sub:r, :] = conv_gate(HALO + r - sub, sub)
            last2 = ext_ref[HALO + tm - 2:HALO + tm, :]
            carry_ref[j] = last2
            st_ref[0] = last2
        else:
            up = jnp.dot(x_ref[...], w_ref[...], preferred_element_type=F32)
            for s_i in range(tm // rows_per_seq):
                base = s_i * (rows_per_seq + HALO) + HALO
                rsl = slice(s_i * rows_per_seq, (s_i + 1) * rows_per_seq)
                ext_ref[base - 2:base, :] = prev_ref[s_i]
                ext_ref[base:base + rows_per_seq, :] = up[rsl]
                st_ref[s_i] = ext_ref[base + rows_per_seq - 2:base + rows_per_seq, :]
                act_ref[rsl, :] = conv_gate(base, rows_per_seq)

    @pl.when(j >= nj)
    def _():
        act_ref[...] = jnp.zeros_like(act_ref)


def _ffn_up(x, w_up, l, conv_w, conv_b, prev, tm, rows_per_seq, seq_rows, n_act_cols):
    m, d = x.shape
    f2 = w_up.shape[2]
    wt = 2 * FFN_TN
    assert f2 % wt == 0 and n_act_cols % FFN_TN == 0
    nj = f2 // wt
    nj_pad = n_act_cols // FFN_TN
    carried = prev is None
    n_seq = tm // rows_per_seq
    tiles_per_seq = max(seq_rows // tm, 1)
    n_tiles = m // tm
    col = lambda i, j: (0, jnp.minimum(j, nj - 1))
    in_specs = [pl.BlockSpec((tm, d), lambda i, j: (i, 0)),
                pl.BlockSpec((None, d, wt), lambda i, j: (l, 0, jnp.minimum(j, nj - 1))),
                pl.BlockSpec((CONV_W, wt), col),
                pl.BlockSpec((1, wt), col)]
    args = [x, w_up, conv_w, conv_b]
    scratch = []
    if carried:
        assert n_seq == 1
        scratch.append(pltpu.VMEM((nj, 2, wt), F32))
    else:
        in_specs.append(pl.BlockSpec((n_seq, 2, wt), lambda i, j: (i, 0, jnp.minimum(j, nj - 1))))
        args.append(prev)
    scratch.append(pltpu.VMEM((n_seq * (rows_per_seq + HALO), wt), F32))
    vmem = 2 * (tm * d * 2 + d * wt * 2 + tm * FFN_TN * 2) + tm * wt * 4 + 8 * min(tm, FFN_SUB) * wt * 4 + (4 << 20)
    act, st = pl.pallas_call(
        functools.partial(_ffn_up_body, tm=tm, rows_per_seq=rows_per_seq, tiles_per_seq=tiles_per_seq,
                          carried=carried, nj=nj),
        grid=(n_tiles, nj_pad),
        in_specs=in_specs,
        out_specs=[pl.BlockSpec((tm, FFN_TN), lambda i, j: (i, j)),
                   pl.BlockSpec((n_seq, 2, wt), lambda i, j: (i, 0, jnp.minimum(j, nj - 1)))],
        out_shape=[jax.ShapeDtypeStruct((m, n_act_cols), BF16),
                   jax.ShapeDtypeStruct((n_tiles * n_seq, 2, f2), F32)],
        scratch_shapes=scratch,
        compiler_params=_cparams(vmem),
        name="ffn_up_conv_gate",
    )(*args)
    return act, st


def _pad_rows(a, n_rows):
    return jnp.pad(a, ((0, 0), (0, n_rows - a.shape[1]), (0, 0)))


def kernel(x_prompt, x_sample, cache_a_k, cache_a_v, cache_b_k, cache_b_v, cache_c_k, cache_c_v, cache_c_idx, cache_mem_k, cache_mem_v, state_ffn_conv, mem_prompt, w_in_even, w_out_even, rel_bias_a, w_in_odd, w_out_odd, w_mem_q, w_mem_kv, w_mem_o, w_up, conv_w, conv_b, w_down, ln_g, ln_b):
    bp, sp, d = x_prompt.shape
    bs, n_new, _ = x_sample.shape
    depth = w_up.shape[0]
    h_a = cache_a_k.shape[3]
    h_b = cache_b_k.shape[3]
    n_kv = cache_c_k.shape[3]
    h_c = n_kv * C_GROUP
    hw_a, hw_b, kvw = h_a * HEAD_DIM, h_b * HEAD_DIM, n_kv * HEAD_DIM
    qiw = IDX_HEADS * IDX_DIM
    a_win_s = cache_a_k.shape[2]
    a_win_p = min(A_LEFT_CHUNKS * CHUNK, sp)
    past = cache_b_k.shape[2]
    n_mem = mem_prompt.shape[1]
    mem_w = w_mem_q.shape[2]
    d_ff = w_down.shape[1]
    mp, ms = bp * sp, bs * n_new
    l_s = past + n_new
    l_s_pad = -(-l_s // LANES) * LANES
    main_odd = h_c * HEAD_DIM + 2 * kvw + qiw
    tail_pad = LANES
    ff_pad = -(-d_ff // 1024) * 1024

    wie = w_in_even.astype(BF16)
    woe = w_out_even.astype(BF16)
    wio_main = w_in_odd[:, :, :main_odd].astype(BF16)
    wio_tail = jnp.pad(w_in_odd[:, :, main_odd:],
                       ((0, 0), (0, 0), (0, tail_pad - (IDX_DIM + IDX_HEADS)))).astype(BF16)
    woo = w_out_odd.astype(BF16)
    wmq = w_mem_q.astype(BF16)
    wmkv = w_mem_kv.astype(BF16)
    wmo = w_mem_o.astype(BF16)
    wu = _interleave_halves(w_up.astype(BF16))
    wd = jnp.pad(w_down, ((0, 0), (0, ff_pad - d_ff), (0, 0))).astype(BF16)
    cw_i = _interleave_halves(conv_w)
    cb_i = _interleave_halves(conv_b)[:, None, :]
    st_prev_i = _interleave_halves(state_ffn_conv)

    xp = x_prompt.reshape(mp, d)
    xs = x_sample.reshape(ms, d)
    xp_b, xs_b = xp.astype(BF16), xs.astype(BF16)
    mem2d = mem_prompt.reshape(bp * n_mem, d).astype(BF16)
    cmk = cache_mem_k.reshape(depth, bs * n_mem, mem_w).astype(BF16)
    cmv = cache_mem_v.reshape(depth, bs * n_mem, mem_w).astype(BF16)

    outs = {k: [] for k in ("akp", "avp", "bkp", "bvp", "ckp", "cvp", "cip", "mkp", "mvp", "fp",
                            "aks", "avs", "bks", "bvs", "cks", "cvs", "cis", "fs")}

    for l in range(depth):
        i = l // 2
        if l % 2 == 0:
            w_out = woe
            pf, pb = _matmul(xp_b, wie, i, (F32, BF16), 1024, 512)
            oa = _band_attn_prompt(pb, rel_bias_a[i], bp, sp, h_a, 0, h_a, 2 * h_a)
            cb0 = 3 * h_a
            ob = _stick_break_attn(pb, pb, pb, bp, sp, sp, h_b, 128, 0, cb0, cb0 + h_b, cb0 + 2 * h_b)
            mix_p = jnp.concatenate([oa, ob], axis=-1)
            pf4 = pf.reshape(bp, sp, -1)
            outs["akp"].append(pf4[:, sp - a_win_p:, hw_a:2 * hw_a].reshape(bp, a_win_p, h_a, HEAD_DIM))
            outs["avp"].append(pf4[:, sp - a_win_p:, 2 * hw_a:3 * hw_a].reshape(bp, a_win_p, h_a, HEAD_DIM))
            o0 = 3 * hw_a
            outs["bkp"].append(pf4[:, :, o0 + hw_b:o0 + 2 * hw_b].reshape(bp, sp, h_b, HEAD_DIM))
            outs["bvp"].append(pf4[:, :, o0 + 2 * hw_b:o0 + 3 * hw_b].reshape(bp, sp, h_b, HEAD_DIM))
            sf, sb = _matmul(xs_b, wie, i, (F32, BF16), 256, 512)
            sf4 = sf.reshape(bs, n_new, -1)
            ka_all = jnp.concatenate([cache_a_k[i].reshape(bs, a_win_s, hw_a), sf4[:, :, hw_a:2 * hw_a]], axis=1)
            va_all = jnp.concatenate([cache_a_v[i].reshape(bs, a_win_s, hw_a), sf4[:, :, 2 * hw_a:3 * hw_a]], axis=1)
            la = a_win_s + n_new
            oa_s = _band_attn_sample(sb, ka_all.astype(BF16).reshape(bs * la, hw_a),
                                     va_all.astype(BF16).reshape(bs * la, hw_a), rel_bias_a[i], bs, n_new, h_a)
            kb_new = sf4[:, :, o0 + hw_b:o0 + 2 * hw_b]
            vb_new = sf4[:, :, o0 + 2 * hw_b:o0 + 3 * hw_b]
            kb_all = _pad_rows(jnp.concatenate([cache_b_k[i].reshape(bs, past, hw_b), kb_new], axis=1), l_s_pad)
            vb_all = _pad_rows(jnp.concatenate([cache_b_v[i].reshape(bs, past, hw_b), vb_new], axis=1), l_s_pad)
            ob_s = _stick_break_attn(sb, kb_all.astype(BF16).reshape(bs * l_s_pad, hw_b),
                                     vb_all.astype(BF16).reshape(bs * l_s_pad, hw_b),
                                     bs, n_new, l_s_pad, h_b, n_new, past, cb0, 0, 0)
            mix_s = jnp.concatenate([oa_s, ob_s], axis=-1)
            outs["aks"].append(ka_all[:, la - a_win_s:].reshape(bs, a_win_s, h_a, HEAD_DIM))
            outs["avs"].append(va_all[:, la - a_win_s:].reshape(bs, a_win_s, h_a, HEAD_DIM))
            outs["bks"].append(kb_new.reshape(bs, n_new, h_b, HEAD_DIM))
            outs["bvs"].append(vb_new.reshape(bs, n_new, h_b, HEAD_DIM))
        else:
            w_out = woo
            kc0 = h_c
            vc0 = h_c + n_kv
            qic = (h_c * HEAD_DIM + 2 * kvw) // qiw

            def kz_of(ki):
                z = jnp.zeros_like(ki)
                return jnp.concatenate([ki, z, z, ki], axis=-1).astype(BF16).reshape(-1, 2 * LANES)

            pf, pb = _matmul(xp_b, wio_main, i, (F32, BF16), 1024, 512)
            (pt,) = _matmul(xp_b, wio_tail, i, (F32,), 1024, tail_pad)
            ki_p = pt[:, :IDX_DIM].reshape(bp, sp, IDX_DIM)
            wi_p = pt[:, IDX_DIM:IDX_DIM + IDX_HEADS]
            sel_p = _dsa_index(pb, wi_p, kz_of(ki_p), bp, sp, sp, sp, 128, 512, 0, qic)
            mix_p = _dsa_attn(pb, pb, pb, sel_p, bp, sp, sp, n_kv, 128, 512, 0, kc0, vc0)
            pf4 = pf.reshape(bp, sp, -1)
            q_w = h_c * HEAD_DIM
            outs["ckp"].append(pf4[:, :, q_w:q_w + kvw].reshape(bp, sp, n_kv, HEAD_DIM))
            outs["cvp"].append(pf4[:, :, q_w + kvw:q_w + 2 * kvw].reshape(bp, sp, n_kv, HEAD_DIM))
            outs["cip"].append(ki_p)
            sf, sb = _matmul(xs_b, wio_main, i, (F32, BF16), 256, 512)
            (stl,) = _matmul(xs_b, wio_tail, i, (F32,), 256, tail_pad)
            sf4 = sf.reshape(bs, n_new, -1)
            k_new = sf4[:, :, q_w:q_w + kvw]
            v_new = sf4[:, :, q_w + kvw:q_w + 2 * kvw]
            ki_new = stl[:, :IDX_DIM].reshape(bs, n_new, IDX_DIM)
            wi_s = stl[:, IDX_DIM:IDX_DIM + IDX_HEADS]
            k_all = _pad_rows(jnp.concatenate([cache_c_k[i].reshape(bs, past, kvw), k_new], axis=1), l_s_pad)
            v_all = _pad_rows(jnp.concatenate([cache_c_v[i].reshape(bs, past, kvw), v_new], axis=1), l_s_pad)
            ki_all = _pad_rows(jnp.concatenate([cache_c_idx[i], ki_new], axis=1), l_s_pad)
            sel_s = _dsa_index(sb, wi_s, kz_of(ki_all), bs, n_new, l_s_pad, l_s, n_new, l_s_pad, past, qic)
            mix_s = _dsa_attn(sb, k_all.astype(BF16).reshape(bs * l_s_pad, kvw),
                              v_all.astype(BF16).reshape(bs * l_s_pad, kvw), sel_s,
                              bs, n_new, l_s_pad, n_kv, n_new, l_s_pad, past, 0, 0)
            outs["cks"].append(k_new.reshape(bs, n_new, n_kv, HEAD_DIM))
            outs["cvs"].append(v_new.reshape(bs, n_new, n_kv, HEAD_DIM))
            outs["cis"].append(ki_new)

        xp, xp_b = _matmul_residual_ln(mix_p, w_out, i, xp, ln_g[l, 0], ln_b[l, 0], 512, 512)
        xs, xs_b = _matmul_residual_ln(mix_s, w_out, i, xs, ln_g[l, 0], ln_b[l, 0], 256, 512)

        (mkv,) = _matmul(mem2d, wmkv, l, (F32,), 512, 512)
        mk_p, mv_p = mkv[:, :mem_w], mkv[:, mem_w:]
        outs["mkp"].append(mk_p.reshape(bp, n_mem, MEM_HEADS, HEAD_DIM))
        outs["mvp"].append(mv_p.reshape(bp, n_mem, MEM_HEADS, HEAD_DIM))
        xp, xp_b = _mem_attn_ln(xp, wmq, wmo, l, mk_p, mv_p, ln_g[l, 1], ln_b[l, 1], 256, 256, sp)
        xs, xs_b = _mem_attn_ln(xs, wmq, wmo, l, cmk[l], cmv[l], ln_g[l, 1], ln_b[l, 1],
                                min(ms, 128), n_new, n_new)

        act_p, st_p = _ffn_up(xp_b, wu, l, cw_i[l], cb_i[l], None, 1024, 1024, sp, ff_pad)
        act_s, st_s = _ffn_up(xs_b, wu, l, cw_i[l], cb_i[l], st_prev_i[l], ms, n_new, n_new, ff_pad)
        tiles_per_seq = sp // 1024
        outs["fp"].append(_deinterleave_halves(st_p[tiles_per_seq - 1::tiles_per_seq]))
        outs["fs"].append(_deinterleave_halves(st_s))
        xp, xp_b = _matmul_residual_ln(act_p, wd, l, xp, ln_g[l, 2], ln_b[l, 2], 512, 512)
        xs, xs_b = _matmul_residual_ln(act_s, wd, l, xs, ln_g[l, 2], ln_b[l, 2], 256, 1024)

    st = jnp.stack
    o = outs
    return (xp.reshape(bp, sp, d), xs.reshape(bs, n_new, d),
            st(o["akp"]), st(o["avp"]), st(o["bkp"]), st(o["bvp"]), st(o["ckp"]), st(o["cvp"]), st(o["cip"]),
            st(o["mkp"]), st(o["mvp"]), st(o["fp"]),
            st(o["aks"]), st(o["avs"]), st(o["bks"]), st(o["bvs"]), st(o["cks"]), st(o["cvs"]), st(o["cis"]),
            st(o["fs"]))
```

```python
import functools

import numpy as np
import jax
import jax.numpy as jnp
from jax import lax
from jax.experimental import pallas as pl
from jax.experimental.pallas import tpu as pltpu

BF16 = jnp.bfloat16
F32 = jnp.float32

CHUNK = 64
HEAD_DIM = 128
A_LEFT_CHUNKS = 8
REL_CLIP = 128
C_GROUP = 4
IDX_HEADS = 32
IDX_DIM = 64
TOPK_MAX = 256
MEM_HEADS = 4
CONV_W = 3
DEPTH = 4
ALPHA = (2.0 * DEPTH) ** 0.25
LN_EPS = 1e-5
ATTN_SCALE = HEAD_DIM ** -0.5
IDX_SCALE = IDX_DIM ** -0.5
IDX_W_SCALE = IDX_HEADS ** -0.5
NEG_INF = -1e30

V7X_VMEM_BYTES = 64 * 1024 * 1024
LANES = 128
SUBLANES = 8
SB_EXIT = 104.0
INT_MIN = -(2 ** 31)
NEG_KEY = int(np.array(NEG_INF, np.float32).view(np.int32)) ^ 0x7FFFFFFF


def _cparams(vmem_bytes):
    limit = int(min(max(vmem_bytes * 1.25 + (4 << 20), 32 << 20), V7X_VMEM_BYTES - (6 << 20)))
    return pltpu.CompilerParams(vmem_limit_bytes=limit)


def _nt_dot(a, b):
    return lax.dot_general(a, b, (((1,), (1,)), ((), ())), preferred_element_type=F32)


def _mm_body(x_ref, w_ref, *out_refs):
    r = jnp.dot(x_ref[...].astype(BF16), w_ref[...], preferred_element_type=F32)
    for o in out_refs:
        o[...] = r.astype(o.dtype)


def _matmul(x, w, l, out_dtypes, tm, tn):
    m, kd = x.shape
    n = w.shape[2]
    tm, tn = min(tm, m), min(tn, n)
    assert m % tm == 0 and n % tn == 0, (x.shape, w.shape, tm, tn)
    vmem = 2 * (tm * kd * x.dtype.itemsize + kd * tn * 2) + 2 * tm * tn * 4
    vmem += sum(2 * tm * tn * jnp.dtype(d).itemsize for d in out_dtypes)
    return pl.pallas_call(
        _mm_body,
        grid=(m // tm, n // tn),
        in_specs=[pl.BlockSpec((tm, kd), lambda i, j: (i, 0)),
                  pl.BlockSpec((None, kd, tn), lambda i, j: (l, 0, j))],
        out_specs=[pl.BlockSpec((tm, tn), lambda i, j: (i, j)) for _ in out_dtypes],
        out_shape=[jax.ShapeDtypeStruct((m, n), d) for d in out_dtypes],
        compiler_params=_cparams(vmem),
        name="matmul",
    )(x, w)


LN_ROWS = 128


def _residual_ln_rows(y_ref, res_ref, g_ref, b_ref, o_ref, ob_ref, rows):
    g = g_ref[...]
    b = b_ref[...]

    def chunk(c, carry):
        sl = pl.ds(pl.multiple_of(c * LN_ROWS, LN_ROWS), LN_ROWS)
        y = ALPHA * res_ref[sl, :] + y_ref[sl, :]
        mu = jnp.mean(y, axis=-1, keepdims=True)
        d = y - mu
        var = jnp.mean(d * d, axis=-1, keepdims=True)
        out = d * lax.rsqrt(var + LN_EPS) * g + b
        o_ref[sl, :] = out
        ob_ref[sl, :] = out.astype(BF16)
        return carry

    lax.fori_loop(0, rows // LN_ROWS, chunk, 0)


MM_LN_CHUNK = 512


def _mm_ln_body(x_ref, w_ref, res_ref, g_ref, b_ref, o_ref, ob_ref, *, n_k, tm):
    k = pl.program_id(1)
    x = x_ref[...]
    n = o_ref.shape[1]
    chunks = [slice(c, c + MM_LN_CHUNK) for c in range(0, n, MM_LN_CHUNK)]

    @pl.when(k == 0)
    def _():
        for sl in chunks:
            o_ref[:, sl] = jnp.dot(x, w_ref[:, sl], preferred_element_type=F32)

    @pl.when(k > 0)
    def _():
        for sl in chunks:
            o_ref[:, sl] += jnp.dot(x, w_ref[:, sl], preferred_element_type=F32)

    @pl.when(k == n_k - 1)
    def _():
        _residual_ln_rows(o_ref, res_ref, g_ref, b_ref, o_ref, ob_ref, tm)


def _matmul_residual_ln(x, w, l, res, g, b, tm, tk):
    m, kd = x.shape
    n = w.shape[2]
    tm, tk = min(tm, m), min(tk, kd)
    assert m % tm == 0 and kd % tk == 0 and tm % LN_ROWS == 0 and n % MM_LN_CHUNK == 0
    n_k = kd // tk
    vmem = 2 * (tm * tk * 2 + tk * n * 2 + tm * n * 4 + tm * n * 4 + tm * n * 2)
    return pl.pallas_call(
        functools.partial(_mm_ln_body, n_k=n_k, tm=tm),
        grid=(m // tm, n_k),
        in_specs=[pl.BlockSpec((tm, tk), lambda i, k: (i, k)),
                  pl.BlockSpec((None, tk, n), lambda i, k: (l, k, 0)),
                  pl.BlockSpec((tm, n), lambda i, k: (i, 0)),
                  pl.BlockSpec((1, n), lambda i, k: (0, 0)),
                  pl.BlockSpec((1, n), lambda i, k: (0, 0))],
        out_specs=[pl.BlockSpec((tm, n), lambda i, k: (i, 0)),
                   pl.BlockSpec((tm, n), lambda i, k: (i, 0))],
        out_shape=[jax.ShapeDtypeStruct((m, n), F32), jax.ShapeDtypeStruct((m, n), BF16)],
        compiler_params=_cparams(vmem),
        name="matmul_residual_ln",
    )(x, w, res, g.reshape(1, n), b.reshape(1, n))


BAND_TQ = 128
BAND_KB = 5
ATTN_HB = 4


def _band_prompt_body(q_ref, k_ref, v_ref, bias_ref, o_ref):
    t = pl.program_id(2)
    heads = [slice(h * HEAD_DIM, (h + 1) * HEAD_DIM) for h in range(ATTN_HB)]
    first = t - (BAND_KB - 1)
    keys = [pl.ds(pl.multiple_of(jnp.maximum(first + jb, 0) * BAND_TQ, BAND_TQ), BAND_TQ) for jb in range(BAND_KB)]
    s = jnp.concatenate(
        [jnp.concatenate([_nt_dot(q_ref[:, hs], k_ref[keys[jb], hs]) for hs in heads], axis=0)
         for jb in range(BAND_KB)], axis=1)
    s = s * ATTN_SCALE + bias_ref[...].reshape(ATTN_HB * BAND_TQ, BAND_KB * BAND_TQ)
    col_block = lax.broadcasted_iota(jnp.int32, (1, BAND_KB * BAND_TQ), 1) // BAND_TQ
    s = jnp.where(first + col_block >= 0, s, NEG_INF)
    m = jnp.max(s, axis=-1, keepdims=True)
    p = jnp.exp(s - m)
    l = jnp.sum(p, axis=-1, keepdims=True)
    p = p.astype(BF16)
    for h, hs in enumerate(heads):
        rows = slice(h * BAND_TQ, (h + 1) * BAND_TQ)
        acc = sum(jnp.dot(p[rows, jb * BAND_TQ:(jb + 1) * BAND_TQ], v_ref[keys[jb], hs],
                          preferred_element_type=F32) for jb in range(BAND_KB))
        o_ref[:, hs] = (acc / l[rows]).astype(o_ref.dtype)


def _toeplitz_bias(rel_bias, n_rows, n_cols, offset):
    m = np.arange(n_rows + n_cols - 1) - (n_rows - 1)
    diag = rel_bias[:, np.clip(offset - m, -REL_CLIP, REL_CLIP) + REL_CLIP]
    rows = [diag[:, n_rows - 1 - i:n_rows - 1 - i + n_cols] for i in range(n_rows)]
    return jnp.stack(rows, axis=1).astype(F32)


def _band_attn_prompt(qkv, rel_bias, batch, seq, n_heads, q_col, k_col, v_col):
    assert n_heads % ATTN_HB == 0 and q_col % ATTN_HB == 0 and k_col % ATTN_HB == 0 and v_col % ATTN_HB == 0
    nt = seq // BAND_TQ
    hw = ATTN_HB * HEAD_DIM
    bias = _toeplitz_bias(rel_bias, BAND_TQ, BAND_KB * BAND_TQ, A_LEFT_CHUNKS * CHUNK)
    qi = np.arange(BAND_TQ)[:, None] // CHUNK
    kj = np.arange(BAND_KB * BAND_TQ)[None, :] // CHUNK
    in_band = (kj >= qi) & (kj <= qi + A_LEFT_CHUNKS)
    bias = jnp.where(jnp.asarray(in_band)[None], bias, NEG_INF)
    vmem = 2 * (2 * seq * hw * 2 + ATTN_HB * BAND_TQ * BAND_KB * BAND_TQ * 4) + (8 << 20)
    return pl.pallas_call(
        _band_prompt_body,
        grid=(batch, n_heads // ATTN_HB, nt),
        in_specs=[pl.BlockSpec((BAND_TQ, hw), lambda b, h, t: (b * nt + t, q_col // ATTN_HB + h)),
                  pl.BlockSpec((seq, hw), lambda b, h, t: (b, k_col // ATTN_HB + h)),
                  pl.BlockSpec((seq, hw), lambda b, h, t: (b, v_col // ATTN_HB + h)),
                  pl.BlockSpec((ATTN_HB, BAND_TQ, BAND_KB * BAND_TQ), lambda b, h, t: (h, 0, 0))],
        out_specs=pl.BlockSpec((BAND_TQ, hw), lambda b, h, t: (b * nt + t, h)),
        out_shape=jax.ShapeDtypeStruct((batch * seq, n_heads * HEAD_DIM), BF16),
        compiler_params=_cparams(vmem),
        name="band_attn_prompt",
    )(qkv, qkv, qkv, bias)


def _bias_attn_body(q_ref, k_ref, v_ref, bias_ref, o_ref, *, n_heads):
    for h in range(n_heads):
        hs = slice(h * HEAD_DIM, (h + 1) * HEAD_DIM)
        s = _nt_dot(q_ref[:, hs], k_ref[:, hs]) * ATTN_SCALE + bias_ref[h]
        m = jnp.max(s, axis=-1, keepdims=True)
        p = jnp.exp(s - m)
        l = jnp.sum(p, axis=-1, keepdims=True)
        acc = jnp.dot(p.astype(BF16), v_ref[:, hs], preferred_element_type=F32)
        o_ref[:, hs] = (acc / l).astype(o_ref.dtype)


def _band_attn_sample(q2d, k_all, v_all, rel_bias, batch, n_new, n_heads):
    L = k_all.shape[0] // batch
    hw = n_heads * HEAD_DIM
    bias = _toeplitz_bias(rel_bias, n_new, L, L - n_new)
    return pl.pallas_call(
        functools.partial(_bias_attn_body, n_heads=n_heads),
        grid=(batch,),
        in_specs=[pl.BlockSpec((n_new, hw), lambda b: (b, 0)),
                  pl.BlockSpec((L, hw), lambda b: (b, 0)),
                  pl.BlockSpec((L, hw), lambda b: (b, 0)),
                  pl.BlockSpec((n_heads, n_new, L), lambda b: (0, 0, 0))],
        out_specs=pl.BlockSpec((n_new, hw), lambda b: (b, 0)),
        out_shape=jax.ShapeDtypeStruct((batch * n_new, hw), BF16),
        name="band_attn_sample",
    )(q2d, k_all, v_all, bias)


SB_TK = 256


def _split2(x):
    hi = x.astype(BF16)
    return hi, (x - hi.astype(F32)).astype(BF16)


def _stick_break_body(q_ref, k_ref, v_ref, u_ref, o_ref, *, tq, qpos_base):
    qt = pl.program_id(2)
    qpos0 = qpos_base + qt * tq
    u = u_ref[...]
    rows = ATTN_HB * tq
    rowpos = qpos0 + lax.broadcasted_iota(jnp.int32, (rows, 1), 0) % tq
    top = (qpos0 + tq + LANES - 1) // LANES * LANES
    heads = [slice(h * HEAD_DIM, (h + 1) * HEAD_DIM) for h in range(ATTN_HB)]

    def cond(c):
        limit, floor = c[0], c[1]
        return jnp.logical_and(limit > 0, floor > -SB_EXIT)

    def body(c):
        limit, _, carry, acc = c
        start = pl.multiple_of(jnp.maximum(limit - SB_TK, 0), LANES)
        keys = pl.ds(start, SB_TK)
        kpos = start + lax.broadcasted_iota(jnp.int32, (1, SB_TK), 1)
        mask = jnp.logical_and(kpos < rowpos, kpos < limit)
        z = jnp.concatenate([_nt_dot(q_ref[:, hs], k_ref[keys, hs]) for hs in heads], axis=0) * ATTN_SCALE
        sp = jnp.maximum(z, 0.0) + jnp.log(1.0 + jnp.exp(-jnp.abs(z)))
        log_1m = jnp.where(mask, -sp, 0.0)
        suffix = sum(jnp.dot(piece, u, preferred_element_type=F32) for piece in _split2(log_1m))
        w = jnp.where(mask, jnp.exp(z - sp + suffix + carry), 0.0).astype(BF16)
        acc = acc + jnp.concatenate(
            [jnp.dot(w[h * tq:(h + 1) * tq], v_ref[keys, hs], preferred_element_type=F32)
             for h, hs in enumerate(heads)], axis=0)
        carry = carry + jnp.sum(log_1m, axis=-1, keepdims=True)
        return limit - SB_TK, jnp.max(carry), carry, acc

    init = (top, jnp.float32(0.0), jnp.zeros((rows, 1), F32), jnp.zeros((rows, HEAD_DIM), F32))
    acc = lax.while_loop(cond, body, init)[3]
    for h, hs in enumerate(heads):
        o_ref[:, hs] = acc[h * tq:(h + 1) * tq].astype(o_ref.dtype)


def _stick_break_attn(q2d, k2d, v2d, batch, n_q, n_k, n_heads, tq, qpos_base, q_col, k_col, v_col):
    assert n_q % tq == 0 and n_k % LANES == 0 and n_k >= SB_TK and n_heads % ATTN_HB == 0
    assert q_col % ATTN_HB == 0 and k_col % ATTN_HB == 0 and v_col % ATTN_HB == 0
    assert -(-(qpos_base + n_q) // LANES) * LANES <= n_k
    nt = n_q // tq
    hw = ATTN_HB * HEAD_DIM
    j = np.arange(SB_TK)
    u = jnp.asarray(j[:, None] > j[None, :], BF16)
    vmem = 2 * (2 * n_k * hw * 2) + (8 << 20)
    return pl.pallas_call(
        functools.partial(_stick_break_body, tq=tq, qpos_base=qpos_base),
        grid=(batch, n_heads // ATTN_HB, nt),
        in_specs=[pl.BlockSpec((tq, hw), lambda b, h, t: (b * nt + t, q_col // ATTN_HB + h)),
                  pl.BlockSpec((n_k, hw), lambda b, h, t: (b, k_col // ATTN_HB + h)),
                  pl.BlockSpec((n_k, hw), lambda b, h, t: (b, v_col // ATTN_HB + h)),
                  pl.BlockSpec((SB_TK, SB_TK), lambda b, h, t: (0, 0))],
        out_specs=pl.BlockSpec((tq, hw), lambda b, h, t: (b * nt + t, h)),
        out_shape=jax.ShapeDtypeStruct((batch * n_q, n_heads * HEAD_DIM), BF16),
        compiler_params=_cparams(vmem),
        name="stick_break_attn",
    )(q2d, k2d, v2d, u)


def _sortable(x):
    bits = pltpu.bitcast(x, jnp.int32)
    return jnp.where(bits < 0, bits ^ jnp.int32(0x7FFFFFFF), bits)


def _dsa_index_body(qi_ref, wi_ref, kz_ref, bias_ref, key_ref, *, tq, tk, n_real, qpos_base, topk):
    n_pad = key_ref.shape[1]
    qt = pl.program_id(1)
    qpos0 = qpos_base + qt * tq
    rowpos = qpos0 + lax.broadcasted_iota(jnp.int32, (tq, 1), 0)
    qchunk = rowpos // CHUNK
    n_kb = jnp.minimum((((qpos0 + tq - 1) // CHUNK + 1) * CHUNK + tk - 1) // tk, n_pad // tk)
    n_tail = jnp.maximum(n_real - n_kb * tk, 0)
    tail0 = n_kb * tk

    w = wi_ref[...] * (IDX_SCALE * IDX_W_SCALE)

    def block_pos(kb):
        start = pl.multiple_of(kb * tk, tk)
        return start, start + lax.broadcasted_iota(jnp.int32, (1, tk), 1)

    def score_block(kb, carry):
        start, kpos = block_pos(kb)
        kz = kz_ref[pl.ds(start, tk), :]
        acc = jnp.zeros((tq, tk), F32)
        for pair in range(IDX_HEADS // 2):
            qp = qi_ref[:, pair * LANES:(pair + 1) * LANES]
            for half in range(2):
                h = 2 * pair + half
                sc = _nt_dot(qp, kz[:, half * LANES:(half + 1) * LANES])
                acc = acc + jnp.maximum(sc, 0.0) * w[:, h:h + 1]
        key = _sortable(jnp.where(kpos // CHUNK <= qchunk, acc, NEG_INF))
        key_ref[:, pl.ds(start, tk)] = jnp.where(kpos < n_real, key, jnp.int32(INT_MIN))
        return carry

    lax.fori_loop(0, n_kb, score_block, 0)

    def count(pred):
        def blk(kb, acc):
            start, kpos = block_pos(kb)
            hit = pred(key_ref[:, pl.ds(start, tk)], kpos).astype(F32)
            return acc + sum(hit[:, c:c + LANES] for c in range(0, tk, LANES))
        lanes = lax.fori_loop(0, n_kb, blk, jnp.zeros((tq, LANES), F32))
        return jnp.sum(lanes, axis=-1, keepdims=True)

    kf = jnp.float32(topk)
    tail_f = n_tail.astype(F32)
    neg_key = jnp.int32(NEG_KEY)

    def thr_step(i, ans_u):
        cand_u = ans_u | lax.shift_left(jnp.int32(1), 31 - i)
        cand = cand_u ^ jnp.int32(INT_MIN)
        cnt = count(lambda key, kpos: key >= cand) + jnp.where(neg_key >= cand, tail_f, 0.0)
        return jnp.where(cnt >= kf, cand_u, ans_u)

    thr = lax.fori_loop(0, 32, thr_step, jnp.zeros((tq, 1), jnp.int32)) ^ jnp.int32(INT_MIN)
    tail_gt = jnp.where(neg_key > thr, tail_f, 0.0)
    n_gt = count(lambda key, kpos: key > thr) + tail_gt
    n_eq = count(lambda key, kpos: key == thr) + jnp.where(neg_key == thr, tail_f, 0.0)
    need = kf - n_gt

    n_bits = int(np.ceil(np.log2(n_pad)))

    def tie_search():
        def pos_step(i, lo):
            cand = lo + lax.shift_left(jnp.int32(1), n_bits - 1 - i)
            in_tail = jnp.clip(cand - tail0, 0, n_tail).astype(F32)
            cnt = (count(lambda key, kpos: jnp.logical_and(key == thr, kpos < cand))
                   + jnp.where(neg_key == thr, in_tail, 0.0))
            return jnp.where(cnt < need, cand, lo)
        return lax.fori_loop(0, n_bits, pos_step, jnp.zeros((tq, 1), jnp.int32))

    has_ties = jnp.max(jnp.where(n_eq > need, 1.0, 0.0)) > 0.0
    last = lax.cond(has_ties, tie_search, lambda: jnp.full((tq, 1), n_pad, jnp.int32))

    bias_ref[0] = jnp.full(bias_ref.shape[1:], NEG_INF, bias_ref.dtype)

    def write_block(kb, carry):
        start, kpos = block_pos(kb)
        key = key_ref[:, pl.ds(start, tk)]
        sel = jnp.logical_or(key > thr, jnp.logical_and(key == thr, kpos <= last))
        ok = jnp.logical_and(jnp.logical_and(sel, kpos // CHUNK <= qchunk), kpos < n_real)
        bias_ref[0, :, pl.ds(start, tk)] = jnp.where(ok, 0.0, NEG_INF).astype(bias_ref.dtype)
        return carry

    lax.fori_loop(0, n_kb, write_block, 0)


def _dsa_index(qi2d, wi, kz, batch, n_q, n_k, n_real, tq, tk, qpos_base, qi_col):
    nt = n_q // tq
    topk = min(TOPK_MAX, n_real // 4)
    vmem = 2 * (tq * 2048 * 2 + n_k * 256 * 2 + tq * n_k * 2) + tq * n_k * 4 + 8 * tq * tk * 4 + (4 << 20)
    return pl.pallas_call(
        functools.partial(_dsa_index_body, tq=tq, tk=tk, n_real=n_real, qpos_base=qpos_base, topk=topk),
        grid=(batch, nt),
        in_specs=[pl.BlockSpec((tq, IDX_HEADS * IDX_DIM), lambda b, t: (b * nt + t, qi_col)),
                  pl.BlockSpec((tq, IDX_HEADS), lambda b, t: (b * nt + t, 0)),
                  pl.BlockSpec((n_k, 2 * LANES), lambda b, t: (b, 0))],
        out_specs=pl.BlockSpec((1, tq, n_k), lambda b, t: (b, t, 0)),
        out_shape=jax.ShapeDtypeStruct((batch, n_q, n_k), BF16),
        scratch_shapes=[pltpu.VMEM((tq, n_k), jnp.int32)],
        compiler_params=_cparams(vmem),
        name="dsa_index",
    )(qi2d, wi, kz)


def _dsa_attn_body(q_ref, k_ref, v_ref, bias_ref, slope_ref, o_ref, s_ref, p_ref, *, tq, tk, qpos_base):
    n_pad = k_ref.shape[0]
    qt = pl.program_id(2)
    qpos0 = qpos_base + qt * tq
    tile_end = qpos0 + tq
    rows = C_GROUP * tq
    rowpos = qpos0 + lax.broadcasted_iota(jnp.int32, (tq, 1), 0)
    n_kb = jnp.minimum((((qpos0 + tq - 1) // CHUNK + 1) * CHUNK + tk - 1) // tk, n_pad // tk)
    qs = jnp.concatenate([q_ref[:, g * HEAD_DIM:(g + 1) * HEAD_DIM] for g in range(C_GROUP)], axis=0)

    def key_rows(kb):
        return pl.ds(pl.multiple_of(kb * tk, tk), tk)

    def scores(kb):
        s_ref[kb % 2] = _nt_dot(qs, k_ref[key_rows(kb), :])

    def weighted_values(kb):
        return jnp.dot(p_ref[(kb + 2) % 2], v_ref[key_rows(jnp.maximum(kb, 0)), :], preferred_element_type=F32)

    slope = slope_ref[0]

    def softmax(kb, m, l, has_later_keys):
        kpos = kb * tk + lax.broadcasted_iota(jnp.int32, (1, tk), 1)
        back = (tile_end - kpos).astype(F32)
        s = s_ref[kb % 2] * ATTN_SCALE - slope * back
        extra = bias_ref[0, :, key_rows(kb)].astype(F32)
        s = (s.reshape(C_GROUP, tq, tk) + extra[None]).reshape(rows, tk)
        if has_later_keys:
            ahead = jnp.where(kpos > rowpos, (rowpos - kpos).astype(F32), 0.0)
            s = s + (2.0 * slope) * jnp.concatenate([ahead] * C_GROUP, axis=0)
        m_new = jnp.maximum(m, jnp.max(s, axis=-1, keepdims=True))
        p = jnp.exp(s - m_new)
        p_ref[kb % 2] = p.astype(BF16)
        a = jnp.exp(m - m_new)
        return m_new, a * l + jnp.sum(p, axis=-1, keepdims=True), a

    def trip(kb, state):
        m, l, a_prev, acc = state
        scores(kb + 1)
        acc = a_prev * acc + weighted_values(kb - 1)
        m, l, a = softmax(kb, m, l, False)
        return m, l, a, acc

    p_ref[1] = jnp.zeros(p_ref.shape[1:], BF16)
    scores(0)
    init = (jnp.full((rows, 1), 0.5 * NEG_INF, F32), jnp.zeros((rows, 1), F32),
            jnp.ones((rows, 1), F32), jnp.zeros((rows, HEAD_DIM), F32))
    last = n_kb - 1
    m, l, a_prev, acc = lax.fori_loop(0, last, trip, init)
    acc = a_prev * acc + weighted_values(last - 1)
    m, l, a = softmax(last, m, l, True)
    acc = a * acc + weighted_values(last)
    o = (acc / l).astype(o_ref.dtype)
    for g in range(C_GROUP):
        o_ref[:, g * HEAD_DIM:(g + 1) * HEAD_DIM] = o[g * tq:(g + 1) * tq]


def _dsa_attn(q2d, k2d, v2d, bias, batch, n_q, n_k, n_kv, tq, tk, qpos_base, k_col, v_col):
    assert tk % tq == 0 and qpos_base % tq == 0 and n_k % tk == 0
    nt = n_q // tq
    gw = C_GROUP * HEAD_DIM
    n_heads = n_kv * C_GROUP
    slopes = 2.0 ** (-8.0 * np.arange(1, n_heads + 1) / n_heads)
    slope_rows = np.repeat(slopes.reshape(n_kv, C_GROUP), tq, axis=1)[..., None].astype(np.float32)
    rows = C_GROUP * tq
    vmem = (2 * (2 * n_k * HEAD_DIM * 2 + tq * n_k * 2 + 2 * tq * gw * 2 + rows * LANES * 4)
            + 2 * rows * tk * 6 + 8 * rows * tk * 4 + (4 << 20))
    return pl.pallas_call(
        functools.partial(_dsa_attn_body, tq=tq, tk=tk, qpos_base=qpos_base),
        grid=(batch, n_kv, nt),
        in_specs=[pl.BlockSpec((tq, gw), lambda b, n, t: (b * nt + t, n)),
                  pl.BlockSpec((n_k, HEAD_DIM), lambda b, n, t: (b, k_col + n)),
                  pl.BlockSpec((n_k, HEAD_DIM), lambda b, n, t: (b, v_col + n)),
                  pl.BlockSpec((1, tq, n_k), lambda b, n, t: (b, t, 0)),
                  pl.BlockSpec((1, rows, 1), lambda b, n, t: (n, 0, 0))],
        out_specs=pl.BlockSpec((tq, gw), lambda b, n, t: (b * nt + t, n)),
        out_shape=jax.ShapeDtypeStruct((batch * n_q, n_heads * HEAD_DIM), BF16),
        scratch_shapes=[pltpu.VMEM((2, rows, tk), F32), pltpu.VMEM((2, rows, tk), BF16)],
        compiler_params=_cparams(vmem),
        name="dsa_attn",
    )(q2d, k2d, v2d, bias, jnp.asarray(slope_rows))


def _mem_attn_body(x_ref, wq_ref, mk_ref, mv_ref, wo_ref, g_ref, b_ref, o_ref, ob_ref, y_ref, *,
                   tm, rows_per_seq, n_mem):
    q = jnp.dot(x_ref[...].astype(BF16), wq_ref[...], preferred_element_type=F32).astype(BF16)
    n_seq = tm // rows_per_seq
    heads = [slice(h * HEAD_DIM, (h + 1) * HEAD_DIM) for h in range(MEM_HEADS)]
    s = jnp.concatenate(
        [_nt_dot(q[s_i * rows_per_seq:(s_i + 1) * rows_per_seq, hs],
                 mk_ref[s_i * n_mem:(s_i + 1) * n_mem, hs].astype(BF16))
         for hs in heads for s_i in range(n_seq)], axis=0) * ATTN_SCALE
    m = jnp.max(s, axis=-1, keepdims=True)
    p = jnp.exp(s - m)
    l = jnp.sum(p, axis=-1, keepdims=True)
    p = p.astype(BF16)
    o_heads = []
    for h, hs in enumerate(heads):
        pieces = []
        for s_i in range(n_seq):
            rows = slice(h * tm + s_i * rows_per_seq, h * tm + (s_i + 1) * rows_per_seq)
            o = jnp.dot(p[rows], mv_ref[s_i * n_mem:(s_i + 1) * n_mem, hs].astype(BF16),
                        preferred_element_type=F32)
            pieces.append((o / l[rows]).astype(BF16))
        o_heads.append(pieces[0] if n_seq == 1 else jnp.concatenate(pieces, axis=0))
    y_ref[...] = jnp.dot(jnp.concatenate(o_heads, axis=-1), wo_ref[...], preferred_element_type=F32)
    _residual_ln_rows(y_ref, x_ref, g_ref, b_ref, o_ref, ob_ref, tm)


def _mem_attn_ln(x, wq, wo, l, mk, mv, g, b, tm, rows_per_seq, seq_rows):
    m, d = x.shape
    mw = wq.shape[2]
    n_seq_tile = tm // rows_per_seq
    n_mem = mk.shape[0] // (m // seq_rows)
    if n_seq_tile == 1:
        mem_map = lambda i: ((i * tm) // seq_rows, 0)
    else:
        assert rows_per_seq == seq_rows
        mem_map = lambda i: (i, 0)
    vmem = 2 * (tm * d * 4 * 2 + tm * d * 2 + 2 * d * mw * 2 + 2 * n_seq_tile * n_mem * mw * 4) + tm * d * 4
    return pl.pallas_call(
        functools.partial(_mem_attn_body, tm=tm, rows_per_seq=rows_per_seq, n_mem=n_mem),
        grid=(m // tm,),
        in_specs=[pl.BlockSpec((tm, d), lambda i: (i, 0)),
                  pl.BlockSpec((None, d, mw), lambda i: (l, 0, 0)),
                  pl.BlockSpec((n_seq_tile * n_mem, mw), mem_map),
                  pl.BlockSpec((n_seq_tile * n_mem, mw), mem_map),
                  pl.BlockSpec((None, mw, d), lambda i: (l, 0, 0)),
                  pl.BlockSpec((1, d), lambda i: (0, 0)),
                  pl.BlockSpec((1, d), lambda i: (0, 0))],
        out_specs=[pl.BlockSpec((tm, d), lambda i: (i, 0)),
                   pl.BlockSpec((tm, d), lambda i: (i, 0))],
        out_shape=[jax.ShapeDtypeStruct((m, d), F32), jax.ShapeDtypeStruct((m, d), BF16)],
        scratch_shapes=[pltpu.VMEM((tm, d), F32)],
        compiler_params=_cparams(vmem),
        name="mem_attn_ln",
    )(x, wq, mk, mv, wo, g.reshape(1, d), b.reshape(1, d))


FFN_TN = 256
FFN_SUB = 256
HALO = 8


def _gelu(x):
    return 0.5 * x * (1.0 + lax.erf(x * np.float32(2.0 ** -0.5)))


def _ffn_up_body(x_ref, wa_ref, wg_ref, cwa_ref, cwg_ref, cba_ref, cbg_ref, *rest,
                 tm, rows_per_seq, tiles_per_seq, carried, nj):
    if carried:
        act_ref, sta_ref, stg_ref, carry_ref, ext_ref = rest
    else:
        pa_ref, pg_ref, act_ref, sta_ref, stg_ref, ext_ref = rest
    i = pl.program_id(0)
    j = pl.program_id(1)

    def conv_gate(base, rows):
        cw = jnp.concatenate([cwa_ref[...], cwg_ref[...]], axis=1)
        cb = jnp.concatenate([cba_ref[...], cbg_ref[...]], axis=1)
        h = (ext_ref[base:base + rows, :] * cw[2:3, :]
             + ext_ref[base - 1:base - 1 + rows, :] * cw[1:2, :]
             + ext_ref[base - 2:base - 2 + rows, :] * cw[0:1, :]
             + cb)
        return (h[:, :FFN_TN] * _gelu(h[:, FFN_TN:])).astype(act_ref.dtype)

    def put_state(s_i, last2):
        sta_ref[s_i] = last2[:, :FFN_TN]
        stg_ref[s_i] = last2[:, FFN_TN:]

    @pl.when(j < nj)
    def _():
        if carried:
            @pl.when(i % tiles_per_seq == 0)
            def _():
                carry_ref[j] = jnp.zeros((2, 2 * FFN_TN), F32)

            ext_ref[HALO - 2:HALO, :] = carry_ref[j]
            sub = min(FFN_SUB, tm)
            for r in range(0, tm, sub):
                ext_ref[HALO + r:HALO + r + sub, :FFN_TN] = jnp.dot(x_ref[r:r + sub, :], wa_ref[...],
                                                                    preferred_element_type=F32)
                ext_ref[HALO + r:HALO + r + sub, FFN_TN:] = jnp.dot(x_ref[r:r + sub, :], wg_ref[...],
                                                                    preferred_element_type=F32)
                act_ref[r:r + sub, :] = conv_gate(HALO + r, sub)
            last2 = ext_ref[HALO + tm - 2:HALO + tm, :]
            carry_ref[j] = last2
            put_state(0, last2)
        else:
            up = jnp.concatenate([jnp.dot(x_ref[...], wa_ref[...], preferred_element_type=F32),
                                  jnp.dot(x_ref[...], wg_ref[...], preferred_element_type=F32)], axis=1)
            for s_i in range(tm // rows_per_seq):
                base = s_i * (rows_per_seq + HALO) + HALO
                rsl = slice(s_i * rows_per_seq, (s_i + 1) * rows_per_seq)
                ext_ref[base - 2:base, :] = jnp.concatenate([pa_ref[s_i], pg_ref[s_i]], axis=1)
                ext_ref[base:base + rows_per_seq, :] = up[rsl]
                put_state(s_i, ext_ref[base + rows_per_seq - 2:base + rows_per_seq, :])
                act_ref[rsl, :] = conv_gate(base, rows_per_seq)

    @pl.when(j >= nj)
    def _():
        act_ref[...] = jnp.zeros_like(act_ref)


def _ffn_up(x, w_up, l, conv_w, conv_b, prev, tm, rows_per_seq, seq_rows, n_act_cols):
    m, d = x.shape
    f2 = w_up.shape[2]
    wt = 2 * FFN_TN
    assert f2 % wt == 0 and n_act_cols % FFN_TN == 0
    nj = f2 // wt
    nj_pad = n_act_cols // FFN_TN
    carried = prev is None
    n_seq = tm // rows_per_seq
    tiles_per_seq = max(seq_rows // tm, 1)
    n_tiles = m // tm
    val = lambda j: jnp.minimum(j, nj - 1)
    in_specs = [pl.BlockSpec((tm, d), lambda i, j: (i, 0)),
                pl.BlockSpec((None, d, FFN_TN), lambda i, j: (l, 0, val(j))),
                pl.BlockSpec((None, d, FFN_TN), lambda i, j: (l, 0, val(j) + nj)),
                pl.BlockSpec((CONV_W, FFN_TN), lambda i, j: (0, val(j))),
                pl.BlockSpec((CONV_W, FFN_TN), lambda i, j: (0, val(j) + nj)),
                pl.BlockSpec((1, FFN_TN), lambda i, j: (0, val(j))),
                pl.BlockSpec((1, FFN_TN), lambda i, j: (0, val(j) + nj))]
    args = [x, w_up, w_up, conv_w, conv_w, conv_b, conv_b]
    scratch = []
    if carried:
        assert n_seq == 1
        scratch.append(pltpu.VMEM((nj, 2, wt), F32))
    else:
        in_specs += [pl.BlockSpec((n_seq, 2, FFN_TN), lambda i, j: (i, 0, val(j))),
                     pl.BlockSpec((n_seq, 2, FFN_TN), lambda i, j: (i, 0, val(j) + nj))]
        args += [prev, prev]
    scratch.append(pltpu.VMEM((n_seq * (rows_per_seq + HALO), wt), F32))
    st_shape = jax.ShapeDtypeStruct((n_tiles * n_seq, 2, f2 // 2), F32)
    st_spec = pl.BlockSpec((n_seq, 2, FFN_TN), lambda i, j: (i, 0, val(j)))
    vmem = 2 * (tm * d * 2 + d * wt * 2 + tm * FFN_TN * 2) + tm * wt * 4 + 8 * min(tm, FFN_SUB) * wt * 4 + (4 << 20)
    act, st_a, st_g = pl.pallas_call(
        functools.partial(_ffn_up_body, tm=tm, rows_per_seq=rows_per_seq, tiles_per_seq=tiles_per_seq,
                          carried=carried, nj=nj),
        grid=(n_tiles, nj_pad),
        in_specs=in_specs,
        out_specs=[pl.BlockSpec((tm, FFN_TN), lambda i, j: (i, j)), st_spec, st_spec],
        out_shape=[jax.ShapeDtypeStruct((m, n_act_cols), BF16), st_shape, st_shape],
        scratch_shapes=scratch,
        compiler_params=_cparams(vmem),
        name="ffn_up_conv_gate",
    )(*args)
    return act, jnp.concatenate([st_a, st_g], axis=-1)


def _pad_rows(a, n_rows):
    return jnp.pad(a, ((0, 0), (0, n_rows - a.shape[1]), (0, 0)))


def kernel(x_prompt, x_sample, cache_a_k, cache_a_v, cache_b_k, cache_b_v, cache_c_k, cache_c_v, cache_c_idx, cache_mem_k, cache_mem_v, state_ffn_conv, mem_prompt, w_in_even, w_out_even, rel_bias_a, w_in_odd, w_out_odd, w_mem_q, w_mem_kv, w_mem_o, w_up, conv_w, conv_b, w_down, ln_g, ln_b):
    bp, sp, d = x_prompt.shape
    bs, n_new, _ = x_sample.shape
    depth = w_up.shape[0]
    h_a = cache_a_k.shape[3]
    h_b = cache_b_k.shape[3]
    n_kv = cache_c_k.shape[3]
    h_c = n_kv * C_GROUP
    hw_a, hw_b, kvw = h_a * HEAD_DIM, h_b * HEAD_DIM, n_kv * HEAD_DIM
    qiw = IDX_HEADS * IDX_DIM
    a_win_s = cache_a_k.shape[2]
    a_win_p = min(A_LEFT_CHUNKS * CHUNK, sp)
    past = cache_b_k.shape[2]
    n_mem = mem_prompt.shape[1]
    mem_w = w_mem_q.shape[2]
    d_ff = w_down.shape[1]
    mp, ms = bp * sp, bs * n_new
    l_s = past + n_new
    l_s_pad = -(-l_s // LANES) * LANES
    main_odd = h_c * HEAD_DIM + 2 * kvw + qiw
    tail_pad = LANES
    ff_pad = -(-d_ff // 1024) * 1024

    wie = w_in_even.astype(BF16)
    woe = w_out_even.astype(BF16)
    wio_main = w_in_odd[:, :, :main_odd].astype(BF16)
    wio_tail = jnp.pad(w_in_odd[:, :, main_odd:],
                       ((0, 0), (0, 0), (0, tail_pad - (IDX_DIM + IDX_HEADS)))).astype(BF16)
    woo = w_out_odd.astype(BF16)
    wmq = w_mem_q.astype(BF16)
    wmkv = w_mem_kv.astype(BF16)
    wmo = w_mem_o.astype(BF16)
    wu = w_up.astype(BF16)
    wd = jnp.pad(w_down, ((0, 0), (0, ff_pad - d_ff), (0, 0))).astype(BF16)

    xp = x_prompt.reshape(mp, d)
    xs = x_sample.reshape(ms, d)
    xp_b, xs_b = xp.astype(BF16), xs.astype(BF16)
    mem2d = mem_prompt.reshape(bp * n_mem, d).astype(BF16)
    cmk = cache_mem_k.reshape(depth, bs * n_mem, mem_w).astype(BF16)
    cmv = cache_mem_v.reshape(depth, bs * n_mem, mem_w).astype(BF16)

    outs = {k: [] for k in ("akp", "avp", "bkp", "bvp", "ckp", "cvp", "cip", "mkp", "mvp", "fp",
                            "aks", "avs", "bks", "bvs", "cks", "cvs", "cis", "fs")}

    for l in range(depth):
        i = l // 2
        if l % 2 == 0:
            w_out = woe
            pf, pb = _matmul(xp_b, wie, i, (F32, BF16), 1024, 512)
            oa = _band_attn_prompt(pb, rel_bias_a[i], bp, sp, h_a, 0, h_a, 2 * h_a)
            cb0 = 3 * h_a
            ob = _stick_break_attn(pb, pb, pb, bp, sp, sp, h_b, 128, 0, cb0, cb0 + h_b, cb0 + 2 * h_b)
            mix_p = jnp.concatenate([oa, ob], axis=-1)
            pf4 = pf.reshape(bp, sp, -1)
            outs["akp"].append(pf4[:, sp - a_win_p:, hw_a:2 * hw_a].reshape(bp, a_win_p, h_a, HEAD_DIM))
            outs["avp"].append(pf4[:, sp - a_win_p:, 2 * hw_a:3 * hw_a].reshape(bp, a_win_p, h_a, HEAD_DIM))
            o0 = 3 * hw_a
            outs["bkp"].append(pf4[:, :, o0 + hw_b:o0 + 2 * hw_b].reshape(bp, sp, h_b, HEAD_DIM))
            outs["bvp"].append(pf4[:, :, o0 + 2 * hw_b:o0 + 3 * hw_b].reshape(bp, sp, h_b, HEAD_DIM))
            sf, sb = _matmul(xs_b, wie, i, (F32, BF16), 256, 512)
            sf4 = sf.reshape(bs, n_new, -1)
            ka_all = jnp.concatenate([cache_a_k[i].reshape(bs, a_win_s, hw_a), sf4[:, :, hw_a:2 * hw_a]], axis=1)
            va_all = jnp.concatenate([cache_a_v[i].reshape(bs, a_win_s, hw_a), sf4[:, :, 2 * hw_a:3 * hw_a]], axis=1)
            la = a_win_s + n_new
            oa_s = _band_attn_sample(sb, ka_all.astype(BF16).reshape(bs * la, hw_a),
                                     va_all.astype(BF16).reshape(bs * la, hw_a), rel_bias_a[i], bs, n_new, h_a)
            kb_new = sf4[:, :, o0 + hw_b:o0 + 2 * hw_b]
            vb_new = sf4[:, :, o0 + 2 * hw_b:o0 + 3 * hw_b]
            kb_all = _pad_rows(jnp.concatenate([cache_b_k[i].reshape(bs, past, hw_b), kb_new], axis=1), l_s_pad)
            vb_all = _pad_rows(jnp.concatenate([cache_b_v[i].reshape(bs, past, hw_b), vb_new], axis=1), l_s_pad)
            ob_s = _stick_break_attn(sb, kb_all.astype(BF16).reshape(bs * l_s_pad, hw_b),
                                     vb_all.astype(BF16).reshape(bs * l_s_pad, hw_b),
                                     bs, n_new, l_s_pad, h_b, n_new, past, cb0, 0, 0)
            mix_s = jnp.concatenate([oa_s, ob_s], axis=-1)
            outs["aks"].append(ka_all[:, la - a_win_s:].reshape(bs, a_win_s, h_a, HEAD_DIM))
            outs["avs"].append(va_all[:, la - a_win_s:].reshape(bs, a_win_s, h_a, HEAD_DIM))
            outs["bks"].append(kb_new.reshape(bs, n_new, h_b, HEAD_DIM))
            outs["bvs"].append(vb_new.reshape(bs, n_new, h_b, HEAD_DIM))
        else:
            w_out = woo
            kc0 = h_c
            vc0 = h_c + n_kv
            qic = (h_c * HEAD_DIM + 2 * kvw) // qiw

            def kz_of(ki):
                z = jnp.zeros_like(ki)
                return jnp.concatenate([ki, z, z, ki], axis=-1).astype(BF16).reshape(-1, 2 * LANES)

            pf, pb = _matmul(xp_b, wio_main, i, (F32, BF16), 1024, 512)
            (pt,) = _matmul(xp_b, wio_tail, i, (F32,), 1024, tail_pad)
            ki_p = pt[:, :IDX_DIM].reshape(bp, sp, IDX_DIM)
            wi_p = pt[:, IDX_DIM:IDX_DIM + IDX_HEADS]
            sel_p = _dsa_index(pb, wi_p, kz_of(ki_p), bp, sp, sp, sp, 128, 512, 0, qic)
            mix_p = _dsa_attn(pb, pb, pb, sel_p, bp, sp, sp, n_kv, 128, 512, 0, kc0, vc0)
            pf4 = pf.reshape(bp, sp, -1)
            q_w = h_c * HEAD_DIM
            outs["ckp"].append(pf4[:, :, q_w:q_w + kvw].reshape(bp, sp, n_kv, HEAD_DIM))
            outs["cvp"].append(pf4[:, :, q_w + kvw:q_w + 2 * kvw].reshape(bp, sp, n_kv, HEAD_DIM))
            outs["cip"].append(ki_p)
            sf, sb = _matmul(xs_b, wio_main, i, (F32, BF16), 256, 512)
            (stl,) = _matmul(xs_b, wio_tail, i, (F32,), 256, tail_pad)
            sf4 = sf.reshape(bs, n_new, -1)
            k_new = sf4[:, :, q_w:q_w + kvw]
            v_new = sf4[:, :, q_w + kvw:q_w + 2 * kvw]
            ki_new = stl[:, :IDX_DIM].reshape(bs, n_new, IDX_DIM)
            wi_s = stl[:, IDX_DIM:IDX_DIM + IDX_HEADS]
            k_all = _pad_rows(jnp.concatenate([cache_c_k[i].reshape(bs, past, kvw), k_new], axis=1), l_s_pad)
            v_all = _pad_rows(jnp.concatenate([cache_c_v[i].reshape(bs, past, kvw), v_new], axis=1), l_s_pad)
            ki_all = _pad_rows(jnp.concatenate([cache_c_idx[i], ki_new], axis=1), l_s_pad)
            sel_s = _dsa_index(sb, wi_s, kz_of(ki_all), bs, n_new, l_s_pad, l_s, n_new, l_s_pad, past, qic)
            mix_s = _dsa_attn(sb, k_all.astype(BF16).reshape(bs * l_s_pad, kvw),
                              v_all.astype(BF16).reshape(bs * l_s_pad, kvw), sel_s,
                              bs, n_new, l_s_pad, n_kv, n_new, l_s_pad, past, 0, 0)
            outs["cks"].append(k_new.reshape(bs, n_new, n_kv, HEAD_DIM))
            outs["cvs"].append(v_new.reshape(bs, n_new, n_kv, HEAD_DIM))
            outs["cis"].append(ki_new)

        xp, xp_b = _matmul_residual_ln(mix_p, w_out, i, xp, ln_g[l, 0], ln_b[l, 0], 512, 512)
        xs, xs_b = _matmul_residual_ln(mix_s, w_out, i, xs, ln_g[l, 0], ln_b[l, 0], 256, 512)

        (mkv,) = _matmul(mem2d, wmkv, l, (F32,), 512, 512)
        mk_p, mv_p = mkv[:, :mem_w], mkv[:, mem_w:]
        outs["mkp"].append(mk_p.reshape(bp, n_mem, MEM_HEADS, HEAD_DIM))
        outs["mvp"].append(mv_p.reshape(bp, n_mem, MEM_HEADS, HEAD_DIM))
        xp, xp_b = _mem_attn_ln(xp, wmq, wmo, l, mk_p, mv_p, ln_g[l, 1], ln_b[l, 1], 256, 256, sp)
        xs, xs_b = _mem_attn_ln(xs, wmq, wmo, l, cmk[l], cmv[l], ln_g[l, 1], ln_b[l, 1],
                                min(ms, 128), n_new, n_new)

        cb_l = conv_b[l][None]
        act_p, st_p = _ffn_up(xp_b, wu, l, conv_w[l], cb_l, None, 1024, 1024, sp, ff_pad)
        act_s, st_s = _ffn_up(xs_b, wu, l, conv_w[l], cb_l, state_ffn_conv[l], ms, n_new, n_new, ff_pad)
        tiles_per_seq = sp // 1024
        outs["fp"].append(st_p[tiles_per_seq - 1::tiles_per_seq])
        outs["fs"].append(st_s)
        xp, xp_b = _matmul_residual_ln(act_p, wd, l, xp, ln_g[l, 2], ln_b[l, 2], 512, 512)
        xs, xs_b = _matmul_residual_ln(act_s, wd, l, xs, ln_g[l, 2], ln_b[l, 2], 256, 1024)

    st = jnp.stack
    o = outs
    return (xp.reshape(bp, sp, d), xs.reshape(bs, n_new, d),
            st(o["akp"]), st(o["avp"]), st(o["bkp"]), st(o["bvp"]), st(o["ckp"]), st(o["cvp"]), st(o["cip"]),
            st(o["mkp"]), st(o["mvp"]), st(o["fp"]),
            st(o["aks"]), st(o["avs"]), st(o["bks"]), st(o["bvs"]), st(o["cks"]), st(o["cvs"]), st(o["cis"]),
            st(o["fs"]))
```

```python
import functools

import numpy as np
import jax
import jax.numpy as jnp
from jax import lax
from jax.experimental import pallas as pl
from jax.experimental.pallas import tpu as pltpu

BF16 = jnp.bfloat16
F32 = jnp.float32

CHUNK = 64
HEAD_DIM = 128
A_LEFT_CHUNKS = 8
REL_CLIP = 128
C_GROUP = 4
IDX_HEADS = 32
IDX_DIM = 64
TOPK_MAX = 256
MEM_HEADS = 4
CONV_W = 3
DEPTH = 4
ALPHA = (2.0 * DEPTH) ** 0.25
LN_EPS = 1e-5
ATTN_SCALE = HEAD_DIM ** -0.5
IDX_SCALE = IDX_DIM ** -0.5
IDX_W_SCALE = IDX_HEADS ** -0.5
NEG_INF = -1e30

V7X_VMEM_BYTES = 64 * 1024 * 1024
LANES = 128
SUBLANES = 8
SB_EXIT = 104.0
INT_MIN = -(2 ** 31)
NEG_KEY = int(np.array(NEG_INF, np.float32).view(np.int32)) ^ 0x7FFFFFFF


def _cparams(vmem_bytes):
    limit = int(min(max(vmem_bytes * 1.25 + (4 << 20), 32 << 20), V7X_VMEM_BYTES - (6 << 20)))
    return pltpu.CompilerParams(vmem_limit_bytes=limit)


def _nt_dot(a, b):
    return lax.dot_general(a, b, (((1,), (1,)), ((), ())), preferred_element_type=F32)


def _w_spec(w, l, block, index):
    if l is None:
        return pl.BlockSpec(block, index)
    return pl.BlockSpec((None,) + block, lambda *grid: (l,) + tuple(index(*grid)))


def _mm_body(x_ref, w_ref, *out_refs, emit_w):
    if emit_w:
        out_refs[-1][...] = w_ref[...].astype(BF16)
        w_ref, out_refs = out_refs[-1], out_refs[:-1]
    r = jnp.dot(x_ref[...].astype(BF16), w_ref[...].astype(BF16), preferred_element_type=F32)
    for o in out_refs:
        o[...] = r.astype(o.dtype)


def _matmul(x, w, l, out_dtypes, tm, tn, col0=0, n_cols=None, emit_w=False):
    m, kd = x.shape
    n = n_cols or w.shape[-1]
    tm, tn = min(tm, m), min(tn, n)
    assert m % tm == 0 and n % tn == 0 and col0 % tn == 0, (x.shape, w.shape, tm, tn, col0)
    assert not emit_w or m == tm
    j0 = col0 // tn
    wsz = w.dtype.itemsize
    vmem = 2 * (tm * kd * x.dtype.itemsize + kd * tn * wsz) + kd * tn * 2 * (3 if emit_w else 1) + 2 * tm * tn * 4
    vmem += sum(2 * tm * tn * jnp.dtype(d).itemsize for d in out_dtypes)
    out_specs = [pl.BlockSpec((tm, tn), lambda i, j: (i, j)) for _ in out_dtypes]
    out_shape = [jax.ShapeDtypeStruct((m, n), d) for d in out_dtypes]
    if emit_w:
        out_specs.append(pl.BlockSpec((kd, tn), lambda i, j: (0, j)))
        out_shape.append(jax.ShapeDtypeStruct((kd, n), BF16))
    return pl.pallas_call(
        functools.partial(_mm_body, emit_w=emit_w),
        grid=(m // tm, n // tn),
        in_specs=[pl.BlockSpec((tm, kd), lambda i, j: (i, 0)),
                  _w_spec(w, l, (kd, tn), lambda i, j: (0, j0 + j))],
        out_specs=out_specs,
        out_shape=out_shape,
        compiler_params=_cparams(vmem),
        name="matmul",
    )(x, w)


LN_ROWS = 128


def _residual_ln_rows(y_ref, res_ref, g_ref, b_ref, o_ref, ob_ref, rows):
    g = g_ref[...]
    b = b_ref[...]

    def chunk(c, carry):
        sl = pl.ds(pl.multiple_of(c * LN_ROWS, LN_ROWS), LN_ROWS)
        y = ALPHA * res_ref[sl, :] + y_ref[sl, :]
        mu = jnp.mean(y, axis=-1, keepdims=True)
        d = y - mu
        var = jnp.mean(d * d, axis=-1, keepdims=True)
        out = d * lax.rsqrt(var + LN_EPS) * g + b
        o_ref[sl, :] = out
        ob_ref[sl, :] = out.astype(BF16)
        return carry

    lax.fori_loop(0, rows // LN_ROWS, chunk, 0)


MM_LN_CHUNK = 512


def _mm_ln_body(x_ref, w_ref, res_ref, g_ref, b_ref, o_ref, ob_ref, *rest, n_k, tm, emit_w, w_rows):
    k = pl.program_id(1)
    x = x_ref[...]
    n = o_ref.shape[1]
    chunks = [slice(c, c + MM_LN_CHUNK) for c in range(0, n, MM_LN_CHUNK)]
    if emit_w:
        tk = w_ref.shape[0]
        row = k * tk + lax.broadcasted_iota(jnp.int32, (tk, 1), 0)
        (wb_ref,) = rest
        wb_ref[...] = jnp.where(row < w_rows, w_ref[...], 0.0).astype(BF16)
        w_ref = wb_ref

    @pl.when(k == 0)
    def _():
        for sl in chunks:
            o_ref[:, sl] = jnp.dot(x, w_ref[:, sl], preferred_element_type=F32)

    @pl.when(k > 0)
    def _():
        for sl in chunks:
            o_ref[:, sl] += jnp.dot(x, w_ref[:, sl], preferred_element_type=F32)

    @pl.when(k == n_k - 1)
    def _():
        _residual_ln_rows(o_ref, res_ref, g_ref, b_ref, o_ref, ob_ref, tm)


def _matmul_residual_ln(x, w, l, res, g, b, tm, tk, emit_w=False):
    m, kd = x.shape
    w_rows, n = w.shape[-2:]
    tm, tk = min(tm, m), min(tk, kd)
    assert m % tm == 0 and kd % tk == 0 and tm % LN_ROWS == 0 and n % MM_LN_CHUNK == 0
    assert (emit_w and m == tm) or w_rows == kd
    n_k = kd // tk
    vmem = 2 * (tm * tk * 2 + tk * n * w.dtype.itemsize + tm * n * 4 + tm * n * 4 + tm * n * 2)
    out_specs = [pl.BlockSpec((tm, n), lambda i, k: (i, 0)), pl.BlockSpec((tm, n), lambda i, k: (i, 0))]
    out_shape = [jax.ShapeDtypeStruct((m, n), F32), jax.ShapeDtypeStruct((m, n), BF16)]
    if emit_w:
        vmem += 3 * tk * n * 2
        out_specs.append(pl.BlockSpec((tk, n), lambda i, k: (k, 0)))
        out_shape.append(jax.ShapeDtypeStruct((kd, n), BF16))
    return pl.pallas_call(
        functools.partial(_mm_ln_body, n_k=n_k, tm=tm, emit_w=emit_w, w_rows=w_rows),
        grid=(m // tm, n_k),
        in_specs=[pl.BlockSpec((tm, tk), lambda i, k: (i, k)),
                  _w_spec(w, l, (tk, n), lambda i, k: (k, 0)),
                  pl.BlockSpec((tm, n), lambda i, k: (i, 0)),
                  pl.BlockSpec((1, n), lambda i, k: (0, 0)),
                  pl.BlockSpec((1, n), lambda i, k: (0, 0))],
        out_specs=out_specs,
        out_shape=out_shape,
        compiler_params=_cparams(vmem),
        name="matmul_residual_ln",
    )(x, w, res, g.reshape(1, n), b.reshape(1, n))


BAND_TQ = 128
BAND_KB = 5
ATTN_HB = 4


def _band_prompt_body(q_ref, k_ref, v_ref, bias_ref, o_ref):
    t = pl.program_id(2)
    heads = [slice(h * HEAD_DIM, (h + 1) * HEAD_DIM) for h in range(ATTN_HB)]
    first = t - (BAND_KB - 1)
    keys = [pl.ds(pl.multiple_of(jnp.maximum(first + jb, 0) * BAND_TQ, BAND_TQ), BAND_TQ) for jb in range(BAND_KB)]
    s = jnp.concatenate(
        [jnp.concatenate([_nt_dot(q_ref[:, hs], k_ref[keys[jb], hs]) for hs in heads], axis=0)
         for jb in range(BAND_KB)], axis=1)
    s = s * ATTN_SCALE + bias_ref[...].reshape(ATTN_HB * BAND_TQ, BAND_KB * BAND_TQ)
    col_block = lax.broadcasted_iota(jnp.int32, (1, BAND_KB * BAND_TQ), 1) // BAND_TQ
    s = jnp.where(first + col_block >= 0, s, NEG_INF)
    m = jnp.max(s, axis=-1, keepdims=True)
    p = jnp.exp(s - m)
    l = jnp.sum(p, axis=-1, keepdims=True)
    p = p.astype(BF16)
    for h, hs in enumerate(heads):
        rows = slice(h * BAND_TQ, (h + 1) * BAND_TQ)
        acc = sum(jnp.dot(p[rows, jb * BAND_TQ:(jb + 1) * BAND_TQ], v_ref[keys[jb], hs],
                          preferred_element_type=F32) for jb in range(BAND_KB))
        o_ref[:, hs] = (acc / l[rows]).astype(o_ref.dtype)


def _toeplitz_bias(rel_bias, n_rows, n_cols, offset):
    m = np.arange(n_rows + n_cols - 1) - (n_rows - 1)
    diag = rel_bias[:, np.clip(offset - m, -REL_CLIP, REL_CLIP) + REL_CLIP]
    rows = [diag[:, n_rows - 1 - i:n_rows - 1 - i + n_cols] for i in range(n_rows)]
    return jnp.stack(rows, axis=1).astype(F32)


def _band_attn_prompt(qkv, rel_bias, batch, seq, n_heads, q_col, k_col, v_col):
    assert n_heads % ATTN_HB == 0 and q_col % ATTN_HB == 0 and k_col % ATTN_HB == 0 and v_col % ATTN_HB == 0
    nt = seq // BAND_TQ
    hw = ATTN_HB * HEAD_DIM
    bias = _toeplitz_bias(rel_bias, BAND_TQ, BAND_KB * BAND_TQ, A_LEFT_CHUNKS * CHUNK)
    qi = np.arange(BAND_TQ)[:, None] // CHUNK
    kj = np.arange(BAND_KB * BAND_TQ)[None, :] // CHUNK
    in_band = (kj >= qi) & (kj <= qi + A_LEFT_CHUNKS)
    bias = jnp.where(jnp.asarray(in_band)[None], bias, NEG_INF)
    vmem = 2 * (2 * seq * hw * 2 + ATTN_HB * BAND_TQ * BAND_KB * BAND_TQ * 4) + (8 << 20)
    return pl.pallas_call(
        _band_prompt_body,
        grid=(batch, n_heads // ATTN_HB, nt),
        in_specs=[pl.BlockSpec((BAND_TQ, hw), lambda b, h, t: (b * nt + t, q_col // ATTN_HB + h)),
                  pl.BlockSpec((seq, hw), lambda b, h, t: (b, k_col // ATTN_HB + h)),
                  pl.BlockSpec((seq, hw), lambda b, h, t: (b, v_col // ATTN_HB + h)),
                  pl.BlockSpec((ATTN_HB, BAND_TQ, BAND_KB * BAND_TQ), lambda b, h, t: (h, 0, 0))],
        out_specs=pl.BlockSpec((BAND_TQ, hw), lambda b, h, t: (b * nt + t, h)),
        out_shape=jax.ShapeDtypeStruct((batch * seq, n_heads * HEAD_DIM), BF16),
        compiler_params=_cparams(vmem),
        name="band_attn_prompt",
    )(qkv, qkv, qkv, bias)


def _bias_attn_body(q_ref, k_ref, v_ref, bias_ref, o_ref, *, n_heads):
    for h in range(n_heads):
        hs = slice(h * HEAD_DIM, (h + 1) * HEAD_DIM)
        s = _nt_dot(q_ref[:, hs], k_ref[:, hs]) * ATTN_SCALE + bias_ref[h]
        m = jnp.max(s, axis=-1, keepdims=True)
        p = jnp.exp(s - m)
        l = jnp.sum(p, axis=-1, keepdims=True)
        acc = jnp.dot(p.astype(BF16), v_ref[:, hs], preferred_element_type=F32)
        o_ref[:, hs] = (acc / l).astype(o_ref.dtype)


def _band_attn_sample(q2d, k_all, v_all, rel_bias, batch, n_new, n_heads):
    L = k_all.shape[0] // batch
    hw = n_heads * HEAD_DIM
    bias = _toeplitz_bias(rel_bias, n_new, L, L - n_new)
    return pl.pallas_call(
        functools.partial(_bias_attn_body, n_heads=n_heads),
        grid=(batch,),
        in_specs=[pl.BlockSpec((n_new, hw), lambda b: (b, 0)),
                  pl.BlockSpec((L, hw), lambda b: (b, 0)),
                  pl.BlockSpec((L, hw), lambda b: (b, 0)),
                  pl.BlockSpec((n_heads, n_new, L), lambda b: (0, 0, 0))],
        out_specs=pl.BlockSpec((n_new, hw), lambda b: (b, 0)),
        out_shape=jax.ShapeDtypeStruct((batch * n_new, hw), BF16),
        name="band_attn_sample",
    )(q2d, k_all, v_all, bias)


SB_TK = 256


def _split2(x):
    hi = x.astype(BF16)
    return hi, (x - hi.astype(F32)).astype(BF16)


def _stick_break_body(q_ref, k_ref, v_ref, u_ref, o_ref, *, tq, qpos_base):
    qt = pl.program_id(2)
    qpos0 = qpos_base + qt * tq
    u = u_ref[...]
    rows = ATTN_HB * tq
    rowpos = qpos0 + lax.broadcasted_iota(jnp.int32, (rows, 1), 0) % tq
    top = (qpos0 + tq + LANES - 1) // LANES * LANES
    heads = [slice(h * HEAD_DIM, (h + 1) * HEAD_DIM) for h in range(ATTN_HB)]

    def cond(c):
        limit, floor = c[0], c[1]
        return jnp.logical_and(limit > 0, floor > -SB_EXIT)

    def body(c):
        limit, _, carry, acc = c
        start = pl.multiple_of(jnp.maximum(limit - SB_TK, 0), LANES)
        keys = pl.ds(start, SB_TK)
        kpos = start + lax.broadcasted_iota(jnp.int32, (1, SB_TK), 1)
        mask = jnp.logical_and(kpos < rowpos, kpos < limit)
        z = jnp.concatenate([_nt_dot(q_ref[:, hs], k_ref[keys, hs]) for hs in heads], axis=0) * ATTN_SCALE
        sp = jnp.maximum(z, 0.0) + jnp.log(1.0 + jnp.exp(-jnp.abs(z)))
        log_1m = jnp.where(mask, -sp, 0.0)
        suffix = sum(jnp.dot(piece, u, preferred_element_type=F32) for piece in _split2(log_1m))
        w = jnp.where(mask, jnp.exp(z - sp + suffix + carry), 0.0).astype(BF16)
        acc = acc + jnp.concatenate(
            [jnp.dot(w[h * tq:(h + 1) * tq], v_ref[keys, hs], preferred_element_type=F32)
             for h, hs in enumerate(heads)], axis=0)
        carry = carry + jnp.sum(log_1m, axis=-1, keepdims=True)
        return limit - SB_TK, jnp.max(carry), carry, acc

    init = (top, jnp.float32(0.0), jnp.zeros((rows, 1), F32), jnp.zeros((rows, HEAD_DIM), F32))
    acc = lax.while_loop(cond, body, init)[3]
    for h, hs in enumerate(heads):
        o_ref[:, hs] = acc[h * tq:(h + 1) * tq].astype(o_ref.dtype)


def _stick_break_attn(q2d, k2d, v2d, batch, n_q, n_k, n_heads, tq, qpos_base, q_col, k_col, v_col):
    assert n_q % tq == 0 and n_k % LANES == 0 and n_k >= SB_TK and n_heads % ATTN_HB == 0
    assert q_col % ATTN_HB == 0 and k_col % ATTN_HB == 0 and v_col % ATTN_HB == 0
    assert -(-(qpos_base + n_q) // LANES) * LANES <= n_k
    nt = n_q // tq
    hw = ATTN_HB * HEAD_DIM
    j = np.arange(SB_TK)
    u = jnp.asarray(j[:, None] > j[None, :], BF16)
    vmem = 2 * (2 * n_k * hw * 2) + (8 << 20)
    return pl.pallas_call(
        functools.partial(_stick_break_body, tq=tq, qpos_base=qpos_base),
        grid=(batch, n_heads // ATTN_HB, nt),
        in_specs=[pl.BlockSpec((tq, hw), lambda b, h, t: (b * nt + t, q_col // ATTN_HB + h)),
                  pl.BlockSpec((n_k, hw), lambda b, h, t: (b, k_col // ATTN_HB + h)),
                  pl.BlockSpec((n_k, hw), lambda b, h, t: (b, v_col // ATTN_HB + h)),
                  pl.BlockSpec((SB_TK, SB_TK), lambda b, h, t: (0, 0))],
        out_specs=pl.BlockSpec((tq, hw), lambda b, h, t: (b * nt + t, h)),
        out_shape=jax.ShapeDtypeStruct((batch * n_q, n_heads * HEAD_DIM), BF16),
        compiler_params=_cparams(vmem),
        name="stick_break_attn",
    )(q2d, k2d, v2d, u)


def _sortable(x):
    bits = pltpu.bitcast(x, jnp.int32)
    return jnp.where(bits < 0, bits ^ jnp.int32(0x7FFFFFFF), bits)


def _dsa_index_body(qi_ref, wi_ref, kz_ref, bias_ref, key_ref, *, tq, tk, n_real, qpos_base, topk):
    n_pad = key_ref.shape[1]
    qt = pl.program_id(1)
    qpos0 = qpos_base + qt * tq
    rowpos = qpos0 + lax.broadcasted_iota(jnp.int32, (tq, 1), 0)
    qchunk = rowpos // CHUNK
    n_kb = jnp.minimum((((qpos0 + tq - 1) // CHUNK + 1) * CHUNK + tk - 1) // tk, n_pad // tk)
    n_tail = jnp.maximum(n_real - n_kb * tk, 0)
    tail0 = n_kb * tk

    w = wi_ref[...] * (IDX_SCALE * IDX_W_SCALE)

    def block_pos(kb):
        start = pl.multiple_of(kb * tk, tk)
        return start, start + lax.broadcasted_iota(jnp.int32, (1, tk), 1)

    def score_block(kb, carry):
        start, kpos = block_pos(kb)
        kz = kz_ref[pl.ds(start, tk), :]
        acc = jnp.zeros((tq, tk), F32)
        for pair in range(IDX_HEADS // 2):
            qp = qi_ref[:, pair * LANES:(pair + 1) * LANES]
            for half in range(2):
                h = 2 * pair + half
                sc = _nt_dot(qp, kz[:, half * LANES:(half + 1) * LANES])
                acc = acc + jnp.maximum(sc, 0.0) * w[:, h:h + 1]
        key = _sortable(jnp.where(kpos // CHUNK <= qchunk, acc, NEG_INF))
        key_ref[:, pl.ds(start, tk)] = jnp.where(kpos < n_real, key, jnp.int32(INT_MIN))
        return carry

    lax.fori_loop(0, n_kb, score_block, 0)

    def count(pred):
        def blk(kb, acc):
            start, kpos = block_pos(kb)
            hit = pred(key_ref[:, pl.ds(start, tk)], kpos).astype(F32)
            return acc + sum(hit[:, c:c + LANES] for c in range(0, tk, LANES))
        lanes = lax.fori_loop(0, n_kb, blk, jnp.zeros((tq, LANES), F32))
        return jnp.sum(lanes, axis=-1, keepdims=True)

    kf = jnp.float32(topk)
    tail_f = n_tail.astype(F32)
    neg_key = jnp.int32(NEG_KEY)

    def thr_step(i, ans_u):
        cand_u = ans_u | lax.shift_left(jnp.int32(1), 31 - i)
        cand = cand_u ^ jnp.int32(INT_MIN)
        cnt = count(lambda key, kpos: key >= cand) + jnp.where(neg_key >= cand, tail_f, 0.0)
        return jnp.where(cnt >= kf, cand_u, ans_u)

    thr = lax.fori_loop(0, 32, thr_step, jnp.zeros((tq, 1), jnp.int32)) ^ jnp.int32(INT_MIN)
    tail_gt = jnp.where(neg_key > thr, tail_f, 0.0)
    n_gt = count(lambda key, kpos: key > thr) + tail_gt
    n_eq = count(lambda key, kpos: key == thr) + jnp.where(neg_key == thr, tail_f, 0.0)
    need = kf - n_gt

    n_bits = int(np.ceil(np.log2(n_pad)))

    def tie_search():
        def pos_step(i, lo):
            cand = lo + lax.shift_left(jnp.int32(1), n_bits - 1 - i)
            in_tail = jnp.clip(cand - tail0, 0, n_tail).astype(F32)
            cnt = (count(lambda key, kpos: jnp.logical_and(key == thr, kpos < cand))
                   + jnp.where(neg_key == thr, in_tail, 0.0))
            return jnp.where(cnt < need, cand, lo)
        return lax.fori_loop(0, n_bits, pos_step, jnp.zeros((tq, 1), jnp.int32))

    has_ties = jnp.max(jnp.where(n_eq > need, 1.0, 0.0)) > 0.0
    last = lax.cond(has_ties, tie_search, lambda: jnp.full((tq, 1), n_pad, jnp.int32))

    bias_ref[0] = jnp.full(bias_ref.shape[1:], NEG_INF, bias_ref.dtype)

    def write_block(kb, carry):
        start, kpos = block_pos(kb)
        key = key_ref[:, pl.ds(start, tk)]
        sel = jnp.logical_or(key > thr, jnp.logical_and(key == thr, kpos <= last))
        ok = jnp.logical_and(jnp.logical_and(sel, kpos // CHUNK <= qchunk), kpos < n_real)
        bias_ref[0, :, pl.ds(start, tk)] = jnp.where(ok, 0.0, NEG_INF).astype(bias_ref.dtype)
        return carry

    lax.fori_loop(0, n_kb, write_block, 0)


def _dsa_index(qi2d, wi, kz, batch, n_q, n_k, n_real, tq, tk, qpos_base, qi_col):
    nt = n_q // tq
    topk = min(TOPK_MAX, n_real // 4)
    vmem = 2 * (tq * 2048 * 2 + n_k * 256 * 2 + tq * n_k * 2) + tq * n_k * 4 + 8 * tq * tk * 4 + (4 << 20)
    return pl.pallas_call(
        functools.partial(_dsa_index_body, tq=tq, tk=tk, n_real=n_real, qpos_base=qpos_base, topk=topk),
        grid=(batch, nt),
        in_specs=[pl.BlockSpec((tq, IDX_HEADS * IDX_DIM), lambda b, t: (b * nt + t, qi_col)),
                  pl.BlockSpec((tq, IDX_HEADS), lambda b, t: (b * nt + t, 0)),
                  pl.BlockSpec((n_k, 2 * LANES), lambda b, t: (b, 0))],
        out_specs=pl.BlockSpec((1, tq, n_k), lambda b, t: (b, t, 0)),
        out_shape=jax.ShapeDtypeStruct((batch, n_q, n_k), BF16),
        scratch_shapes=[pltpu.VMEM((tq, n_k), jnp.int32)],
        compiler_params=_cparams(vmem),
        name="dsa_index",
    )(qi2d, wi, kz)


def _dsa_attn_body(q_ref, k_ref, v_ref, bias_ref, slope_ref, o_ref, s_ref, p_ref, *, tq, tk, qpos_base):
    n_pad = k_ref.shape[0]
    qt = pl.program_id(2)
    qpos0 = qpos_base + qt * tq
    tile_end = qpos0 + tq
    rows = C_GROUP * tq
    rowpos = qpos0 + lax.broadcasted_iota(jnp.int32, (tq, 1), 0)
    n_kb = jnp.minimum((((qpos0 + tq - 1) // CHUNK + 1) * CHUNK + tk - 1) // tk, n_pad // tk)
    qs = jnp.concatenate([q_ref[:, g * HEAD_DIM:(g + 1) * HEAD_DIM] for g in range(C_GROUP)], axis=0)

    def key_rows(kb):
        return pl.ds(pl.multiple_of(kb * tk, tk), tk)

    def scores(kb):
        s_ref[kb % 2] = _nt_dot(qs, k_ref[key_rows(kb), :])

    def weighted_values(kb):
        return jnp.dot(p_ref[(kb + 2) % 2], v_ref[key_rows(jnp.maximum(kb, 0)), :], preferred_element_type=F32)

    slope = slope_ref[0]

    def softmax(kb, m, l, has_later_keys):
        kpos = kb * tk + lax.broadcasted_iota(jnp.int32, (1, tk), 1)
        back = (tile_end - kpos).astype(F32)
        s = s_ref[kb % 2] * ATTN_SCALE - slope * back
        extra = bias_ref[0, :, key_rows(kb)].astype(F32)
        s = (s.reshape(C_GROUP, tq, tk) + extra[None]).reshape(rows, tk)
        if has_later_keys:
            ahead = jnp.where(kpos > rowpos, (rowpos - kpos).astype(F32), 0.0)
            s = s + (2.0 * slope) * jnp.concatenate([ahead] * C_GROUP, axis=0)
        m_new = jnp.maximum(m, jnp.max(s, axis=-1, keepdims=True))
        p = jnp.exp(s - m_new)
        p_ref[kb % 2] = p.astype(BF16)
        a = jnp.exp(m - m_new)
        return m_new, a * l + jnp.sum(p, axis=-1, keepdims=True), a

    def trip(kb, state):
        m, l, a_prev, acc = state
        scores(kb + 1)
        acc = a_prev * acc + weighted_values(kb - 1)
        m, l, a = softmax(kb, m, l, False)
        return m, l, a, acc

    p_ref[1] = jnp.zeros(p_ref.shape[1:], BF16)
    scores(0)
    init = (jnp.full((rows, 1), 0.5 * NEG_INF, F32), jnp.zeros((rows, 1), F32),
            jnp.ones((rows, 1), F32), jnp.zeros((rows, HEAD_DIM), F32))
    last = n_kb - 1
    m, l, a_prev, acc = lax.fori_loop(0, last, trip, init)
    acc = a_prev * acc + weighted_values(last - 1)
    m, l, a = softmax(last, m, l, True)
    acc = a * acc + weighted_values(last)
    o = (acc / l).astype(o_ref.dtype)
    for g in range(C_GROUP):
        o_ref[:, g * HEAD_DIM:(g + 1) * HEAD_DIM] = o[g * tq:(g + 1) * tq]


def _dsa_attn(q2d, k2d, v2d, bias, batch, n_q, n_k, n_kv, tq, tk, qpos_base, k_col, v_col):
    assert tk % tq == 0 and qpos_base % tq == 0 and n_k % tk == 0
    nt = n_q // tq
    gw = C_GROUP * HEAD_DIM
    n_heads = n_kv * C_GROUP
    slopes = 2.0 ** (-8.0 * np.arange(1, n_heads + 1) / n_heads)
    slope_rows = np.repeat(slopes.reshape(n_kv, C_GROUP), tq, axis=1)[..., None].astype(np.float32)
    rows = C_GROUP * tq
    vmem = (2 * (2 * n_k * HEAD_DIM * 2 + tq * n_k * 2 + 2 * tq * gw * 2 + rows * LANES * 4)
            + 2 * rows * tk * 6 + 8 * rows * tk * 4 + (4 << 20))
    return pl.pallas_call(
        functools.partial(_dsa_attn_body, tq=tq, tk=tk, qpos_base=qpos_base),
        grid=(batch, n_kv, nt),
        in_specs=[pl.BlockSpec((tq, gw), lambda b, n, t: (b * nt + t, n)),
                  pl.BlockSpec((n_k, HEAD_DIM), lambda b, n, t: (b, k_col + n)),
                  pl.BlockSpec((n_k, HEAD_DIM), lambda b, n, t: (b, v_col + n)),
                  pl.BlockSpec((1, tq, n_k), lambda b, n, t: (b, t, 0)),
                  pl.BlockSpec((1, rows, 1), lambda b, n, t: (n, 0, 0))],
        out_specs=pl.BlockSpec((tq, gw), lambda b, n, t: (b * nt + t, n)),
        out_shape=jax.ShapeDtypeStruct((batch * n_q, n_heads * HEAD_DIM), BF16),
        scratch_shapes=[pltpu.VMEM((2, rows, tk), F32), pltpu.VMEM((2, rows, tk), BF16)],
        compiler_params=_cparams(vmem),
        name="dsa_attn",
    )(q2d, k2d, v2d, bias, jnp.asarray(slope_rows))


def _mem_attn_body(x_ref, wq_ref, mk_ref, mv_ref, wo_ref, g_ref, b_ref, o_ref, ob_ref, y_ref, *,
                   tm, rows_per_seq, n_mem):
    q = jnp.dot(x_ref[...].astype(BF16), wq_ref[...], preferred_element_type=F32).astype(BF16)
    n_seq = tm // rows_per_seq
    heads = [slice(h * HEAD_DIM, (h + 1) * HEAD_DIM) for h in range(MEM_HEADS)]
    s = jnp.concatenate(
        [_nt_dot(q[s_i * rows_per_seq:(s_i + 1) * rows_per_seq, hs],
                 mk_ref[s_i * n_mem:(s_i + 1) * n_mem, hs].astype(BF16))
         for hs in heads for s_i in range(n_seq)], axis=0) * ATTN_SCALE
    m = jnp.max(s, axis=-1, keepdims=True)
    p = jnp.exp(s - m)
    l = jnp.sum(p, axis=-1, keepdims=True)
    p = p.astype(BF16)
    o_heads = []
    for h, hs in enumerate(heads):
        pieces = []
        for s_i in range(n_seq):
            rows = slice(h * tm + s_i * rows_per_seq, h * tm + (s_i + 1) * rows_per_seq)
            o = jnp.dot(p[rows], mv_ref[s_i * n_mem:(s_i + 1) * n_mem, hs].astype(BF16),
                        preferred_element_type=F32)
            pieces.append((o / l[rows]).astype(BF16))
        o_heads.append(pieces[0] if n_seq == 1 else jnp.concatenate(pieces, axis=0))
    y_ref[...] = jnp.dot(jnp.concatenate(o_heads, axis=-1), wo_ref[...], preferred_element_type=F32)
    _residual_ln_rows(y_ref, x_ref, g_ref, b_ref, o_ref, ob_ref, tm)


def _mem_attn_ln(x, wq, wo, l, mk, mv, g, b, tm, rows_per_seq, seq_rows):
    m, d = x.shape
    mw = wq.shape[-1]
    n_seq_tile = tm // rows_per_seq
    n_mem = mk.shape[0] // (m // seq_rows)
    if n_seq_tile == 1:
        mem_map = lambda i: ((i * tm) // seq_rows, 0)
    else:
        assert rows_per_seq == seq_rows
        mem_map = lambda i: (i, 0)
    vmem = 2 * (tm * d * 4 * 2 + tm * d * 2 + 2 * d * mw * 2 + 2 * n_seq_tile * n_mem * mw * 4) + tm * d * 4
    out_specs = [pl.BlockSpec((tm, d), lambda i: (i, 0)), pl.BlockSpec((tm, d), lambda i: (i, 0))]
    out_shape = [jax.ShapeDtypeStruct((m, d), F32), jax.ShapeDtypeStruct((m, d), BF16)]
    return pl.pallas_call(
        functools.partial(_mem_attn_body, tm=tm, rows_per_seq=rows_per_seq, n_mem=n_mem),
        grid=(m // tm,),
        in_specs=[pl.BlockSpec((tm, d), lambda i: (i, 0)),
                  _w_spec(wq, l, (d, mw), lambda i: (0, 0)),
                  pl.BlockSpec((n_seq_tile * n_mem, mw), mem_map),
                  pl.BlockSpec((n_seq_tile * n_mem, mw), mem_map),
                  _w_spec(wo, l, (mw, d), lambda i: (0, 0)),
                  pl.BlockSpec((1, d), lambda i: (0, 0)),
                  pl.BlockSpec((1, d), lambda i: (0, 0))],
        out_specs=out_specs,
        out_shape=out_shape,
        scratch_shapes=[pltpu.VMEM((tm, d), F32)],
        compiler_params=_cparams(vmem),
        name="mem_attn_ln",
    )(x, wq, mk, mv, wo, g.reshape(1, d), b.reshape(1, d))


FFN_TN = 256
FFN_SUB = 256
FFN_TM = 2048
HALO = 8


def _gelu(x):
    return 0.5 * x * (1.0 + lax.erf(x * np.float32(2.0 ** -0.5)))


def _ffn_up_body(x_ref, wa_ref, wg_ref, cwa_ref, cwg_ref, cba_ref, cbg_ref, *rest,
                 tm, rows_per_seq, tiles_per_seq, carried, nj, emit_w):
    if carried:
        act_ref, sta_ref, stg_ref, carry_ref, ext_ref = rest
    elif emit_w:
        pa_ref, pg_ref, act_ref, sta_ref, stg_ref, wab_ref, wgb_ref, ext_ref = rest
    else:
        pa_ref, pg_ref, act_ref, sta_ref, stg_ref, ext_ref = rest
    i = pl.program_id(0)
    j = pl.program_id(1)

    def conv_gate(base, rows):
        cw = jnp.concatenate([cwa_ref[...], cwg_ref[...]], axis=1)
        cb = jnp.concatenate([cba_ref[...], cbg_ref[...]], axis=1)
        h = (ext_ref[base:base + rows, :] * cw[2:3, :]
             + ext_ref[base - 1:base - 1 + rows, :] * cw[1:2, :]
             + ext_ref[base - 2:base - 2 + rows, :] * cw[0:1, :]
             + cb)
        return (h[:, :FFN_TN] * _gelu(h[:, FFN_TN:])).astype(act_ref.dtype)

    def put_state(s_i, last2):
        sta_ref[s_i] = last2[:, :FFN_TN]
        stg_ref[s_i] = last2[:, FFN_TN:]

    @pl.when(j < nj)
    def _():
        if carried:
            @pl.when(i % tiles_per_seq == 0)
            def _():
                carry_ref[j] = jnp.zeros((2, 2 * FFN_TN), F32)

            ext_ref[HALO - 2:HALO, :] = carry_ref[j]
            sub = min(FFN_SUB, tm)
            for r in range(0, tm, sub):
                ext_ref[HALO + r:HALO + r + sub, :FFN_TN] = jnp.dot(x_ref[r:r + sub, :], wa_ref[...],
                                                                    preferred_element_type=F32)
                ext_ref[HALO + r:HALO + r + sub, FFN_TN:] = jnp.dot(x_ref[r:r + sub, :], wg_ref[...],
                                                                    preferred_element_type=F32)
                act_ref[r:r + sub, :] = conv_gate(HALO + r, sub)
            last2 = ext_ref[HALO + tm - 2:HALO + tm, :]
            carry_ref[j] = last2
            put_state(0, last2)
        else:
            wa, wg = wa_ref, wg_ref
            if emit_w:
                wab_ref[...] = wa_ref[...].astype(BF16)
                wgb_ref[...] = wg_ref[...].astype(BF16)
                wa, wg = wab_ref, wgb_ref
            up = jnp.concatenate([jnp.dot(x_ref[...], wa[...], preferred_element_type=F32),
                                  jnp.dot(x_ref[...], wg[...], preferred_element_type=F32)], axis=1)
            for s_i in range(tm // rows_per_seq):
                base = s_i * (rows_per_seq + HALO) + HALO
                rsl = slice(s_i * rows_per_seq, (s_i + 1) * rows_per_seq)
                ext_ref[base - 2:base, :] = jnp.concatenate([pa_ref[s_i], pg_ref[s_i]], axis=1)
                ext_ref[base:base + rows_per_seq, :] = up[rsl]
                put_state(s_i, ext_ref[base + rows_per_seq - 2:base + rows_per_seq, :])
                act_ref[rsl, :] = conv_gate(base, rows_per_seq)

    @pl.when(j >= nj)
    def _():
        act_ref[...] = jnp.zeros_like(act_ref)


def _ffn_up(x, w_val, w_gate, l, conv_w, conv_b, prev, tm, rows_per_seq, seq_rows, n_act_cols, emit_w=False):
    m, d = x.shape
    f2 = conv_w.shape[1]
    wt = 2 * FFN_TN
    assert f2 % wt == 0 and n_act_cols % FFN_TN == 0
    nj = f2 // wt
    nj_pad = n_act_cols // FFN_TN
    carried = prev is None
    assert not (emit_w and carried)
    n_seq = tm // rows_per_seq
    tiles_per_seq = max(seq_rows // tm, 1)
    n_tiles = m // tm
    val = lambda j: jnp.minimum(j, nj - 1)
    gate0 = 0 if l is None else nj
    in_specs = [pl.BlockSpec((tm, d), lambda i, j: (i, 0)),
                _w_spec(w_val, l, (d, FFN_TN), lambda i, j: (0, val(j))),
                _w_spec(w_gate, l, (d, FFN_TN), lambda i, j: (0, val(j) + gate0)),
                pl.BlockSpec((CONV_W, FFN_TN), lambda i, j: (0, val(j))),
                pl.BlockSpec((CONV_W, FFN_TN), lambda i, j: (0, val(j) + nj)),
                pl.BlockSpec((1, FFN_TN), lambda i, j: (0, val(j))),
                pl.BlockSpec((1, FFN_TN), lambda i, j: (0, val(j) + nj))]
    args = [x, w_val, w_gate, conv_w, conv_w, conv_b, conv_b]
    scratch = []
    if carried:
        assert n_seq == 1
        scratch.append(pltpu.VMEM((nj, 2, wt), F32))
    else:
        in_specs += [pl.BlockSpec((n_seq, 2, FFN_TN), lambda i, j: (i, 0, val(j))),
                     pl.BlockSpec((n_seq, 2, FFN_TN), lambda i, j: (i, 0, val(j) + nj))]
        args += [prev, prev]
    scratch.append(pltpu.VMEM((n_seq * (rows_per_seq + HALO), wt), F32))
    st_shape = jax.ShapeDtypeStruct((n_tiles * n_seq, 2, f2 // 2), F32)
    st_spec = pl.BlockSpec((n_seq, 2, FFN_TN), lambda i, j: (i, 0, val(j)))
    vmem = (2 * (tm * d * 2 + d * wt * w_val.dtype.itemsize + tm * FFN_TN * 2) + tm * wt * 4
            + 8 * min(tm, FFN_SUB) * wt * 4 + (4 << 20))
    out_specs = [pl.BlockSpec((tm, FFN_TN), lambda i, j: (i, j)), st_spec, st_spec]
    out_shape = [jax.ShapeDtypeStruct((m, n_act_cols), BF16), st_shape, st_shape]
    if emit_w:
        vmem += 2 * d * wt * 2
        out_specs += [pl.BlockSpec((d, FFN_TN), lambda i, j: (0, val(j)))] * 2
        out_shape += [jax.ShapeDtypeStruct((d, f2 // 2), BF16)] * 2
    act, st_a, st_g, *w_images = pl.pallas_call(
        functools.partial(_ffn_up_body, tm=tm, rows_per_seq=rows_per_seq, tiles_per_seq=tiles_per_seq,
                          carried=carried, nj=nj, emit_w=emit_w),
        grid=(n_tiles, nj_pad),
        in_specs=in_specs,
        out_specs=out_specs,
        out_shape=out_shape,
        scratch_shapes=scratch,
        compiler_params=_cparams(vmem),
        name="ffn_up_conv_gate",
    )(*args)
    return (act, jnp.concatenate([st_a, st_g], axis=-1), *w_images)


def _pad_rows(a, n_rows):
    return jnp.pad(a, ((0, 0), (0, n_rows - a.shape[1]), (0, 0)))


def kernel(x_prompt, x_sample, cache_a_k, cache_a_v, cache_b_k, cache_b_v, cache_c_k, cache_c_v, cache_c_idx, cache_mem_k, cache_mem_v, state_ffn_conv, mem_prompt, w_in_even, w_out_even, rel_bias_a, w_in_odd, w_out_odd, w_mem_q, w_mem_kv, w_mem_o, w_up, conv_w, conv_b, w_down, ln_g, ln_b):
    bp, sp, d = x_prompt.shape
    bs, n_new, _ = x_sample.shape
    depth = w_up.shape[0]
    h_a = cache_a_k.shape[3]
    h_b = cache_b_k.shape[3]
    n_kv = cache_c_k.shape[3]
    h_c = n_kv * C_GROUP
    hw_a, hw_b, kvw = h_a * HEAD_DIM, h_b * HEAD_DIM, n_kv * HEAD_DIM
    qiw = IDX_HEADS * IDX_DIM
    a_win_s = cache_a_k.shape[2]
    a_win_p = min(A_LEFT_CHUNKS * CHUNK, sp)
    past = cache_b_k.shape[2]
    n_mem = mem_prompt.shape[1]
    mem_w = w_mem_q.shape[2]
    d_ff = w_down.shape[1]
    mp, ms = bp * sp, bs * n_new
    l_s = past + n_new
    l_s_pad = -(-l_s // LANES) * LANES
    main_odd = h_c * HEAD_DIM + 2 * kvw + qiw
    tail_pad = LANES
    ff_pad = -(-d_ff // 1024) * 1024

    wmq = w_mem_q.astype(BF16)
    wmo = w_mem_o.astype(BF16)

    xp = x_prompt.reshape(mp, d)
    xs = x_sample.reshape(ms, d)
    xp_b, xs_b = xp.astype(BF16), xs.astype(BF16)
    mem2d = mem_prompt.reshape(bp * n_mem, d).astype(BF16)
    cmk = cache_mem_k.reshape(depth, bs * n_mem, mem_w).astype(BF16)
    cmv = cache_mem_v.reshape(depth, bs * n_mem, mem_w).astype(BF16)

    outs = {k: [] for k in ("akp", "avp", "bkp", "bvp", "ckp", "cvp", "cip", "mkp", "mvp", "fp",
                            "aks", "avs", "bks", "bvs", "cks", "cvs", "cis", "fs")}

    for l in range(depth):
        i = l // 2
        if l % 2 == 0:
            w_out = w_out_even
            sf, sb, w_in_l = _matmul(xs_b, w_in_even, i, (F32, BF16), 256, 512, emit_w=True)
            pf, pb = _matmul(xp_b, w_in_l, None, (F32, BF16), 1024, 512)
            oa = _band_attn_prompt(pb, rel_bias_a[i], bp, sp, h_a, 0, h_a, 2 * h_a)
            cb0 = 3 * h_a
            ob = _stick_break_attn(pb, pb, pb, bp, sp, sp, h_b, 128, 0, cb0, cb0 + h_b, cb0 + 2 * h_b)
            mix_p = jnp.concatenate([oa, ob], axis=-1)
            pf4 = pf.reshape(bp, sp, -1)
            outs["akp"].append(pf4[:, sp - a_win_p:, hw_a:2 * hw_a].reshape(bp, a_win_p, h_a, HEAD_DIM))
            outs["avp"].append(pf4[:, sp - a_win_p:, 2 * hw_a:3 * hw_a].reshape(bp, a_win_p, h_a, HEAD_DIM))
            o0 = 3 * hw_a
            outs["bkp"].append(pf4[:, :, o0 + hw_b:o0 + 2 * hw_b].reshape(bp, sp, h_b, HEAD_DIM))
            outs["bvp"].append(pf4[:, :, o0 + 2 * hw_b:o0 + 3 * hw_b].reshape(bp, sp, h_b, HEAD_DIM))
            sf4 = sf.reshape(bs, n_new, -1)
            ka_all = jnp.concatenate([cache_a_k[i].reshape(bs, a_win_s, hw_a), sf4[:, :, hw_a:2 * hw_a]], axis=1)
            va_all = jnp.concatenate([cache_a_v[i].reshape(bs, a_win_s, hw_a), sf4[:, :, 2 * hw_a:3 * hw_a]], axis=1)
            la = a_win_s + n_new
            oa_s = _band_attn_sample(sb, ka_all.astype(BF16).reshape(bs * la, hw_a),
                                     va_all.astype(BF16).reshape(bs * la, hw_a), rel_bias_a[i], bs, n_new, h_a)
            kb_new = sf4[:, :, o0 + hw_b:o0 + 2 * hw_b]
            vb_new = sf4[:, :, o0 + 2 * hw_b:o0 + 3 * hw_b]
            kb_all = _pad_rows(jnp.concatenate([cache_b_k[i].reshape(bs, past, hw_b), kb_new], axis=1), l_s_pad)
            vb_all = _pad_rows(jnp.concatenate([cache_b_v[i].reshape(bs, past, hw_b), vb_new], axis=1), l_s_pad)
            ob_s = _stick_break_attn(sb, kb_all.astype(BF16).reshape(bs * l_s_pad, hw_b),
                                     vb_all.astype(BF16).reshape(bs * l_s_pad, hw_b),
                                     bs, n_new, l_s_pad, h_b, n_new, past, cb0, 0, 0)
            mix_s = jnp.concatenate([oa_s, ob_s], axis=-1)
            outs["aks"].append(ka_all[:, la - a_win_s:].reshape(bs, a_win_s, h_a, HEAD_DIM))
            outs["avs"].append(va_all[:, la - a_win_s:].reshape(bs, a_win_s, h_a, HEAD_DIM))
            outs["bks"].append(kb_new.reshape(bs, n_new, h_b, HEAD_DIM))
            outs["bvs"].append(vb_new.reshape(bs, n_new, h_b, HEAD_DIM))
        else:
            w_out = w_out_odd
            sf, sb, w_main_l = _matmul(xs_b, w_in_odd, i, (F32, BF16), 256, 512, n_cols=main_odd, emit_w=True)
            stl, w_tail_l = _matmul(xs_b, w_in_odd, i, (F32,), 256, tail_pad, col0=main_odd, n_cols=tail_pad,
                                    emit_w=True)
            kc0 = h_c
            vc0 = h_c + n_kv
            qic = (h_c * HEAD_DIM + 2 * kvw) // qiw

            def kz_of(ki):
                z = jnp.zeros_like(ki)
                return jnp.concatenate([ki, z, z, ki], axis=-1).astype(BF16).reshape(-1, 2 * LANES)

            pf, pb = _matmul(xp_b, w_main_l, None, (F32, BF16), 1024, 512)
            (pt,) = _matmul(xp_b, w_tail_l, None, (F32,), 1024, tail_pad)
            ki_p = pt[:, :IDX_DIM].reshape(bp, sp, IDX_DIM)
            wi_p = pt[:, IDX_DIM:IDX_DIM + IDX_HEADS]
            sel_p = _dsa_index(pb, wi_p, kz_of(ki_p), bp, sp, sp, sp, 128, 512, 0, qic)
            mix_p = _dsa_attn(pb, pb, pb, sel_p, bp, sp, sp, n_kv, 256, 512, 0, kc0, vc0)
            pf4 = pf.reshape(bp, sp, -1)
            q_w = h_c * HEAD_DIM
            outs["ckp"].append(pf4[:, :, q_w:q_w + kvw].reshape(bp, sp, n_kv, HEAD_DIM))
            outs["cvp"].append(pf4[:, :, q_w + kvw:q_w + 2 * kvw].reshape(bp, sp, n_kv, HEAD_DIM))
            outs["cip"].append(ki_p)
            sf4 = sf.reshape(bs, n_new, -1)
            k_new = sf4[:, :, q_w:q_w + kvw]
            v_new = sf4[:, :, q_w + kvw:q_w + 2 * kvw]
            ki_new = stl[:, :IDX_DIM].reshape(bs, n_new, IDX_DIM)
            wi_s = stl[:, IDX_DIM:IDX_DIM + IDX_HEADS]
            k_all = _pad_rows(jnp.concatenate([cache_c_k[i].reshape(bs, past, kvw), k_new], axis=1), l_s_pad)
            v_all = _pad_rows(jnp.concatenate([cache_c_v[i].reshape(bs, past, kvw), v_new], axis=1), l_s_pad)
            ki_all = _pad_rows(jnp.concatenate([cache_c_idx[i], ki_new], axis=1), l_s_pad)
            sel_s = _dsa_index(sb, wi_s, kz_of(ki_all), bs, n_new, l_s_pad, l_s, n_new, l_s_pad, past, qic)
            mix_s = _dsa_attn(sb, k_all.astype(BF16).reshape(bs * l_s_pad, kvw),
                              v_all.astype(BF16).reshape(bs * l_s_pad, kvw), sel_s,
                              bs, n_new, l_s_pad, n_kv, n_new, l_s_pad, past, 0, 0)
            outs["cks"].append(k_new.reshape(bs, n_new, n_kv, HEAD_DIM))
            outs["cvs"].append(v_new.reshape(bs, n_new, n_kv, HEAD_DIM))
            outs["cis"].append(ki_new)

        xs, xs_b, w_out_l = _matmul_residual_ln(mix_s, w_out, i, xs, ln_g[l, 0], ln_b[l, 0], 256, 512, emit_w=True)
        xp, xp_b = _matmul_residual_ln(mix_p, w_out_l, None, xp, ln_g[l, 0], ln_b[l, 0], 512, 512)

        (mkv,) = _matmul(mem2d, w_mem_kv, l, (F32,), 512, 512)
        mk_p, mv_p = mkv[:, :mem_w], mkv[:, mem_w:]
        outs["mkp"].append(mk_p.reshape(bp, n_mem, MEM_HEADS, HEAD_DIM))
        outs["mvp"].append(mv_p.reshape(bp, n_mem, MEM_HEADS, HEAD_DIM))
        xp, xp_b = _mem_attn_ln(xp, wmq, wmo, l, mk_p, mv_p, ln_g[l, 1], ln_b[l, 1], 256, 256, sp)
        xs, xs_b = _mem_attn_ln(xs, wmq, wmo, l, cmk[l], cmv[l], ln_g[l, 1], ln_b[l, 1],
                                min(ms, 128), n_new, n_new)

        cb_l = conv_b[l][None]
        ffn_tm = min(FFN_TM, sp)
        act_s, st_s, w_val_l, w_gate_l = _ffn_up(xs_b, w_up, w_up, l, conv_w[l], cb_l, state_ffn_conv[l],
                                                 ms, n_new, n_new, ff_pad, emit_w=True)
        act_p, st_p = _ffn_up(xp_b, w_val_l, w_gate_l, None, conv_w[l], cb_l, None, ffn_tm, ffn_tm, sp, ff_pad)
        tiles_per_seq = sp // ffn_tm
        outs["fp"].append(st_p[tiles_per_seq - 1::tiles_per_seq])
        outs["fs"].append(st_s)
        xs, xs_b, w_down_l = _matmul_residual_ln(act_s, w_down, l, xs, ln_g[l, 2], ln_b[l, 2], 256, 512, emit_w=True)
        xp, xp_b = _matmul_residual_ln(act_p, w_down_l, None, xp, ln_g[l, 2], ln_b[l, 2], 512, 512)

    st = jnp.stack
    o = outs
    return (xp.reshape(bp, sp, d), xs.reshape(bs, n_new, d),
            st(o["akp"]), st(o["avp"]), st(o["bkp"]), st(o["bvp"]), st(o["ckp"]), st(o["cvp"]), st(o["cip"]),
            st(o["mkp"]), st(o["mvp"]), st(o["fp"]),
            st(o["aks"]), st(o["avs"]), st(o["bks"]), st(o["bvs"]), st(o["cks"]), st(o["cvs"]), st(o["cis"]),
            st(o["fs"]))
```

```python
import functools

import numpy as np
import jax
import jax.numpy as jnp
from jax import lax
from jax.experimental import pallas as pl
from jax.experimental.pallas import tpu as pltpu

BF16 = jnp.bfloat16
F32 = jnp.float32

CHUNK = 64
HEAD_DIM = 128
A_LEFT_CHUNKS = 8
REL_CLIP = 128
C_GROUP = 4
IDX_HEADS = 32
IDX_DIM = 64
TOPK_MAX = 256
MEM_HEADS = 4
CONV_W = 3
DEPTH = 4
ALPHA = (2.0 * DEPTH) ** 0.25
LN_EPS = 1e-5
ATTN_SCALE = HEAD_DIM ** -0.5
IDX_SCALE = IDX_DIM ** -0.5
IDX_W_SCALE = IDX_HEADS ** -0.5
NEG_INF = -1e30

V7X_VMEM_BYTES = 64 * 1024 * 1024
LANES = 128
SUBLANES = 8
SB_EXIT = 104.0
INT_MIN = -(2 ** 31)
NEG_KEY = int(np.array(NEG_INF, np.float32).view(np.int32)) ^ 0x7FFFFFFF


def _cparams(vmem_bytes):
    limit = int(min(max(vmem_bytes * 1.25 + (4 << 20), 32 << 20), V7X_VMEM_BYTES - (6 << 20)))
    return pltpu.CompilerParams(vmem_limit_bytes=limit)


def _nt_dot(a, b):
    return lax.dot_general(a, b, (((1,), (1,)), ((), ())), preferred_element_type=F32)


def _w_spec(w, l, block, index):
    if l is None:
        return pl.BlockSpec(block, index)
    return pl.BlockSpec((None,) + block, lambda *grid: (l,) + tuple(index(*grid)))


def _mm_body(x_ref, w_ref, *out_refs, emit_w):
    if emit_w:
        out_refs[-1][...] = w_ref[...].astype(BF16)
        w_ref, out_refs = out_refs[-1], out_refs[:-1]
    r = jnp.dot(x_ref[...].astype(BF16), w_ref[...].astype(BF16), preferred_element_type=F32)
    for o in out_refs:
        o[...] = r.astype(o.dtype)


def _matmul(x, w, l, out_dtypes, tm, tn, col0=0, n_cols=None, emit_w=False):
    m, kd = x.shape
    n = n_cols or w.shape[-1]
    tm, tn = min(tm, m), min(tn, n)
    assert m % tm == 0 and n % tn == 0 and col0 % tn == 0, (x.shape, w.shape, tm, tn, col0)
    assert not emit_w or m == tm
    j0 = col0 // tn
    wsz = w.dtype.itemsize
    vmem = 2 * (tm * kd * x.dtype.itemsize + kd * tn * wsz) + kd * tn * 2 * (3 if emit_w else 1) + 2 * tm * tn * 4
    vmem += sum(2 * tm * tn * jnp.dtype(d).itemsize for d in out_dtypes)
    out_specs = [pl.BlockSpec((tm, tn), lambda i, j: (i, j)) for _ in out_dtypes]
    out_shape = [jax.ShapeDtypeStruct((m, n), d) for d in out_dtypes]
    if emit_w:
        out_specs.append(pl.BlockSpec((kd, tn), lambda i, j: (0, j)))
        out_shape.append(jax.ShapeDtypeStruct((kd, n), BF16))
    return pl.pallas_call(
        functools.partial(_mm_body, emit_w=emit_w),
        grid=(m // tm, n // tn),
        in_specs=[pl.BlockSpec((tm, kd), lambda i, j: (i, 0)),
                  _w_spec(w, l, (kd, tn), lambda i, j: (0, j0 + j))],
        out_specs=out_specs,
        out_shape=out_shape,
        compiler_params=_cparams(vmem),
        name="matmul",
    )(x, w)


LN_ROWS = 128


def _residual_ln_rows(y_ref, res_ref, g_ref, b_ref, o_ref, ob_ref, rows):
    g = g_ref[...]
    b = b_ref[...]

    def chunk(c, carry):
        sl = pl.ds(pl.multiple_of(c * LN_ROWS, LN_ROWS), LN_ROWS)
        y = ALPHA * res_ref[sl, :] + y_ref[sl, :]
        mu = jnp.mean(y, axis=-1, keepdims=True)
        d = y - mu
        var = jnp.mean(d * d, axis=-1, keepdims=True)
        out = d * lax.rsqrt(var + LN_EPS) * g + b
        o_ref[sl, :] = out
        ob_ref[sl, :] = out.astype(BF16)
        return carry

    lax.fori_loop(0, rows // LN_ROWS, chunk, 0)


MM_LN_CHUNK = 512


def _mm_ln_body(x_ref, w_ref, res_ref, g_ref, b_ref, o_ref, ob_ref, *rest, n_k, tm, emit_w, w_rows):
    k = pl.program_id(1)
    x = x_ref[...]
    n = o_ref.shape[1]
    chunks = [slice(c, c + MM_LN_CHUNK) for c in range(0, n, MM_LN_CHUNK)]
    if emit_w:
        tk = w_ref.shape[0]
        row = k * tk + lax.broadcasted_iota(jnp.int32, (tk, 1), 0)
        (wb_ref,) = rest
        wb_ref[...] = jnp.where(row < w_rows, w_ref[...], 0.0).astype(BF16)
        w_ref = wb_ref

    @pl.when(k == 0)
    def _():
        for sl in chunks:
            o_ref[:, sl] = jnp.dot(x, w_ref[:, sl], preferred_element_type=F32)

    @pl.when(k > 0)
    def _():
        for sl in chunks:
            o_ref[:, sl] += jnp.dot(x, w_ref[:, sl], preferred_element_type=F32)

    @pl.when(k == n_k - 1)
    def _():
        _residual_ln_rows(o_ref, res_ref, g_ref, b_ref, o_ref, ob_ref, tm)


def _matmul_residual_ln(x, w, l, res, g, b, tm, tk, emit_w=False):
    m, kd = x.shape
    w_rows, n = w.shape[-2:]
    tm, tk = min(tm, m), min(tk, kd)
    assert m % tm == 0 and kd % tk == 0 and tm % LN_ROWS == 0 and n % MM_LN_CHUNK == 0
    assert (emit_w and m == tm) or w_rows == kd
    n_k = kd // tk
    vmem = 2 * (tm * tk * 2 + tk * n * w.dtype.itemsize + tm * n * 4 + tm * n * 4 + tm * n * 2)
    out_specs = [pl.BlockSpec((tm, n), lambda i, k: (i, 0)), pl.BlockSpec((tm, n), lambda i, k: (i, 0))]
    out_shape = [jax.ShapeDtypeStruct((m, n), F32), jax.ShapeDtypeStruct((m, n), BF16)]
    if emit_w:
        vmem += 3 * tk * n * 2
        out_specs.append(pl.BlockSpec((tk, n), lambda i, k: (k, 0)))
        out_shape.append(jax.ShapeDtypeStruct((kd, n), BF16))
    return pl.pallas_call(
        functools.partial(_mm_ln_body, n_k=n_k, tm=tm, emit_w=emit_w, w_rows=w_rows),
        grid=(m // tm, n_k),
        in_specs=[pl.BlockSpec((tm, tk), lambda i, k: (i, k)),
                  _w_spec(w, l, (tk, n), lambda i, k: (k, 0)),
                  pl.BlockSpec((tm, n), lambda i, k: (i, 0)),
                  pl.BlockSpec((1, n), lambda i, k: (0, 0)),
                  pl.BlockSpec((1, n), lambda i, k: (0, 0))],
        out_specs=out_specs,
        out_shape=out_shape,
        compiler_params=_cparams(vmem),
        name="matmul_residual_ln",
    )(x, w, res, g.reshape(1, n), b.reshape(1, n))


BAND_TQ = 128
BAND_KB = 5
ATTN_HB = 4


def _band_prompt_body(q_ref, k_ref, v_ref, bias_ref, o_ref):
    t = pl.program_id(2)
    heads = [slice(h * HEAD_DIM, (h + 1) * HEAD_DIM) for h in range(ATTN_HB)]
    first = t - (BAND_KB - 1)
    keys = [pl.ds(pl.multiple_of(jnp.maximum(first + jb, 0) * BAND_TQ, BAND_TQ), BAND_TQ) for jb in range(BAND_KB)]
    s = jnp.concatenate(
        [jnp.concatenate([_nt_dot(q_ref[:, hs], k_ref[keys[jb], hs]) for hs in heads], axis=0)
         for jb in range(BAND_KB)], axis=1)
    s = s * ATTN_SCALE + bias_ref[...].reshape(ATTN_HB * BAND_TQ, BAND_KB * BAND_TQ)
    col_block = lax.broadcasted_iota(jnp.int32, (1, BAND_KB * BAND_TQ), 1) // BAND_TQ
    s = jnp.where(first + col_block >= 0, s, NEG_INF)
    m = jnp.max(s, axis=-1, keepdims=True)
    p = jnp.exp(s - m)
    l = jnp.sum(p, axis=-1, keepdims=True)
    p = p.astype(BF16)
    for h, hs in enumerate(heads):
        rows = slice(h * BAND_TQ, (h + 1) * BAND_TQ)
        acc = sum(jnp.dot(p[rows, jb * BAND_TQ:(jb + 1) * BAND_TQ], v_ref[keys[jb], hs],
                          preferred_element_type=F32) for jb in range(BAND_KB))
        o_ref[:, hs] = (acc / l[rows]).astype(o_ref.dtype)


def _toeplitz_bias(rel_bias, n_rows, n_cols, offset):
    m = np.arange(n_rows + n_cols - 1) - (n_rows - 1)
    diag = rel_bias[:, np.clip(offset - m, -REL_CLIP, REL_CLIP) + REL_CLIP]
    rows = [diag[:, n_rows - 1 - i:n_rows - 1 - i + n_cols] for i in range(n_rows)]
    return jnp.stack(rows, axis=1).astype(F32)


def _band_attn_prompt(qkv, rel_bias, batch, seq, n_heads, q_col, k_col, v_col):
    assert n_heads % ATTN_HB == 0 and q_col % ATTN_HB == 0 and k_col % ATTN_HB == 0 and v_col % ATTN_HB == 0
    nt = seq // BAND_TQ
    hw = ATTN_HB * HEAD_DIM
    bias = _toeplitz_bias(rel_bias, BAND_TQ, BAND_KB * BAND_TQ, A_LEFT_CHUNKS * CHUNK)
    qi = np.arange(BAND_TQ)[:, None] // CHUNK
    kj = np.arange(BAND_KB * BAND_TQ)[None, :] // CHUNK
    in_band = (kj >= qi) & (kj <= qi + A_LEFT_CHUNKS)
    bias = jnp.where(jnp.asarray(in_band)[None], bias, NEG_INF)
    vmem = 2 * (2 * seq * hw * 2 + ATTN_HB * BAND_TQ * BAND_KB * BAND_TQ * 4) + (8 << 20)
    return pl.pallas_call(
        _band_prompt_body,
        grid=(batch, n_heads // ATTN_HB, nt),
        in_specs=[pl.BlockSpec((BAND_TQ, hw), lambda b, h, t: (b * nt + t, q_col // ATTN_HB + h)),
                  pl.BlockSpec((seq, hw), lambda b, h, t: (b, k_col // ATTN_HB + h)),
                  pl.BlockSpec((seq, hw), lambda b, h, t: (b, v_col // ATTN_HB + h)),
                  pl.BlockSpec((ATTN_HB, BAND_TQ, BAND_KB * BAND_TQ), lambda b, h, t: (h, 0, 0))],
        out_specs=pl.BlockSpec((BAND_TQ, hw), lambda b, h, t: (b * nt + t, h)),
        out_shape=jax.ShapeDtypeStruct((batch * seq, n_heads * HEAD_DIM), BF16),
        compiler_params=_cparams(vmem),
        name="band_attn_prompt",
    )(qkv, qkv, qkv, bias)


def _band_cached_body(q_ref, kc_ref, vc_ref, kn_ref, vn_ref, bias_ref, o_ref, *, n_heads, window):
    for h in range(n_heads):
        hs = slice(h * HEAD_DIM, (h + 1) * HEAD_DIM)
        cached = pl.ds(h, window, stride=n_heads)
        k = jnp.concatenate([kc_ref[cached, :].astype(BF16), kn_ref[:, hs]], axis=0)
        v = jnp.concatenate([vc_ref[cached, :].astype(BF16), vn_ref[:, hs]], axis=0)
        s = _nt_dot(q_ref[:, hs], k) * ATTN_SCALE + bias_ref[h]
        m = jnp.max(s, axis=-1, keepdims=True)
        p = jnp.exp(s - m)
        l = jnp.sum(p, axis=-1, keepdims=True)
        acc = jnp.dot(p.astype(BF16), v, preferred_element_type=F32)
        o_ref[:, hs] = (acc / l).astype(o_ref.dtype)


def _band_attn_cached(q2d, cache_k, cache_v, layer, new2d, rel_bias, batch, n_new, k_col, v_col):
    window, n_heads = cache_k.shape[2:4]
    hw = n_heads * HEAD_DIM
    n_keys = window + n_new
    bias = _toeplitz_bias(rel_bias, n_new, n_keys, window)
    flat = lambda c: c.reshape(c.shape[0], batch, window * n_heads, HEAD_DIM)
    cache_spec = pl.BlockSpec((None, None, window * n_heads, HEAD_DIM), lambda b: (layer, b, 0, 0))
    vmem = 2 * (2 * window * hw * 4 + n_heads * n_new * n_keys * 4) + (8 << 20)
    return pl.pallas_call(
        functools.partial(_band_cached_body, n_heads=n_heads, window=window),
        grid=(batch,),
        in_specs=[pl.BlockSpec((n_new, hw), lambda b: (b, 0)),
                  cache_spec, cache_spec,
                  pl.BlockSpec((n_new, hw), lambda b: (b, k_col)),
                  pl.BlockSpec((n_new, hw), lambda b: (b, v_col)),
                  pl.BlockSpec((n_heads, n_new, n_keys), lambda b: (0, 0, 0))],
        out_specs=pl.BlockSpec((n_new, hw), lambda b: (b, 0)),
        out_shape=jax.ShapeDtypeStruct((batch * n_new, hw), BF16),
        compiler_params=_cparams(vmem),
        name="band_attn_cached",
    )(q2d, flat(cache_k), flat(cache_v), new2d, new2d, bias)


SB_TK = 256


def _split2(x):
    hi = x.astype(BF16)
    return hi, (x - hi.astype(F32)).astype(BF16)


def _sb_window(qs, ks, vs, u, mask, carry, acc, tq):
    z = jnp.concatenate([_nt_dot(q, k) for q, k in zip(qs, ks)], axis=0) * ATTN_SCALE
    sp = jnp.maximum(z, 0.0) + jnp.log(1.0 + jnp.exp(-jnp.abs(z)))
    log_1m = jnp.where(mask, -sp, 0.0)
    suffix = sum(jnp.dot(piece, u, preferred_element_type=F32) for piece in _split2(log_1m))
    w = jnp.where(mask, jnp.exp(z - sp + suffix + carry), 0.0).astype(BF16)
    acc = acc + jnp.concatenate(
        [jnp.dot(w[h * tq:(h + 1) * tq], v, preferred_element_type=F32) for h, v in enumerate(vs)], axis=0)
    return carry + jnp.sum(log_1m, axis=-1, keepdims=True), acc


def _sb_continue(c):
    limit, floor = c[0], c[1]
    return jnp.logical_and(limit > 0, floor > -SB_EXIT)


def _stick_break_body(q_ref, k_ref, v_ref, u_ref, o_ref, *, tq, qpos_base):
    qt = pl.program_id(2)
    qpos0 = qpos_base + qt * tq
    u = u_ref[...]
    rows = ATTN_HB * tq
    rowpos = qpos0 + lax.broadcasted_iota(jnp.int32, (rows, 1), 0) % tq
    top = (qpos0 + tq + LANES - 1) // LANES * LANES
    heads = [slice(h * HEAD_DIM, (h + 1) * HEAD_DIM) for h in range(ATTN_HB)]
    qs = [q_ref[:, hs] for hs in heads]

    def body(c):
        limit, _, carry, acc = c
        start = pl.multiple_of(jnp.maximum(limit - SB_TK, 0), LANES)
        keys = pl.ds(start, SB_TK)
        kpos = start + lax.broadcasted_iota(jnp.int32, (1, SB_TK), 1)
        mask = jnp.logical_and(kpos < rowpos, kpos < limit)
        carry, acc = _sb_window(qs, [k_ref[keys, hs] for hs in heads], [v_ref[keys, hs] for hs in heads],
                                u, mask, carry, acc, tq)
        return limit - SB_TK, jnp.max(carry), carry, acc

    init = (top, jnp.float32(0.0), jnp.zeros((rows, 1), F32), jnp.zeros((rows, HEAD_DIM), F32))
    acc = lax.while_loop(_sb_continue, body, init)[3]
    for h, hs in enumerate(heads):
        o_ref[:, hs] = acc[h * tq:(h + 1) * tq].astype(o_ref.dtype)


def _stick_break_cached_body(q_ref, kc_ref, vc_ref, kn_ref, vn_ref, u_ref, o_ref, *, n_new, past, n_heads):
    hg = pl.program_id(1)
    u = u_ref[...]
    rows = ATTN_HB * n_new
    rowpos = past + lax.broadcasted_iota(jnp.int32, (rows, 1), 0) % n_new
    heads = [slice(h * HEAD_DIM, (h + 1) * HEAD_DIM) for h in range(ATTN_HB)]
    qs = [q_ref[:, hs] for hs in heads]
    first = past - (SB_TK - LANES)
    filler = jnp.zeros((LANES - n_new, HEAD_DIM), BF16)

    def cached(ref, start, size, h):
        row0 = start * n_heads + hg * ATTN_HB + h
        return ref[pl.ds(row0, size, stride=n_heads), :].astype(BF16)

    def newest(cache_ref, new_ref):
        return [jnp.concatenate([cached(cache_ref, first, LANES, h), new_ref[:, hs], filler], axis=0)
                for h, hs in enumerate(heads)]

    kpos = first + lax.broadcasted_iota(jnp.int32, (1, SB_TK), 1)
    carry, acc = _sb_window(qs, newest(kc_ref, kn_ref), newest(vc_ref, vn_ref), u, kpos < rowpos,
                            jnp.zeros((rows, 1), F32), jnp.zeros((rows, HEAD_DIM), F32), n_new)

    def body(c):
        limit, _, carry, acc = c
        start = jnp.maximum(limit - SB_TK, 0)
        mask = jnp.broadcast_to(start + lax.broadcasted_iota(jnp.int32, (1, SB_TK), 1) < limit, (rows, SB_TK))
        carry, acc = _sb_window(qs, [cached(kc_ref, start, SB_TK, h) for h in range(ATTN_HB)],
                                [cached(vc_ref, start, SB_TK, h) for h in range(ATTN_HB)],
                                u, mask, carry, acc, n_new)
        return limit - SB_TK, jnp.max(carry), carry, acc

    acc = lax.while_loop(_sb_continue, body, (jnp.int32(first), jnp.max(carry), carry, acc))[3]
    for h, hs in enumerate(heads):
        o_ref[:, hs] = acc[h * n_new:(h + 1) * n_new].astype(o_ref.dtype)


def _stick_break_cached(q2d, cache_k, cache_v, layer, new2d, batch, n_new, q_col, k_col, v_col):
    past, n_heads = cache_k.shape[2:4]
    assert past % LANES == 0 and past >= SB_TK and n_new <= LANES and n_heads % ATTN_HB == 0
    assert q_col % ATTN_HB == 0 and k_col % ATTN_HB == 0 and v_col % ATTN_HB == 0
    hw = ATTN_HB * HEAD_DIM
    flat = lambda c: c.reshape(c.shape[0], batch, past * n_heads, HEAD_DIM)
    j = np.arange(SB_TK)
    u = jnp.asarray(j[:, None] > j[None, :], BF16)
    cache_spec = pl.BlockSpec((None, None, past * n_heads, HEAD_DIM), lambda b, h: (layer, b, 0, 0))
    vmem = 2 * (2 * past * n_heads * HEAD_DIM * 4) + (8 << 20)
    return pl.pallas_call(
        functools.partial(_stick_break_cached_body, n_new=n_new, past=past, n_heads=n_heads),
        grid=(batch, n_heads // ATTN_HB),
        in_specs=[pl.BlockSpec((n_new, hw), lambda b, h: (b, q_col // ATTN_HB + h)),
                  cache_spec, cache_spec,
                  pl.BlockSpec((n_new, hw), lambda b, h: (b, k_col // ATTN_HB + h)),
                  pl.BlockSpec((n_new, hw), lambda b, h: (b, v_col // ATTN_HB + h)),
                  pl.BlockSpec((SB_TK, SB_TK), lambda b, h: (0, 0))],
        out_specs=pl.BlockSpec((n_new, hw), lambda b, h: (b, h)),
        out_shape=jax.ShapeDtypeStruct((batch * n_new, n_heads * HEAD_DIM), BF16),
        compiler_params=_cparams(vmem),
        name="stick_break_cached",
    )(q2d, flat(cache_k), flat(cache_v), new2d, new2d, u)


def _stick_break_attn(q2d, k2d, v2d, batch, n_q, n_k, n_heads, tq, qpos_base, q_col, k_col, v_col):
    assert n_q % tq == 0 and n_k % LANES == 0 and n_k >= SB_TK and n_heads % ATTN_HB == 0
    assert q_col % ATTN_HB == 0 and k_col % ATTN_HB == 0 and v_col % ATTN_HB == 0
    assert -(-(qpos_base + n_q) // LANES) * LANES <= n_k
    nt = n_q // tq
    hw = ATTN_HB * HEAD_DIM
    j = np.arange(SB_TK)
    u = jnp.asarray(j[:, None] > j[None, :], BF16)
    vmem = 2 * (2 * n_k * hw * 2) + (8 << 20)
    return pl.pallas_call(
        functools.partial(_stick_break_body, tq=tq, qpos_base=qpos_base),
        grid=(batch, n_heads // ATTN_HB, nt),
        in_specs=[pl.BlockSpec((tq, hw), lambda b, h, t: (b * nt + t, q_col // ATTN_HB + h)),
                  pl.BlockSpec((n_k, hw), lambda b, h, t: (b, k_col // ATTN_HB + h)),
                  pl.BlockSpec((n_k, hw), lambda b, h, t: (b, v_col // ATTN_HB + h)),
                  pl.BlockSpec((SB_TK, SB_TK), lambda b, h, t: (0, 0))],
        out_specs=pl.BlockSpec((tq, hw), lambda b, h, t: (b * nt + t, h)),
        out_shape=jax.ShapeDtypeStruct((batch * n_q, n_heads * HEAD_DIM), BF16),
        compiler_params=_cparams(vmem),
        name="stick_break_attn",
    )(q2d, k2d, v2d, u)


def _sortable(x):
    bits = pltpu.bitcast(x, jnp.int32)
    return jnp.where(bits < 0, bits ^ jnp.int32(0x7FFFFFFF), bits)


def _dsa_index_body(qi_ref, wi_ref, kz_ref, bias_ref, key_ref, *, tq, tk, n_real, qpos_base, topk):
    n_pad = key_ref.shape[1]
    qt = pl.program_id(1)
    qpos0 = qpos_base + qt * tq
    rowpos = qpos0 + lax.broadcasted_iota(jnp.int32, (tq, 1), 0)
    qchunk = rowpos // CHUNK
    n_kb = jnp.minimum((((qpos0 + tq - 1) // CHUNK + 1) * CHUNK + tk - 1) // tk, n_pad // tk)
    n_tail = jnp.maximum(n_real - n_kb * tk, 0)
    tail0 = n_kb * tk

    w = wi_ref[...] * (IDX_SCALE * IDX_W_SCALE)

    def block_pos(kb):
        start = pl.multiple_of(kb * tk, tk)
        return start, start + lax.broadcasted_iota(jnp.int32, (1, tk), 1)

    def score_block(kb, carry):
        start, kpos = block_pos(kb)
        kz = kz_ref[pl.ds(start, tk), :]
        acc = jnp.zeros((tq, tk), F32)
        for pair in range(IDX_HEADS // 2):
            qp = qi_ref[:, pair * LANES:(pair + 1) * LANES]
            for half in range(2):
                h = 2 * pair + half
                sc = _nt_dot(qp, kz[:, half * LANES:(half + 1) * LANES])
                acc = acc + jnp.maximum(sc, 0.0) * w[:, h:h + 1]
        key = _sortable(jnp.where(kpos // CHUNK <= qchunk, acc, NEG_INF))
        key_ref[:, pl.ds(start, tk)] = jnp.where(kpos < n_real, key, jnp.int32(INT_MIN))
        return carry

    lax.fori_loop(0, n_kb, score_block, 0)

    def count(pred):
        def blk(kb, acc):
            start, kpos = block_pos(kb)
            hit = pred(key_ref[:, pl.ds(start, tk)], kpos).astype(F32)
            return acc + sum(hit[:, c:c + LANES] for c in range(0, tk, LANES))
        lanes = lax.fori_loop(0, n_kb, blk, jnp.zeros((tq, LANES), F32))
        return jnp.sum(lanes, axis=-1, keepdims=True)

    kf = jnp.float32(topk)
    tail_f = n_tail.astype(F32)
    neg_key = jnp.int32(NEG_KEY)

    def thr_step(i, ans_u):
        cand_u = ans_u | lax.shift_left(jnp.int32(1), 31 - i)
        cand = cand_u ^ jnp.int32(INT_MIN)
        cnt = count(lambda key, kpos: key >= cand) + jnp.where(neg_key >= cand, tail_f, 0.0)
        return jnp.where(cnt >= kf, cand_u, ans_u)

    thr = lax.fori_loop(0, 32, thr_step, jnp.zeros((tq, 1), jnp.int32)) ^ jnp.int32(INT_MIN)
    tail_gt = jnp.where(neg_key > thr, tail_f, 0.0)
    n_gt = count(lambda key, kpos: key > thr) + tail_gt
    n_eq = count(lambda key, kpos: key == thr) + jnp.where(neg_key == thr, tail_f, 0.0)
    need = kf - n_gt

    n_bits = int(np.ceil(np.log2(n_pad)))

    def tie_search():
        def pos_step(i, lo):
            cand = lo + lax.shift_left(jnp.int32(1), n_bits - 1 - i)
            in_tail = jnp.clip(cand - tail0, 0, n_tail).astype(F32)
            cnt = (count(lambda key, kpos: jnp.logical_and(key == thr, kpos < cand))
                   + jnp.where(neg_key == thr, in_tail, 0.0))
            return jnp.where(cnt < need, cand, lo)
        return lax.fori_loop(0, n_bits, pos_step, jnp.zeros((tq, 1), jnp.int32))

    has_ties = jnp.max(jnp.where(n_eq > need, 1.0, 0.0)) > 0.0
    last = lax.cond(has_ties, tie_search, lambda: jnp.full((tq, 1), n_pad, jnp.int32))

    bias_ref[0] = jnp.full(bias_ref.shape[1:], NEG_INF, bias_ref.dtype)

    def write_block(kb, carry):
        start, kpos = block_pos(kb)
        key = key_ref[:, pl.ds(start, tk)]
        sel = jnp.logical_or(key > thr, jnp.logical_and(key == thr, kpos <= last))
        ok = jnp.logical_and(jnp.logical_and(sel, kpos // CHUNK <= qchunk), kpos < n_real)
        bias_ref[0, :, pl.ds(start, tk)] = jnp.where(ok, 0.0, NEG_INF).astype(bias_ref.dtype)
        return carry

    lax.fori_loop(0, n_kb, write_block, 0)


def _dsa_index(qi2d, wi, kz, batch, n_q, n_k, n_real, tq, tk, qpos_base, qi_col):
    nt = n_q // tq
    topk = min(TOPK_MAX, n_real // 4)
    vmem = 2 * (tq * 2048 * 2 + n_k * 256 * 2 + tq * n_k * 2) + tq * n_k * 4 + 8 * tq * tk * 4 + (4 << 20)
    return pl.pallas_call(
        functools.partial(_dsa_index_body, tq=tq, tk=tk, n_real=n_real, qpos_base=qpos_base, topk=topk),
        grid=(batch, nt),
        in_specs=[pl.BlockSpec((tq, IDX_HEADS * IDX_DIM), lambda b, t: (b * nt + t, qi_col)),
                  pl.BlockSpec((tq, IDX_HEADS), lambda b, t: (b * nt + t, 0)),
                  pl.BlockSpec((n_k, 2 * LANES), lambda b, t: (b, 0))],
        out_specs=pl.BlockSpec((1, tq, n_k), lambda b, t: (b, t, 0)),
        out_shape=jax.ShapeDtypeStruct((batch, n_q, n_k), BF16),
        scratch_shapes=[pltpu.VMEM((tq, n_k), jnp.int32)],
        compiler_params=_cparams(vmem),
        name="dsa_index",
    )(qi2d, wi, kz)


def _dsa_attn_body(*refs, tq, tk, qpos_base, cached):
    if cached:
        q_ref, kc_ref, vc_ref, kn_ref, vn_ref, bias_ref, slope_ref, o_ref, s_ref, p_ref, k_ref, v_ref = refs
        past, n_kv = cached
        n_new = kn_ref.shape[0]
        own = pl.ds(pl.program_id(1), past, stride=n_kv)
        for full_ref, cache_ref, new_ref in ((k_ref, kc_ref, kn_ref), (v_ref, vc_ref, vn_ref)):
            full_ref[0:past, :] = cache_ref[own, :].astype(BF16)
            full_ref[past:past + n_new, :] = new_ref[...]
            full_ref[past + n_new:, :] = jnp.zeros((full_ref.shape[0] - past - n_new, HEAD_DIM), BF16)
    else:
        q_ref, k_ref, v_ref, bias_ref, slope_ref, o_ref, s_ref, p_ref = refs
    n_pad = k_ref.shape[0]
    qt = pl.program_id(2)
    qpos0 = qpos_base + qt * tq
    tile_end = qpos0 + tq
    rows = C_GROUP * tq
    rowpos = qpos0 + lax.broadcasted_iota(jnp.int32, (tq, 1), 0)
    n_kb = jnp.minimum((((qpos0 + tq - 1) // CHUNK + 1) * CHUNK + tk - 1) // tk, n_pad // tk)
    qs = jnp.concatenate([q_ref[:, g * HEAD_DIM:(g + 1) * HEAD_DIM] for g in range(C_GROUP)], axis=0)

    def key_rows(kb):
        return pl.ds(pl.multiple_of(kb * tk, tk), tk)

    def scores(kb):
        s_ref[kb % 2] = _nt_dot(qs, k_ref[key_rows(kb), :])

    def weighted_values(kb):
        return jnp.dot(p_ref[(kb + 2) % 2], v_ref[key_rows(jnp.maximum(kb, 0)), :], preferred_element_type=F32)

    slope = slope_ref[0]

    def softmax(kb, m, l, has_later_keys):
        kpos = kb * tk + lax.broadcasted_iota(jnp.int32, (1, tk), 1)
        back = (tile_end - kpos).astype(F32)
        s = s_ref[kb % 2] * ATTN_SCALE - slope * back
        extra = bias_ref[0, :, key_rows(kb)].astype(F32)
        s = (s.reshape(C_GROUP, tq, tk) + extra[None]).reshape(rows, tk)
        if has_later_keys:
            ahead = jnp.where(kpos > rowpos, (rowpos - kpos).astype(F32), 0.0)
            s = s + (2.0 * slope) * jnp.concatenate([ahead] * C_GROUP, axis=0)
        m_new = jnp.maximum(m, jnp.max(s, axis=-1, keepdims=True))
        p = jnp.exp(s - m_new)
        p_ref[kb % 2] = p.astype(BF16)
        a = jnp.exp(m - m_new)
        return m_new, a * l + jnp.sum(p, axis=-1, keepdims=True), a

    def trip(kb, state):
        m, l, a_prev, acc = state
        scores(kb + 1)
        acc = a_prev * acc + weighted_values(kb - 1)
        m, l, a = softmax(kb, m, l, False)
        return m, l, a, acc

    p_ref[1] = jnp.zeros(p_ref.shape[1:], BF16)
    scores(0)
    init = (jnp.full((rows, 1), 0.5 * NEG_INF, F32), jnp.zeros((rows, 1), F32),
            jnp.ones((rows, 1), F32), jnp.zeros((rows, HEAD_DIM), F32))
    last = n_kb - 1
    m, l, a_prev, acc = lax.fori_loop(0, last, trip, init)
    acc = a_prev * acc + weighted_values(last - 1)
    m, l, a = softmax(last, m, l, True)
    acc = a * acc + weighted_values(last)
    o = (acc / l).astype(o_ref.dtype)
    for g in range(C_GROUP):
        o_ref[:, g * HEAD_DIM:(g + 1) * HEAD_DIM] = o[g * tq:(g + 1) * tq]


def _dsa_attn(q2d, k2d, v2d, bias, batch, n_q, n_k, n_kv, tq, tk, qpos_base, k_col, v_col, cache=None):
    assert tk % tq == 0 and qpos_base % tq == 0 and n_k % tk == 0
    nt = n_q // tq
    gw = C_GROUP * HEAD_DIM
    n_heads = n_kv * C_GROUP
    slopes = 2.0 ** (-8.0 * np.arange(1, n_heads + 1) / n_heads)
    slope_rows = np.repeat(slopes.reshape(n_kv, C_GROUP), tq, axis=1)[..., None].astype(np.float32)
    rows = C_GROUP * tq
    vmem = (2 * (2 * n_k * HEAD_DIM * 2 + tq * n_k * 2 + 2 * tq * gw * 2 + rows * LANES * 4)
            + 2 * rows * tk * 6 + 8 * rows * tk * 4 + (4 << 20))
    scratch = [pltpu.VMEM((2, rows, tk), F32), pltpu.VMEM((2, rows, tk), BF16)]
    if cache is None:
        cached = None
        kv_specs = [pl.BlockSpec((n_k, HEAD_DIM), lambda b, n, t: (b, k_col + n)),
                    pl.BlockSpec((n_k, HEAD_DIM), lambda b, n, t: (b, v_col + n))]
        kv_args = [k2d, v2d]
    else:
        cache_k, cache_v, layer = cache
        past = cache_k.shape[2]
        assert nt == 1 and qpos_base == past and past + n_q <= n_k and cache_k.shape[3] == n_kv
        cached = (past, n_kv)
        flat = lambda c: c.reshape(c.shape[0], batch, past * n_kv, HEAD_DIM)
        cache_spec = pl.BlockSpec((None, None, past * n_kv, HEAD_DIM), lambda b, n, t: (layer, b, 0, 0))
        kv_specs = [cache_spec, cache_spec,
                    pl.BlockSpec((n_q, HEAD_DIM), lambda b, n, t: (b, k_col + n)),
                    pl.BlockSpec((n_q, HEAD_DIM), lambda b, n, t: (b, v_col + n))]
        kv_args = [flat(cache_k), flat(cache_v), k2d, v2d]
        scratch += [pltpu.VMEM((n_k, HEAD_DIM), BF16), pltpu.VMEM((n_k, HEAD_DIM), BF16)]
        vmem += 2 * 2 * past * n_kv * HEAD_DIM * 4
    return pl.pallas_call(
        functools.partial(_dsa_attn_body, tq=tq, tk=tk, qpos_base=qpos_base, cached=cached),
        grid=(batch, n_kv, nt),
        in_specs=[pl.BlockSpec((tq, gw), lambda b, n, t: (b * nt + t, n))] + kv_specs + [
            pl.BlockSpec((1, tq, n_k), lambda b, n, t: (b, t, 0)),
            pl.BlockSpec((1, rows, 1), lambda b, n, t: (n, 0, 0))],
        out_specs=pl.BlockSpec((tq, gw), lambda b, n, t: (b * nt + t, n)),
        out_shape=jax.ShapeDtypeStruct((batch * n_q, n_heads * HEAD_DIM), BF16),
        scratch_shapes=scratch,
        compiler_params=_cparams(vmem),
        name="dsa_attn",
    )(q2d, *kv_args, bias, jnp.asarray(slope_rows))


def _mem_attn_body(x_ref, wq_ref, mk_ref, mv_ref, wo_ref, g_ref, b_ref, o_ref, ob_ref, y_ref, *,
                   tm, rows_per_seq, n_mem):
    q = jnp.dot(x_ref[...].astype(BF16), wq_ref[...], preferred_element_type=F32).astype(BF16)
    n_seq = tm // rows_per_seq
    heads = [slice(h * HEAD_DIM, (h + 1) * HEAD_DIM) for h in range(MEM_HEADS)]
    s = jnp.concatenate(
        [_nt_dot(q[s_i * rows_per_seq:(s_i + 1) * rows_per_seq, hs],
                 mk_ref[s_i * n_mem:(s_i + 1) * n_mem, hs].astype(BF16))
         for hs in heads for s_i in range(n_seq)], axis=0) * ATTN_SCALE
    m = jnp.max(s, axis=-1, keepdims=True)
    p = jnp.exp(s - m)
    l = jnp.sum(p, axis=-1, keepdims=True)
    p = p.astype(BF16)
    o_heads = []
    for h, hs in enumerate(heads):
        pieces = []
        for s_i in range(n_seq):
            rows = slice(h * tm + s_i * rows_per_seq, h * tm + (s_i + 1) * rows_per_seq)
            o = jnp.dot(p[rows], mv_ref[s_i * n_mem:(s_i + 1) * n_mem, hs].astype(BF16),
                        preferred_element_type=F32)
            pieces.append((o / l[rows]).astype(BF16))
        o_heads.append(pieces[0] if n_seq == 1 else jnp.concatenate(pieces, axis=0))
    y_ref[...] = jnp.dot(jnp.concatenate(o_heads, axis=-1), wo_ref[...], preferred_element_type=F32)
    _residual_ln_rows(y_ref, x_ref, g_ref, b_ref, o_ref, ob_ref, tm)


def _mem_attn_ln(x, wq, wo, l, mk, mv, g, b, tm, rows_per_seq, seq_rows):
    m, d = x.shape
    mw = wq.shape[-1]
    n_seq_tile = tm // rows_per_seq
    n_mem = mk.shape[0] // (m // seq_rows)
    if n_seq_tile == 1:
        mem_map = lambda i: ((i * tm) // seq_rows, 0)
    else:
        assert rows_per_seq == seq_rows
        mem_map = lambda i: (i, 0)
    vmem = 2 * (tm * d * 4 * 2 + tm * d * 2 + 2 * d * mw * 2 + 2 * n_seq_tile * n_mem * mw * 4) + tm * d * 4
    out_specs = [pl.BlockSpec((tm, d), lambda i: (i, 0)), pl.BlockSpec((tm, d), lambda i: (i, 0))]
    out_shape = [jax.ShapeDtypeStruct((m, d), F32), jax.ShapeDtypeStruct((m, d), BF16)]
    return pl.pallas_call(
        functools.partial(_mem_attn_body, tm=tm, rows_per_seq=rows_per_seq, n_mem=n_mem),
        grid=(m // tm,),
        in_specs=[pl.BlockSpec((tm, d), lambda i: (i, 0)),
                  _w_spec(wq, l, (d, mw), lambda i: (0, 0)),
                  pl.BlockSpec((n_seq_tile * n_mem, mw), mem_map),
                  pl.BlockSpec((n_seq_tile * n_mem, mw), mem_map),
                  _w_spec(wo, l, (mw, d), lambda i: (0, 0)),
                  pl.BlockSpec((1, d), lambda i: (0, 0)),
                  pl.BlockSpec((1, d), lambda i: (0, 0))],
        out_specs=out_specs,
        out_shape=out_shape,
        scratch_shapes=[pltpu.VMEM((tm, d), F32)],
        compiler_params=_cparams(vmem),
        name="mem_attn_ln",
    )(x, wq, mk, mv, wo, g.reshape(1, d), b.reshape(1, d))


FFN_TN = 256
FFN_SUB = 256
FFN_TM = 2048
HALO = 8


def _gelu(x):
    return 0.5 * x * (1.0 + lax.erf(x * np.float32(2.0 ** -0.5)))


def _ffn_up_body(x_ref, wa_ref, wg_ref, cwa_ref, cwg_ref, cba_ref, cbg_ref, *rest,
                 tm, rows_per_seq, tiles_per_seq, carried, nj, emit_w):
    if carried:
        act_ref, sta_ref, stg_ref, carry_ref, ext_ref = rest
    elif emit_w:
        pa_ref, pg_ref, act_ref, sta_ref, stg_ref, wab_ref, wgb_ref, ext_ref = rest
    else:
        pa_ref, pg_ref, act_ref, sta_ref, stg_ref, ext_ref = rest
    i = pl.program_id(0)
    j = pl.program_id(1)

    def conv_gate(base, rows):
        cw = jnp.concatenate([cwa_ref[...], cwg_ref[...]], axis=1)
        cb = jnp.concatenate([cba_ref[...], cbg_ref[...]], axis=1)
        h = (ext_ref[base:base + rows, :] * cw[2:3, :]
             + ext_ref[base - 1:base - 1 + rows, :] * cw[1:2, :]
             + ext_ref[base - 2:base - 2 + rows, :] * cw[0:1, :]
             + cb)
        return (h[:, :FFN_TN] * _gelu(h[:, FFN_TN:])).astype(act_ref.dtype)

    def put_state(s_i, last2):
        sta_ref[s_i] = last2[:, :FFN_TN]
        stg_ref[s_i] = last2[:, FFN_TN:]

    @pl.when(j < nj)
    def _():
        if carried:
            @pl.when(i % tiles_per_seq == 0)
            def _():
                carry_ref[j] = jnp.zeros((2, 2 * FFN_TN), F32)

            ext_ref[HALO - 2:HALO, :] = carry_ref[j]
            sub = min(FFN_SUB, tm)
            for r in range(0, tm, sub):
                ext_ref[HALO + r:HALO + r + sub, :FFN_TN] = jnp.dot(x_ref[r:r + sub, :], wa_ref[...],
                                                                    preferred_element_type=F32)
                ext_ref[HALO + r:HALO + r + sub, FFN_TN:] = jnp.dot(x_ref[r:r + sub, :], wg_ref[...],
                                                                    preferred_element_type=F32)
                act_ref[r:r + sub, :] = conv_gate(HALO + r, sub)
            last2 = ext_ref[HALO + tm - 2:HALO + tm, :]
            carry_ref[j] = last2
            put_state(0, last2)
        else:
            wa, wg = wa_ref, wg_ref
            if emit_w:
                wab_ref[...] = wa_ref[...].astype(BF16)
                wgb_ref[...] = wg_ref[...].astype(BF16)
                wa, wg = wab_ref, wgb_ref
            up = jnp.concatenate([jnp.dot(x_ref[...], wa[...], preferred_element_type=F32),
                                  jnp.dot(x_ref[...], wg[...], preferred_element_type=F32)], axis=1)
            for s_i in range(tm // rows_per_seq):
                base = s_i * (rows_per_seq + HALO) + HALO
                rsl = slice(s_i * rows_per_seq, (s_i + 1) * rows_per_seq)
                ext_ref[base - 2:base, :] = jnp.concatenate([pa_ref[s_i], pg_ref[s_i]], axis=1)
                ext_ref[base:base + rows_per_seq, :] = up[rsl]
                put_state(s_i, ext_ref[base + rows_per_seq - 2:base + rows_per_seq, :])
                act_ref[rsl, :] = conv_gate(base, rows_per_seq)

    @pl.when(j >= nj)
    def _():
        act_ref[...] = jnp.zeros_like(act_ref)


def _ffn_up(x, w_val, w_gate, l, conv_w, conv_b, prev, tm, rows_per_seq, seq_rows, n_act_cols, emit_w=False):
    m, d = x.shape
    f2 = conv_w.shape[1]
    wt = 2 * FFN_TN
    assert f2 % wt == 0 and n_act_cols % FFN_TN == 0
    nj = f2 // wt
    nj_pad = n_act_cols // FFN_TN
    carried = prev is None
    assert not (emit_w and carried)
    n_seq = tm // rows_per_seq
    tiles_per_seq = max(seq_rows // tm, 1)
    n_tiles = m // tm
    val = lambda j: jnp.minimum(j, nj - 1)
    gate0 = 0 if l is None else nj
    in_specs = [pl.BlockSpec((tm, d), lambda i, j: (i, 0)),
                _w_spec(w_val, l, (d, FFN_TN), lambda i, j: (0, val(j))),
                _w_spec(w_gate, l, (d, FFN_TN), lambda i, j: (0, val(j) + gate0)),
                pl.BlockSpec((CONV_W, FFN_TN), lambda i, j: (0, val(j))),
                pl.BlockSpec((CONV_W, FFN_TN), lambda i, j: (0, val(j) + nj)),
                pl.BlockSpec((1, FFN_TN), lambda i, j: (0, val(j))),
                pl.BlockSpec((1, FFN_TN), lambda i, j: (0, val(j) + nj))]
    args = [x, w_val, w_gate, conv_w, conv_w, conv_b, conv_b]
    scratch = []
    if carried:
        assert n_seq == 1
        scratch.append(pltpu.VMEM((nj, 2, wt), F32))
    else:
        in_specs += [pl.BlockSpec((n_seq, 2, FFN_TN), lambda i, j: (i, 0, val(j))),
                     pl.BlockSpec((n_seq, 2, FFN_TN), lambda i, j: (i, 0, val(j) + nj))]
        args += [prev, prev]
    scratch.append(pltpu.VMEM((n_seq * (rows_per_seq + HALO), wt), F32))
    st_shape = jax.ShapeDtypeStruct((n_tiles * n_seq, 2, f2 // 2), F32)
    st_spec = pl.BlockSpec((n_seq, 2, FFN_TN), lambda i, j: (i, 0, val(j)))
    vmem = (2 * (tm * d * 2 + d * wt * w_val.dtype.itemsize + tm * FFN_TN * 2) + tm * wt * 4
            + 8 * min(tm, FFN_SUB) * wt * 4 + (4 << 20))
    out_specs = [pl.BlockSpec((tm, FFN_TN), lambda i, j: (i, j)), st_spec, st_spec]
    out_shape = [jax.ShapeDtypeStruct((m, n_act_cols), BF16), st_shape, st_shape]
    if emit_w:
        vmem += 2 * d * wt * 2
        out_specs += [pl.BlockSpec((d, FFN_TN), lambda i, j: (0, val(j)))] * 2
        out_shape += [jax.ShapeDtypeStruct((d, f2 // 2), BF16)] * 2
    act, st_a, st_g, *w_images = pl.pallas_call(
        functools.partial(_ffn_up_body, tm=tm, rows_per_seq=rows_per_seq, tiles_per_seq=tiles_per_seq,
                          carried=carried, nj=nj, emit_w=emit_w),
        grid=(n_tiles, nj_pad),
        in_specs=in_specs,
        out_specs=out_specs,
        out_shape=out_shape,
        scratch_shapes=scratch,
        compiler_params=_cparams(vmem),
        name="ffn_up_conv_gate",
    )(*args)
    return (act, jnp.concatenate([st_a, st_g], axis=-1), *w_images)


def _pad_rows(a, n_rows):
    return jnp.pad(a, ((0, 0), (0, n_rows - a.shape[1]), (0, 0)))


def kernel(x_prompt, x_sample, cache_a_k, cache_a_v, cache_b_k, cache_b_v, cache_c_k, cache_c_v, cache_c_idx, cache_mem_k, cache_mem_v, state_ffn_conv, mem_prompt, w_in_even, w_out_even, rel_bias_a, w_in_odd, w_out_odd, w_mem_q, w_mem_kv, w_mem_o, w_up, conv_w, conv_b, w_down, ln_g, ln_b):
    bp, sp, d = x_prompt.shape
    bs, n_new, _ = x_sample.shape
    depth = w_up.shape[0]
    h_a = cache_a_k.shape[3]
    h_b = cache_b_k.shape[3]
    n_kv = cache_c_k.shape[3]
    h_c = n_kv * C_GROUP
    hw_a, hw_b, kvw = h_a * HEAD_DIM, h_b * HEAD_DIM, n_kv * HEAD_DIM
    qiw = IDX_HEADS * IDX_DIM
    a_win_s = cache_a_k.shape[2]
    a_win_p = min(A_LEFT_CHUNKS * CHUNK, sp)
    past = cache_b_k.shape[2]
    n_mem = mem_prompt.shape[1]
    mem_w = w_mem_q.shape[2]
    d_ff = w_down.shape[1]
    mp, ms = bp * sp, bs * n_new
    l_s = past + n_new
    l_s_pad = -(-l_s // LANES) * LANES
    main_odd = h_c * HEAD_DIM + 2 * kvw + qiw
    tail_pad = LANES
    ff_pad = -(-d_ff // 1024) * 1024

    wmq = w_mem_q.astype(BF16)
    wmo = w_mem_o.astype(BF16)

    xp = x_prompt.reshape(mp, d)
    xs = x_sample.reshape(ms, d)
    xp_b, xs_b = xp.astype(BF16), xs.astype(BF16)
    mem2d = mem_prompt.reshape(bp * n_mem, d).astype(BF16)
    cmk = cache_mem_k.reshape(depth, bs * n_mem, mem_w).astype(BF16)
    cmv = cache_mem_v.reshape(depth, bs * n_mem, mem_w).astype(BF16)

    outs = {k: [] for k in ("akp", "avp", "bkp", "bvp", "ckp", "cvp", "cip", "mkp", "mvp", "fp",
                            "aks", "avs", "bks", "bvs", "cks", "cvs", "cis", "fs")}

    for l in range(depth):
        i = l // 2
        if l % 2 == 0:
            w_out = w_out_even
            sf, sb, w_in_l = _matmul(xs_b, w_in_even, i, (F32, BF16), 256, 512, emit_w=True)
            pf, pb = _matmul(xp_b, w_in_l, None, (F32, BF16), 1024, 512)
            oa = _band_attn_prompt(pb, rel_bias_a[i], bp, sp, h_a, 0, h_a, 2 * h_a)
            cb0 = 3 * h_a
            ob = _stick_break_attn(pb, pb, pb, bp, sp, sp, h_b, 128, 0, cb0, cb0 + h_b, cb0 + 2 * h_b)
            mix_p = jnp.concatenate([oa, ob], axis=-1)
            pf4 = pf.reshape(bp, sp, -1)
            outs["akp"].append(pf4[:, sp - a_win_p:, hw_a:2 * hw_a].reshape(bp, a_win_p, h_a, HEAD_DIM))
            outs["avp"].append(pf4[:, sp - a_win_p:, 2 * hw_a:3 * hw_a].reshape(bp, a_win_p, h_a, HEAD_DIM))
            o0 = 3 * hw_a
            outs["bkp"].append(pf4[:, :, o0 + hw_b:o0 + 2 * hw_b].reshape(bp, sp, h_b, HEAD_DIM))
            outs["bvp"].append(pf4[:, :, o0 + 2 * hw_b:o0 + 3 * hw_b].reshape(bp, sp, h_b, HEAD_DIM))
            sf4 = sf.reshape(bs, n_new, -1)
            oa_s = _band_attn_cached(sb, cache_a_k, cache_a_v, i, sb, rel_bias_a[i], bs, n_new, 1, 2)
            kb_new = sf4[:, :, o0 + hw_b:o0 + 2 * hw_b]
            vb_new = sf4[:, :, o0 + 2 * hw_b:o0 + 3 * hw_b]
            ob_s = _stick_break_cached(sb, cache_b_k, cache_b_v, i, sb, bs, n_new, cb0, cb0 + h_b, cb0 + 2 * h_b)
            mix_s = jnp.concatenate([oa_s, ob_s], axis=-1)
            outs["aks"].append(sf4[:, :, hw_a:2 * hw_a].reshape(bs, n_new, h_a, HEAD_DIM))
            outs["avs"].append(sf4[:, :, 2 * hw_a:3 * hw_a].reshape(bs, n_new, h_a, HEAD_DIM))
            outs["bks"].append(kb_new.reshape(bs, n_new, h_b, HEAD_DIM))
            outs["bvs"].append(vb_new.reshape(bs, n_new, h_b, HEAD_DIM))
        else:
            w_out = w_out_odd
            sf, sb, w_main_l = _matmul(xs_b, w_in_odd, i, (F32, BF16), 256, 512, n_cols=main_odd, emit_w=True)
            stl, w_tail_l = _matmul(xs_b, w_in_odd, i, (F32,), 256, tail_pad, col0=main_odd, n_cols=tail_pad,
                                    emit_w=True)
            kc0 = h_c
            vc0 = h_c + n_kv
            qic = (h_c * HEAD_DIM + 2 * kvw) // qiw

            def kz_of(ki):
                z = jnp.zeros_like(ki)
                return jnp.concatenate([ki, z, z, ki], axis=-1).astype(BF16).reshape(-1, 2 * LANES)

            pf, pb = _matmul(xp_b, w_main_l, None, (F32, BF16), 1024, 512)
            (pt,) = _matmul(xp_b, w_tail_l, None, (F32,), 1024, tail_pad)
            ki_p = pt[:, :IDX_DIM].reshape(bp, sp, IDX_DIM)
            wi_p = pt[:, IDX_DIM:IDX_DIM + IDX_HEADS]
            sel_p = _dsa_index(pb, wi_p, kz_of(ki_p), bp, sp, sp, sp, 128, 512, 0, qic)
            mix_p = _dsa_attn(pb, pb, pb, sel_p, bp, sp, sp, n_kv, 256, 512, 0, kc0, vc0)
            pf4 = pf.reshape(bp, sp, -1)
            q_w = h_c * HEAD_DIM
            outs["ckp"].append(pf4[:, :, q_w:q_w + kvw].reshape(bp, sp, n_kv, HEAD_DIM))
            outs["cvp"].append(pf4[:, :, q_w + kvw:q_w + 2 * kvw].reshape(bp, sp, n_kv, HEAD_DIM))
            outs["cip"].append(ki_p)
            sf4 = sf.reshape(bs, n_new, -1)
            k_new = sf4[:, :, q_w:q_w + kvw]
            v_new = sf4[:, :, q_w + kvw:q_w + 2 * kvw]
            ki_new = stl[:, :IDX_DIM].reshape(bs, n_new, IDX_DIM)
            wi_s = stl[:, IDX_DIM:IDX_DIM + IDX_HEADS]
            ki_all = _pad_rows(jnp.concatenate([cache_c_idx[i], ki_new], axis=1), l_s_pad)
            sel_s = _dsa_index(sb, wi_s, kz_of(ki_all), bs, n_new, l_s_pad, l_s, n_new, l_s_pad, past, qic)
            mix_s = _dsa_attn(sb, sb, sb, sel_s, bs, n_new, l_s_pad, n_kv, n_new, l_s_pad, past, kc0, vc0,
                              cache=(cache_c_k, cache_c_v, i))
            outs["cks"].append(k_new.reshape(bs, n_new, n_kv, HEAD_DIM))
            outs["cvs"].append(v_new.reshape(bs, n_new, n_kv, HEAD_DIM))
            outs["cis"].append(ki_new)

        xs, xs_b, w_out_l = _matmul_residual_ln(mix_s, w_out, i, xs, ln_g[l, 0], ln_b[l, 0], 256, 512, emit_w=True)
        xp, xp_b = _matmul_residual_ln(mix_p, w_out_l, None, xp, ln_g[l, 0], ln_b[l, 0], 512, 512)

        (mkv,) = _matmul(mem2d, w_mem_kv, l, (F32,), 512, 512)
        mk_p, mv_p = mkv[:, :mem_w], mkv[:, mem_w:]
        outs["mkp"].append(mk_p.reshape(bp, n_mem, MEM_HEADS, HEAD_DIM))
        outs["mvp"].append(mv_p.reshape(bp, n_mem, MEM_HEADS, HEAD_DIM))
        xp, xp_b = _mem_attn_ln(xp, wmq, wmo, l, mk_p, mv_p, ln_g[l, 1], ln_b[l, 1], 256, 256, sp)
        xs, xs_b = _mem_attn_ln(xs, wmq, wmo, l, cmk[l], cmv[l], ln_g[l, 1], ln_b[l, 1],
                                min(ms, 128), n_new, n_new)

        cb_l = conv_b[l][None]
        ffn_tm = min(FFN_TM, sp)
        act_s, st_s, w_val_l, w_gate_l = _ffn_up(xs_b, w_up, w_up, l, conv_w[l], cb_l, state_ffn_conv[l],
                                                 ms, n_new, n_new, ff_pad, emit_w=True)
        act_p, st_p = _ffn_up(xp_b, w_val_l, w_gate_l, None, conv_w[l], cb_l, None, ffn_tm, ffn_tm, sp, ff_pad)
        tiles_per_seq = sp // ffn_tm
        outs["fp"].append(st_p[tiles_per_seq - 1::tiles_per_seq])
        outs["fs"].append(st_s)
        xs, xs_b, w_down_l = _matmul_residual_ln(act_s, w_down, l, xs, ln_g[l, 2], ln_b[l, 2], 256, 512, emit_w=True)
        xp, xp_b = _matmul_residual_ln(act_p, w_down_l, None, xp, ln_g[l, 2], ln_b[l, 2], 512, 512)

    st = jnp.stack
    o = outs
    assert n_new <= a_win_s
    rolled_k = jnp.concatenate([cache_a_k[:, :, n_new:], st(o["aks"])], axis=2)
    rolled_v = jnp.concatenate([cache_a_v[:, :, n_new:], st(o["avs"])], axis=2)
    return (xp.reshape(bp, sp, d), xs.reshape(bs, n_new, d),
            st(o["akp"]), st(o["avp"]), st(o["bkp"]), st(o["bvp"]), st(o["ckp"]), st(o["cvp"]), st(o["cip"]),
            st(o["mkp"]), st(o["mvp"]), st(o["fp"]),
            rolled_k, rolled_v, st(o["bks"]), st(o["bvs"]), st(o["cks"]), st(o["cvs"]), st(o["cis"]),
            st(o["fs"]))
```

```python
import functools

import numpy as np
import jax
import jax.numpy as jnp
from jax import lax
from jax.experimental import pallas as pl
from jax.experimental.pallas import tpu as pltpu

BF16 = jnp.bfloat16
F32 = jnp.float32

CHUNK = 64
HEAD_DIM = 128
A_LEFT_CHUNKS = 8
REL_CLIP = 128
C_GROUP = 4
IDX_HEADS = 32
IDX_DIM = 64
TOPK_MAX = 256
MEM_HEADS = 4
CONV_W = 3
DEPTH = 4
ALPHA = (2.0 * DEPTH) ** 0.25
LN_EPS = 1e-5
ATTN_SCALE = HEAD_DIM ** -0.5
IDX_SCALE = IDX_DIM ** -0.5
IDX_W_SCALE = IDX_HEADS ** -0.5
NEG_INF = -1e30

V7X_VMEM_BYTES = 64 * 1024 * 1024
LANES = 128
SUBLANES = 8
SB_EXIT = 104.0
INT_MIN = -(2 ** 31)
NEG_KEY = int(np.array(NEG_INF, np.float32).view(np.int32)) ^ 0x7FFFFFFF


def _cparams(vmem_bytes):
    limit = int(min(max(vmem_bytes * 1.25 + (4 << 20), 32 << 20), V7X_VMEM_BYTES - (6 << 20)))
    return pltpu.CompilerParams(vmem_limit_bytes=limit)


def _nt_dot(a, b):
    return lax.dot_general(a, b, (((1,), (1,)), ((), ())), preferred_element_type=F32)


def _w_spec(w, l, block, index):
    if l is None:
        return pl.BlockSpec(block, index)
    return pl.BlockSpec((None,) + block, lambda *grid: (l,) + tuple(index(*grid)))


def _mm_body(x_ref, w_ref, *out_refs, emit_w):
    if emit_w:
        out_refs[-1][...] = w_ref[...].astype(BF16)
        w_ref, out_refs = out_refs[-1], out_refs[:-1]
    r = jnp.dot(x_ref[...].astype(BF16), w_ref[...].astype(BF16), preferred_element_type=F32)
    for o in out_refs:
        o[...] = r.astype(o.dtype)


def _matmul(x, w, l, out_dtypes, tm, tn, col0=0, n_cols=None, emit_w=False):
    m, kd = x.shape
    n = n_cols or w.shape[-1]
    tm, tn = min(tm, m), min(tn, n)
    assert m % tm == 0 and n % tn == 0 and col0 % tn == 0, (x.shape, w.shape, tm, tn, col0)
    assert not emit_w or m == tm
    j0 = col0 // tn
    wsz = w.dtype.itemsize
    vmem = 2 * (tm * kd * x.dtype.itemsize + kd * tn * wsz) + kd * tn * 2 * (3 if emit_w else 1) + 2 * tm * tn * 4
    vmem += sum(2 * tm * tn * jnp.dtype(d).itemsize for d in out_dtypes)
    out_specs = [pl.BlockSpec((tm, tn), lambda i, j: (i, j)) for _ in out_dtypes]
    out_shape = [jax.ShapeDtypeStruct((m, n), d) for d in out_dtypes]
    if emit_w:
        out_specs.append(pl.BlockSpec((kd, tn), lambda i, j: (0, j)))
        out_shape.append(jax.ShapeDtypeStruct((kd, n), BF16))
    return pl.pallas_call(
        functools.partial(_mm_body, emit_w=emit_w),
        grid=(m // tm, n // tn),
        in_specs=[pl.BlockSpec((tm, kd), lambda i, j: (i, 0)),
                  _w_spec(w, l, (kd, tn), lambda i, j: (0, j0 + j))],
        out_specs=out_specs,
        out_shape=out_shape,
        compiler_params=_cparams(vmem),
        name="matmul",
    )(x, w)


LN_ROWS = 128


def _residual_ln_rows(y_ref, res_ref, g_ref, b_ref, o_ref, ob_ref, rows):
    g = g_ref[...]
    b = b_ref[...]

    def chunk(c, carry):
        sl = pl.ds(pl.multiple_of(c * LN_ROWS, LN_ROWS), LN_ROWS)
        y = ALPHA * res_ref[sl, :] + y_ref[sl, :]
        mu = jnp.mean(y, axis=-1, keepdims=True)
        d = y - mu
        var = jnp.mean(d * d, axis=-1, keepdims=True)
        out = d * lax.rsqrt(var + LN_EPS) * g + b
        o_ref[sl, :] = out
        ob_ref[sl, :] = out.astype(BF16)
        return carry

    lax.fori_loop(0, rows // LN_ROWS, chunk, 0)


MM_LN_CHUNK = 512


def _mm_ln_body(x_ref, w_ref, res_ref, g_ref, b_ref, o_ref, ob_ref, *rest, n_k, tm, emit_w, w_rows):
    k = pl.program_id(1)
    x = x_ref[...]
    n = o_ref.shape[1]
    chunks = [slice(c, c + MM_LN_CHUNK) for c in range(0, n, MM_LN_CHUNK)]
    if emit_w:
        tk = w_ref.shape[0]
        row = k * tk + lax.broadcasted_iota(jnp.int32, (tk, 1), 0)
        (wb_ref,) = rest
        wb_ref[...] = jnp.where(row < w_rows, w_ref[...], 0.0).astype(BF16)
        w_ref = wb_ref

    @pl.when(k == 0)
    def _():
        for sl in chunks:
            o_ref[:, sl] = jnp.dot(x, w_ref[:, sl], preferred_element_type=F32)

    @pl.when(k > 0)
    def _():
        for sl in chunks:
            o_ref[:, sl] += jnp.dot(x, w_ref[:, sl], preferred_element_type=F32)

    @pl.when(k == n_k - 1)
    def _():
        _residual_ln_rows(o_ref, res_ref, g_ref, b_ref, o_ref, ob_ref, tm)


def _matmul_residual_ln(x, w, l, res, g, b, tm, tk, emit_w=False):
    m, kd = x.shape
    w_rows, n = w.shape[-2:]
    tm, tk = min(tm, m), min(tk, kd)
    assert m % tm == 0 and kd % tk == 0 and tm % LN_ROWS == 0 and n % MM_LN_CHUNK == 0
    assert (emit_w and m == tm) or w_rows == kd
    n_k = kd // tk
    vmem = 2 * (tm * tk * 2 + tk * n * w.dtype.itemsize + tm * n * 4 + tm * n * 2) + tm * n * 4
    out_specs = [pl.BlockSpec((tm, n), lambda i, k: (i, 0)), pl.BlockSpec((tm, n), lambda i, k: (i, 0))]
    out_shape = [jax.ShapeDtypeStruct((m, n), F32), jax.ShapeDtypeStruct((m, n), BF16)]
    if emit_w:
        vmem += 3 * tk * n * 2
        out_specs.append(pl.BlockSpec((tk, n), lambda i, k: (k, 0)))
        out_shape.append(jax.ShapeDtypeStruct((kd, n), BF16))
    return pl.pallas_call(
        functools.partial(_mm_ln_body, n_k=n_k, tm=tm, emit_w=emit_w, w_rows=w_rows),
        grid=(m // tm, n_k),
        in_specs=[pl.BlockSpec((tm, tk), lambda i, k: (i, k)),
                  _w_spec(w, l, (tk, n), lambda i, k: (k, 0)),
                  pl.BlockSpec((tm, n), lambda i, k: (i, 0), pipeline_mode=pl.Buffered(1)),
                  pl.BlockSpec((1, n), lambda i, k: (0, 0)),
                  pl.BlockSpec((1, n), lambda i, k: (0, 0))],
        out_specs=out_specs,
        out_shape=out_shape,
        compiler_params=_cparams(vmem),
        name="matmul_residual_ln",
    )(x, w, res, g.reshape(1, n), b.reshape(1, n))


BAND_TQ = 128
BAND_KB = 5
ATTN_HB = 4


def _band_prompt_body(q_ref, k_ref, v_ref, bias_ref, o_ref):
    t = pl.program_id(2)
    heads = [slice(h * HEAD_DIM, (h + 1) * HEAD_DIM) for h in range(ATTN_HB)]
    first = t - (BAND_KB - 1)
    keys = [pl.ds(pl.multiple_of(jnp.maximum(first + jb, 0) * BAND_TQ, BAND_TQ), BAND_TQ) for jb in range(BAND_KB)]
    s = jnp.concatenate(
        [jnp.concatenate([_nt_dot(q_ref[:, hs], k_ref[keys[jb], hs]) for hs in heads], axis=0)
         for jb in range(BAND_KB)], axis=1)
    s = s * ATTN_SCALE + bias_ref[...].reshape(ATTN_HB * BAND_TQ, BAND_KB * BAND_TQ)
    col_block = lax.broadcasted_iota(jnp.int32, (1, BAND_KB * BAND_TQ), 1) // BAND_TQ
    s = jnp.where(first + col_block >= 0, s, NEG_INF)
    m = jnp.max(s, axis=-1, keepdims=True)
    p = jnp.exp(s - m)
    l = jnp.sum(p, axis=-1, keepdims=True)
    p = p.astype(BF16)
    for h, hs in enumerate(heads):
        rows = slice(h * BAND_TQ, (h + 1) * BAND_TQ)
        acc = sum(jnp.dot(p[rows, jb * BAND_TQ:(jb + 1) * BAND_TQ], v_ref[keys[jb], hs],
                          preferred_element_type=F32) for jb in range(BAND_KB))
        o_ref[:, hs] = (acc / l[rows]).astype(o_ref.dtype)


def _toeplitz_bias(rel_bias, n_rows, n_cols, offset):
    m = np.arange(n_rows + n_cols - 1) - (n_rows - 1)
    diag = rel_bias[:, np.clip(offset - m, -REL_CLIP, REL_CLIP) + REL_CLIP]
    rows = [diag[:, n_rows - 1 - i:n_rows - 1 - i + n_cols] for i in range(n_rows)]
    return jnp.stack(rows, axis=1).astype(F32)


def _band_attn_prompt(qkv, rel_bias, batch, seq, n_heads, q_col, k_col, v_col):
    assert n_heads % ATTN_HB == 0 and q_col % ATTN_HB == 0 and k_col % ATTN_HB == 0 and v_col % ATTN_HB == 0
    nt = seq // BAND_TQ
    hw = ATTN_HB * HEAD_DIM
    bias = _toeplitz_bias(rel_bias, BAND_TQ, BAND_KB * BAND_TQ, A_LEFT_CHUNKS * CHUNK)
    qi = np.arange(BAND_TQ)[:, None] // CHUNK
    kj = np.arange(BAND_KB * BAND_TQ)[None, :] // CHUNK
    in_band = (kj >= qi) & (kj <= qi + A_LEFT_CHUNKS)
    bias = jnp.where(jnp.asarray(in_band)[None], bias, NEG_INF)
    vmem = 2 * (2 * seq * hw * 2 + ATTN_HB * BAND_TQ * BAND_KB * BAND_TQ * 4) + (8 << 20)
    return pl.pallas_call(
        _band_prompt_body,
        grid=(batch, n_heads // ATTN_HB, nt),
        in_specs=[pl.BlockSpec((BAND_TQ, hw), lambda b, h, t: (b * nt + t, q_col // ATTN_HB + h)),
                  pl.BlockSpec((seq, hw), lambda b, h, t: (b, k_col // ATTN_HB + h)),
                  pl.BlockSpec((seq, hw), lambda b, h, t: (b, v_col // ATTN_HB + h)),
                  pl.BlockSpec((ATTN_HB, BAND_TQ, BAND_KB * BAND_TQ), lambda b, h, t: (h, 0, 0))],
        out_specs=pl.BlockSpec((BAND_TQ, hw), lambda b, h, t: (b * nt + t, h)),
        out_shape=jax.ShapeDtypeStruct((batch * seq, n_heads * HEAD_DIM), BF16),
        compiler_params=_cparams(vmem),
        name="band_attn_prompt",
    )(qkv, qkv, qkv, bias)


def _band_cached_body(q_ref, kc_ref, vc_ref, kn_ref, vn_ref, bias_ref, o_ref, *, n_heads, window):
    for h in range(n_heads):
        hs = slice(h * HEAD_DIM, (h + 1) * HEAD_DIM)
        cached = pl.ds(h, window, stride=n_heads)
        k = jnp.concatenate([kc_ref[cached, :].astype(BF16), kn_ref[:, hs]], axis=0)
        v = jnp.concatenate([vc_ref[cached, :].astype(BF16), vn_ref[:, hs]], axis=0)
        s = _nt_dot(q_ref[:, hs], k) * ATTN_SCALE + bias_ref[h]
        m = jnp.max(s, axis=-1, keepdims=True)
        p = jnp.exp(s - m)
        l = jnp.sum(p, axis=-1, keepdims=True)
        acc = jnp.dot(p.astype(BF16), v, preferred_element_type=F32)
        o_ref[:, hs] = (acc / l).astype(o_ref.dtype)


def _band_attn_cached(q2d, cache_k, cache_v, layer, new2d, rel_bias, batch, n_new, k_col, v_col):
    window, n_heads = cache_k.shape[2:4]
    hw = n_heads * HEAD_DIM
    n_keys = window + n_new
    bias = _toeplitz_bias(rel_bias, n_new, n_keys, window)
    flat = lambda c: c.reshape(c.shape[0], batch, window * n_heads, HEAD_DIM)
    cache_spec = pl.BlockSpec((None, None, window * n_heads, HEAD_DIM), lambda b: (layer, b, 0, 0))
    vmem = 2 * (2 * window * hw * 4 + n_heads * n_new * n_keys * 4) + (8 << 20)
    return pl.pallas_call(
        functools.partial(_band_cached_body, n_heads=n_heads, window=window),
        grid=(batch,),
        in_specs=[pl.BlockSpec((n_new, hw), lambda b: (b, 0)),
                  cache_spec, cache_spec,
                  pl.BlockSpec((n_new, hw), lambda b: (b, k_col)),
                  pl.BlockSpec((n_new, hw), lambda b: (b, v_col)),
                  pl.BlockSpec((n_heads, n_new, n_keys), lambda b: (0, 0, 0))],
        out_specs=pl.BlockSpec((n_new, hw), lambda b: (b, 0)),
        out_shape=jax.ShapeDtypeStruct((batch * n_new, hw), BF16),
        compiler_params=_cparams(vmem),
        name="band_attn_cached",
    )(q2d, flat(cache_k), flat(cache_v), new2d, new2d, bias)


SB_TK = 256


def _split2(x):
    hi = x.astype(BF16)
    return hi, (x - hi.astype(F32)).astype(BF16)


def _sb_window(qs, ks, vs, u, mask, carry, acc, tq):
    z = jnp.concatenate([_nt_dot(q, k) for q, k in zip(qs, ks)], axis=0) * ATTN_SCALE
    sp = jnp.maximum(z, 0.0) + jnp.log(1.0 + jnp.exp(-jnp.abs(z)))
    log_1m = jnp.where(mask, -sp, 0.0)
    suffix = sum(jnp.dot(piece, u, preferred_element_type=F32) for piece in _split2(log_1m))
    w = jnp.where(mask, jnp.exp(z - sp + suffix + carry), 0.0).astype(BF16)
    acc = acc + jnp.concatenate(
        [jnp.dot(w[h * tq:(h + 1) * tq], v, preferred_element_type=F32) for h, v in enumerate(vs)], axis=0)
    return carry + jnp.sum(log_1m, axis=-1, keepdims=True), acc


def _sb_continue(c):
    limit, floor = c[0], c[1]
    return jnp.logical_and(limit > 0, floor > -SB_EXIT)


def _stick_break_body(q_ref, k_ref, v_ref, u_ref, o_ref, *, tq, qpos_base):
    qt = pl.program_id(2)
    qpos0 = qpos_base + qt * tq
    u = u_ref[...]
    rows = ATTN_HB * tq
    rowpos = qpos0 + lax.broadcasted_iota(jnp.int32, (rows, 1), 0) % tq
    top = (qpos0 + tq + LANES - 1) // LANES * LANES
    heads = [slice(h * HEAD_DIM, (h + 1) * HEAD_DIM) for h in range(ATTN_HB)]
    qs = [q_ref[:, hs] for hs in heads]

    def body(c):
        limit, _, carry, acc = c
        start = pl.multiple_of(jnp.maximum(limit - SB_TK, 0), LANES)
        keys = pl.ds(start, SB_TK)
        kpos = start + lax.broadcasted_iota(jnp.int32, (1, SB_TK), 1)
        mask = jnp.logical_and(kpos < rowpos, kpos < limit)
        carry, acc = _sb_window(qs, [k_ref[keys, hs] for hs in heads], [v_ref[keys, hs] for hs in heads],
                                u, mask, carry, acc, tq)
        return limit - SB_TK, jnp.max(carry), carry, acc

    init = (top, jnp.float32(0.0), jnp.zeros((rows, 1), F32), jnp.zeros((rows, HEAD_DIM), F32))
    acc = lax.while_loop(_sb_continue, body, init)[3]
    for h, hs in enumerate(heads):
        o_ref[:, hs] = acc[h * tq:(h + 1) * tq].astype(o_ref.dtype)


def _stick_break_cached_body(q_ref, kc_ref, vc_ref, kn_ref, vn_ref, u_ref, o_ref, *, n_new, past, n_heads):
    hg = pl.program_id(1)
    u = u_ref[...]
    rows = ATTN_HB * n_new
    rowpos = past + lax.broadcasted_iota(jnp.int32, (rows, 1), 0) % n_new
    heads = [slice(h * HEAD_DIM, (h + 1) * HEAD_DIM) for h in range(ATTN_HB)]
    qs = [q_ref[:, hs] for hs in heads]
    first = past - (SB_TK - LANES)
    filler = jnp.zeros((LANES - n_new, HEAD_DIM), BF16)

    def cached(ref, start, size, h):
        row0 = start * n_heads + hg * ATTN_HB + h
        return ref[pl.ds(row0, size, stride=n_heads), :].astype(BF16)

    def newest(cache_ref, new_ref):
        return [jnp.concatenate([cached(cache_ref, first, LANES, h), new_ref[:, hs], filler], axis=0)
                for h, hs in enumerate(heads)]

    kpos = first + lax.broadcasted_iota(jnp.int32, (1, SB_TK), 1)
    carry, acc = _sb_window(qs, newest(kc_ref, kn_ref), newest(vc_ref, vn_ref), u, kpos < rowpos,
                            jnp.zeros((rows, 1), F32), jnp.zeros((rows, HEAD_DIM), F32), n_new)

    def body(c):
        limit, _, carry, acc = c
        start = jnp.maximum(limit - SB_TK, 0)
        mask = jnp.broadcast_to(start + lax.broadcasted_iota(jnp.int32, (1, SB_TK), 1) < limit, (rows, SB_TK))
        carry, acc = _sb_window(qs, [cached(kc_ref, start, SB_TK, h) for h in range(ATTN_HB)],
                                [cached(vc_ref, start, SB_TK, h) for h in range(ATTN_HB)],
                                u, mask, carry, acc, n_new)
        return limit - SB_TK, jnp.max(carry), carry, acc

    acc = lax.while_loop(_sb_continue, body, (jnp.int32(first), jnp.max(carry), carry, acc))[3]
    for h, hs in enumerate(heads):
        o_ref[:, hs] = acc[h * n_new:(h + 1) * n_new].astype(o_ref.dtype)


def _stick_break_cached(q2d, cache_k, cache_v, layer, new2d, batch, n_new, q_col, k_col, v_col):
    past, n_heads = cache_k.shape[2:4]
    assert past % LANES == 0 and past >= SB_TK and n_new <= LANES and n_heads % ATTN_HB == 0
    assert q_col % ATTN_HB == 0 and k_col % ATTN_HB == 0 and v_col % ATTN_HB == 0
    hw = ATTN_HB * HEAD_DIM
    flat = lambda c: c.reshape(c.shape[0], batch, past * n_heads, HEAD_DIM)
    j = np.arange(SB_TK)
    u = jnp.asarray(j[:, None] > j[None, :], BF16)
    cache_spec = pl.BlockSpec((None, None, past * n_heads, HEAD_DIM), lambda b, h: (layer, b, 0, 0))
    vmem = 2 * (2 * past * n_heads * HEAD_DIM * 4) + (8 << 20)
    return pl.pallas_call(
        functools.partial(_stick_break_cached_body, n_new=n_new, past=past, n_heads=n_heads),
        grid=(batch, n_heads // ATTN_HB),
        in_specs=[pl.BlockSpec((n_new, hw), lambda b, h: (b, q_col // ATTN_HB + h)),
                  cache_spec, cache_spec,
                  pl.BlockSpec((n_new, hw), lambda b, h: (b, k_col // ATTN_HB + h)),
                  pl.BlockSpec((n_new, hw), lambda b, h: (b, v_col // ATTN_HB + h)),
                  pl.BlockSpec((SB_TK, SB_TK), lambda b, h: (0, 0))],
        out_specs=pl.BlockSpec((n_new, hw), lambda b, h: (b, h)),
        out_shape=jax.ShapeDtypeStruct((batch * n_new, n_heads * HEAD_DIM), BF16),
        compiler_params=_cparams(vmem),
        name="stick_break_cached",
    )(q2d, flat(cache_k), flat(cache_v), new2d, new2d, u)


def _stick_break_attn(q2d, k2d, v2d, batch, n_q, n_k, n_heads, tq, qpos_base, q_col, k_col, v_col):
    assert n_q % tq == 0 and n_k % LANES == 0 and n_k >= SB_TK and n_heads % ATTN_HB == 0
    assert q_col % ATTN_HB == 0 and k_col % ATTN_HB == 0 and v_col % ATTN_HB == 0
    assert -(-(qpos_base + n_q) // LANES) * LANES <= n_k
    nt = n_q // tq
    hw = ATTN_HB * HEAD_DIM
    j = np.arange(SB_TK)
    u = jnp.asarray(j[:, None] > j[None, :], BF16)
    vmem = 2 * (2 * n_k * hw * 2) + (8 << 20)
    return pl.pallas_call(
        functools.partial(_stick_break_body, tq=tq, qpos_base=qpos_base),
        grid=(batch, n_heads // ATTN_HB, nt),
        in_specs=[pl.BlockSpec((tq, hw), lambda b, h, t: (b * nt + t, q_col // ATTN_HB + h)),
                  pl.BlockSpec((n_k, hw), lambda b, h, t: (b, k_col // ATTN_HB + h)),
                  pl.BlockSpec((n_k, hw), lambda b, h, t: (b, v_col // ATTN_HB + h)),
                  pl.BlockSpec((SB_TK, SB_TK), lambda b, h, t: (0, 0))],
        out_specs=pl.BlockSpec((tq, hw), lambda b, h, t: (b * nt + t, h)),
        out_shape=jax.ShapeDtypeStruct((batch * n_q, n_heads * HEAD_DIM), BF16),
        compiler_params=_cparams(vmem),
        name="stick_break_attn",
    )(q2d, k2d, v2d, u)


def _sortable(x):
    bits = pltpu.bitcast(x, jnp.int32)
    return jnp.where(bits < 0, bits ^ jnp.int32(0x7FFFFFFF), bits)


def _dsa_index_body(qi_ref, wi_ref, kz_ref, bias_ref, key_ref, *, tq, tk, n_real, qpos_base, topk):
    n_pad = key_ref.shape[1]
    qt = pl.program_id(1)
    qpos0 = qpos_base + qt * tq
    rowpos = qpos0 + lax.broadcasted_iota(jnp.int32, (tq, 1), 0)
    qchunk = rowpos // CHUNK
    n_kb = jnp.minimum((((qpos0 + tq - 1) // CHUNK + 1) * CHUNK + tk - 1) // tk, n_pad // tk)
    n_tail = jnp.maximum(n_real - n_kb * tk, 0)
    tail0 = n_kb * tk

    w = wi_ref[...] * (IDX_SCALE * IDX_W_SCALE)

    def block_pos(kb):
        start = pl.multiple_of(kb * tk, tk)
        return start, start + lax.broadcasted_iota(jnp.int32, (1, tk), 1)

    def score_block(kb, carry):
        start, kpos = block_pos(kb)
        kz = kz_ref[pl.ds(start, tk), :]
        acc = jnp.zeros((tq, tk), F32)
        for pair in range(IDX_HEADS // 2):
            qp = qi_ref[:, pair * LANES:(pair + 1) * LANES]
            for half in range(2):
                h = 2 * pair + half
                sc = _nt_dot(qp, kz[:, half * LANES:(half + 1) * LANES])
                acc = acc + jnp.maximum(sc, 0.0) * w[:, h:h + 1]
        key = _sortable(jnp.where(kpos // CHUNK <= qchunk, acc, NEG_INF))
        key_ref[:, pl.ds(start, tk)] = jnp.where(kpos < n_real, key, jnp.int32(INT_MIN))
        return carry

    lax.fori_loop(0, n_kb, score_block, 0)

    def count(pred):
        def blk(kb, acc):
            start, kpos = block_pos(kb)
            hit = pred(key_ref[:, pl.ds(start, tk)], kpos).astype(F32)
            return acc + sum(hit[:, c:c + LANES] for c in range(0, tk, LANES))
        lanes = lax.fori_loop(0, n_kb, blk, jnp.zeros((tq, LANES), F32))
        return jnp.sum(lanes, axis=-1, keepdims=True)

    kf = jnp.float32(topk)
    tail_f = n_tail.astype(F32)
    neg_key = jnp.int32(NEG_KEY)

    def thr_step(i, ans_u):
        cand_u = ans_u | lax.shift_left(jnp.int32(1), 31 - i)
        cand = cand_u ^ jnp.int32(INT_MIN)
        cnt = count(lambda key, kpos: key >= cand) + jnp.where(neg_key >= cand, tail_f, 0.0)
        return jnp.where(cnt >= kf, cand_u, ans_u)

    thr = lax.fori_loop(0, 32, thr_step, jnp.zeros((tq, 1), jnp.int32)) ^ jnp.int32(INT_MIN)
    tail_gt = jnp.where(neg_key > thr, tail_f, 0.0)
    n_gt = count(lambda key, kpos: key > thr) + tail_gt
    n_eq = count(lambda key, kpos: key == thr) + jnp.where(neg_key == thr, tail_f, 0.0)
    need = kf - n_gt

    n_bits = int(np.ceil(np.log2(n_pad)))

    def tie_search():
        def pos_step(i, lo):
            cand = lo + lax.shift_left(jnp.int32(1), n_bits - 1 - i)
            in_tail = jnp.clip(cand - tail0, 0, n_tail).astype(F32)
            cnt = (count(lambda key, kpos: jnp.logical_and(key == thr, kpos < cand))
                   + jnp.where(neg_key == thr, in_tail, 0.0))
            return jnp.where(cnt < need, cand, lo)
        return lax.fori_loop(0, n_bits, pos_step, jnp.zeros((tq, 1), jnp.int32))

    has_ties = jnp.max(jnp.where(n_eq > need, 1.0, 0.0)) > 0.0
    last = lax.cond(has_ties, tie_search, lambda: jnp.full((tq, 1), n_pad, jnp.int32))

    bias_ref[0] = jnp.full(bias_ref.shape[1:], NEG_INF, bias_ref.dtype)

    def write_block(kb, carry):
        start, kpos = block_pos(kb)
        key = key_ref[:, pl.ds(start, tk)]
        sel = jnp.logical_or(key > thr, jnp.logical_and(key == thr, kpos <= last))
        ok = jnp.logical_and(jnp.logical_and(sel, kpos // CHUNK <= qchunk), kpos < n_real)
        bias_ref[0, :, pl.ds(start, tk)] = jnp.where(ok, 0.0, NEG_INF).astype(bias_ref.dtype)
        return carry

    lax.fori_loop(0, n_kb, write_block, 0)


def _dsa_index(qi2d, wi, kz, batch, n_q, n_k, n_real, tq, tk, qpos_base, qi_col):
    nt = n_q // tq
    topk = min(TOPK_MAX, n_real // 4)
    vmem = 2 * (tq * 2048 * 2 + n_k * 256 * 2 + tq * n_k * 2) + tq * n_k * 4 + 8 * tq * tk * 4 + (4 << 20)
    return pl.pallas_call(
        functools.partial(_dsa_index_body, tq=tq, tk=tk, n_real=n_real, qpos_base=qpos_base, topk=topk),
        grid=(batch, nt),
        in_specs=[pl.BlockSpec((tq, IDX_HEADS * IDX_DIM), lambda b, t: (b * nt + t, qi_col)),
                  pl.BlockSpec((tq, IDX_HEADS), lambda b, t: (b * nt + t, 0)),
                  pl.BlockSpec((n_k, 2 * LANES), lambda b, t: (b, 0))],
        out_specs=pl.BlockSpec((1, tq, n_k), lambda b, t: (b, t, 0)),
        out_shape=jax.ShapeDtypeStruct((batch, n_q, n_k), BF16),
        scratch_shapes=[pltpu.VMEM((tq, n_k), jnp.int32)],
        compiler_params=_cparams(vmem),
        name="dsa_index",
    )(qi2d, wi, kz)


def _dsa_attn_body(*refs, tq, tk, qpos_base, cached):
    if cached:
        q_ref, kc_ref, vc_ref, kn_ref, vn_ref, bias_ref, slope_ref, o_ref, s_ref, p_ref, k_ref, v_ref = refs
        past, n_kv = cached
        n_new = kn_ref.shape[0]
        own = pl.ds(pl.program_id(1), past, stride=n_kv)
        for full_ref, cache_ref, new_ref in ((k_ref, kc_ref, kn_ref), (v_ref, vc_ref, vn_ref)):
            full_ref[0:past, :] = cache_ref[own, :].astype(BF16)
            full_ref[past:past + n_new, :] = new_ref[...]
            full_ref[past + n_new:, :] = jnp.zeros((full_ref.shape[0] - past - n_new, HEAD_DIM), BF16)
    else:
        q_ref, k_ref, v_ref, bias_ref, slope_ref, o_ref, s_ref, p_ref = refs
    n_pad = k_ref.shape[0]
    qt = pl.program_id(2)
    qpos0 = qpos_base + qt * tq
    tile_end = qpos0 + tq
    rows = C_GROUP * tq
    rowpos = qpos0 + lax.broadcasted_iota(jnp.int32, (tq, 1), 0)
    n_kb = jnp.minimum((((qpos0 + tq - 1) // CHUNK + 1) * CHUNK + tk - 1) // tk, n_pad // tk)
    qs = jnp.concatenate([q_ref[:, g * HEAD_DIM:(g + 1) * HEAD_DIM] for g in range(C_GROUP)], axis=0)

    def key_rows(kb):
        return pl.ds(pl.multiple_of(kb * tk, tk), tk)

    def scores(kb):
        s_ref[kb % 2] = _nt_dot(qs, k_ref[key_rows(kb), :])

    def weighted_values(kb):
        return jnp.dot(p_ref[(kb + 2) % 2], v_ref[key_rows(jnp.maximum(kb, 0)), :], preferred_element_type=F32)

    slope = slope_ref[0]

    def softmax(kb, m, l, has_later_keys):
        kpos = kb * tk + lax.broadcasted_iota(jnp.int32, (1, tk), 1)
        back = (tile_end - kpos).astype(F32)
        s = s_ref[kb % 2] * ATTN_SCALE - slope * back
        extra = bias_ref[0, :, key_rows(kb)].astype(F32)
        s = (s.reshape(C_GROUP, tq, tk) + extra[None]).reshape(rows, tk)
        if has_later_keys:
            ahead = jnp.where(kpos > rowpos, (rowpos - kpos).astype(F32), 0.0)
            s = s + (2.0 * slope) * jnp.concatenate([ahead] * C_GROUP, axis=0)
        m_new = jnp.maximum(m, jnp.max(s, axis=-1, keepdims=True))
        p = jnp.exp(s - m_new)
        p_ref[kb % 2] = p.astype(BF16)
        a = jnp.exp(m - m_new)
        return m_new, a * l + jnp.sum(p, axis=-1, keepdims=True), a

    def trip(kb, state):
        m, l, a_prev, acc = state
        acc = a_prev * acc + weighted_values(kb - 1)
        m, l, a = softmax(kb, m, l, False)
        scores(kb + 1)
        return m, l, a, acc

    p_ref[1] = jnp.zeros(p_ref.shape[1:], BF16)
    scores(0)
    init = (jnp.full((rows, 1), 0.5 * NEG_INF, F32), jnp.zeros((rows, 1), F32),
            jnp.ones((rows, 1), F32), jnp.zeros((rows, HEAD_DIM), F32))
    last = n_kb - 1
    m, l, a_prev, acc = lax.fori_loop(0, last, trip, init)
    acc = a_prev * acc + weighted_values(last - 1)
    m, l, a = softmax(last, m, l, True)
    acc = a * acc + weighted_values(last)
    o = (acc / l).astype(o_ref.dtype)
    for g in range(C_GROUP):
        o_ref[:, g * HEAD_DIM:(g + 1) * HEAD_DIM] = o[g * tq:(g + 1) * tq]


def _dsa_attn(q2d, k2d, v2d, bias, batch, n_q, n_k, n_kv, tq, tk, qpos_base, k_col, v_col, cache=None):
    assert tk % tq == 0 and qpos_base % tq == 0 and n_k % tk == 0
    nt = n_q // tq
    gw = C_GROUP * HEAD_DIM
    n_heads = n_kv * C_GROUP
    slopes = 2.0 ** (-8.0 * np.arange(1, n_heads + 1) / n_heads)
    slope_rows = np.repeat(slopes.reshape(n_kv, C_GROUP), tq, axis=1)[..., None].astype(np.float32)
    rows = C_GROUP * tq
    vmem = (2 * (2 * n_k * HEAD_DIM * 2 + tq * n_k * 2 + 2 * tq * gw * 2 + rows * LANES * 4)
            + 2 * rows * tk * 6 + 8 * rows * tk * 4 + (4 << 20))
    scratch = [pltpu.VMEM((2, rows, tk), F32), pltpu.VMEM((2, rows, tk), BF16)]
    if cache is None:
        cached = None
        kv_specs = [pl.BlockSpec((n_k, HEAD_DIM), lambda b, n, t: (b, k_col + n)),
                    pl.BlockSpec((n_k, HEAD_DIM), lambda b, n, t: (b, v_col + n))]
        kv_args = [k2d, v2d]
    else:
        cache_k, cache_v, layer = cache
        past = cache_k.shape[2]
        assert nt == 1 and qpos_base == past and past + n_q <= n_k and cache_k.shape[3] == n_kv
        cached = (past, n_kv)
        flat = lambda c: c.reshape(c.shape[0], batch, past * n_kv, HEAD_DIM)
        cache_spec = pl.BlockSpec((None, None, past * n_kv, HEAD_DIM), lambda b, n, t: (layer, b, 0, 0))
        kv_specs = [cache_spec, cache_spec,
                    pl.BlockSpec((n_q, HEAD_DIM), lambda b, n, t: (b, k_col + n)),
                    pl.BlockSpec((n_q, HEAD_DIM), lambda b, n, t: (b, v_col + n))]
        kv_args = [flat(cache_k), flat(cache_v), k2d, v2d]
        scratch += [pltpu.VMEM((n_k, HEAD_DIM), BF16), pltpu.VMEM((n_k, HEAD_DIM), BF16)]
        vmem += 2 * 2 * past * n_kv * HEAD_DIM * 4
    return pl.pallas_call(
        functools.partial(_dsa_attn_body, tq=tq, tk=tk, qpos_base=qpos_base, cached=cached),
        grid=(batch, n_kv, nt),
        in_specs=[pl.BlockSpec((tq, gw), lambda b, n, t: (b * nt + t, n))] + kv_specs + [
            pl.BlockSpec((1, tq, n_k), lambda b, n, t: (b, t, 0)),
            pl.BlockSpec((1, rows, 1), lambda b, n, t: (n, 0, 0))],
        out_specs=pl.BlockSpec((tq, gw), lambda b, n, t: (b * nt + t, n)),
        out_shape=jax.ShapeDtypeStruct((batch * n_q, n_heads * HEAD_DIM), BF16),
        scratch_shapes=scratch,
        compiler_params=_cparams(vmem),
        name="dsa_attn",
    )(q2d, *kv_args, bias, jnp.asarray(slope_rows))


def _mem_attn_body(x_ref, wq_ref, mk_ref, mv_ref, wo_ref, g_ref, b_ref, o_ref, ob_ref, y_ref, *,
                   tm, rows_per_seq, n_mem):
    q = jnp.dot(x_ref[...].astype(BF16), wq_ref[...], preferred_element_type=F32).astype(BF16)
    n_seq = tm // rows_per_seq
    heads = [slice(h * HEAD_DIM, (h + 1) * HEAD_DIM) for h in range(MEM_HEADS)]
    s = jnp.concatenate(
        [_nt_dot(q[s_i * rows_per_seq:(s_i + 1) * rows_per_seq, hs],
                 mk_ref[s_i * n_mem:(s_i + 1) * n_mem, hs].astype(BF16))
         for hs in heads for s_i in range(n_seq)], axis=0) * ATTN_SCALE
    m = jnp.max(s, axis=-1, keepdims=True)
    p = jnp.exp(s - m)
    l = jnp.sum(p, axis=-1, keepdims=True)
    p = p.astype(BF16)
    o_heads = []
    for h, hs in enumerate(heads):
        pieces = []
        for s_i in range(n_seq):
            rows = slice(h * tm + s_i * rows_per_seq, h * tm + (s_i + 1) * rows_per_seq)
            o = jnp.dot(p[rows], mv_ref[s_i * n_mem:(s_i + 1) * n_mem, hs].astype(BF16),
                        preferred_element_type=F32)
            pieces.append((o / l[rows]).astype(BF16))
        o_heads.append(pieces[0] if n_seq == 1 else jnp.concatenate(pieces, axis=0))
    y_ref[...] = jnp.dot(jnp.concatenate(o_heads, axis=-1), wo_ref[...], preferred_element_type=F32)
    _residual_ln_rows(y_ref, x_ref, g_ref, b_ref, o_ref, ob_ref, tm)


def _mem_attn_ln(x, wq, wo, l, mk, mv, g, b, tm, rows_per_seq, seq_rows):
    m, d = x.shape
    mw = wq.shape[-1]
    n_seq_tile = tm // rows_per_seq
    n_mem = mk.shape[0] // (m // seq_rows)
    if n_seq_tile == 1:
        mem_map = lambda i: ((i * tm) // seq_rows, 0)
    else:
        assert rows_per_seq == seq_rows
        mem_map = lambda i: (i, 0)
    vmem = 2 * (tm * d * 4 * 2 + tm * d * 2 + 2 * d * mw * 2 + 2 * n_seq_tile * n_mem * mw * 4) + tm * d * 4
    out_specs = [pl.BlockSpec((tm, d), lambda i: (i, 0)), pl.BlockSpec((tm, d), lambda i: (i, 0))]
    out_shape = [jax.ShapeDtypeStruct((m, d), F32), jax.ShapeDtypeStruct((m, d), BF16)]
    return pl.pallas_call(
        functools.partial(_mem_attn_body, tm=tm, rows_per_seq=rows_per_seq, n_mem=n_mem),
        grid=(m // tm,),
        in_specs=[pl.BlockSpec((tm, d), lambda i: (i, 0)),
                  _w_spec(wq, l, (d, mw), lambda i: (0, 0)),
                  pl.BlockSpec((n_seq_tile * n_mem, mw), mem_map),
                  pl.BlockSpec((n_seq_tile * n_mem, mw), mem_map),
                  _w_spec(wo, l, (mw, d), lambda i: (0, 0)),
                  pl.BlockSpec((1, d), lambda i: (0, 0)),
                  pl.BlockSpec((1, d), lambda i: (0, 0))],
        out_specs=out_specs,
        out_shape=out_shape,
        scratch_shapes=[pltpu.VMEM((tm, d), F32)],
        compiler_params=_cparams(vmem),
        name="mem_attn_ln",
    )(x, wq, mk, mv, wo, g.reshape(1, d), b.reshape(1, d))


FFN_TN = 256
FFN_SUB = 256
FFN_TM = 2048
HALO = 8


def _gelu(x):
    return 0.5 * x * (1.0 + lax.erf(x * np.float32(2.0 ** -0.5)))


def _ffn_up_body(x_ref, wa_ref, wg_ref, cwa_ref, cwg_ref, cba_ref, cbg_ref, *rest,
                 tm, rows_per_seq, tiles_per_seq, carried, nj, emit_w):
    if carried:
        act_ref, sta_ref, stg_ref, carry_ref, ext_ref = rest
    elif emit_w:
        pa_ref, pg_ref, act_ref, sta_ref, stg_ref, wab_ref, wgb_ref, ext_ref = rest
    else:
        pa_ref, pg_ref, act_ref, sta_ref, stg_ref, ext_ref = rest
    i = pl.program_id(0)
    j = pl.program_id(1)

    def conv_gate(base, rows):
        cw = jnp.concatenate([cwa_ref[...], cwg_ref[...]], axis=1)
        cb = jnp.concatenate([cba_ref[...], cbg_ref[...]], axis=1)
        h = (ext_ref[base:base + rows, :] * cw[2:3, :]
             + ext_ref[base - 1:base - 1 + rows, :] * cw[1:2, :]
             + ext_ref[base - 2:base - 2 + rows, :] * cw[0:1, :]
             + cb)
        return (h[:, :FFN_TN] * _gelu(h[:, FFN_TN:])).astype(act_ref.dtype)

    def put_state(s_i, last2):
        sta_ref[s_i] = last2[:, :FFN_TN]
        stg_ref[s_i] = last2[:, FFN_TN:]

    @pl.when(j < nj)
    def _():
        if carried:
            @pl.when(i % tiles_per_seq == 0)
            def _():
                carry_ref[j] = jnp.zeros((2, 2 * FFN_TN), F32)

            ext_ref[HALO - 2:HALO, :] = carry_ref[j]
            sub = min(FFN_SUB, tm)
            for r in range(0, tm, sub):
                ext_ref[HALO + r:HALO + r + sub, :FFN_TN] = jnp.dot(x_ref[r:r + sub, :], wa_ref[...],
                                                                    preferred_element_type=F32)
                ext_ref[HALO + r:HALO + r + sub, FFN_TN:] = jnp.dot(x_ref[r:r + sub, :], wg_ref[...],
                                                                    preferred_element_type=F32)
                act_ref[r:r + sub, :] = conv_gate(HALO + r, sub)
            last2 = ext_ref[HALO + tm - 2:HALO + tm, :]
            carry_ref[j] = last2
            put_state(0, last2)
        else:
            wa, wg = wa_ref, wg_ref
            if emit_w:
                wab_ref[...] = wa_ref[...].astype(BF16)
                wgb_ref[...] = wg_ref[...].astype(BF16)
                wa, wg = wab_ref, wgb_ref
            up = jnp.concatenate([jnp.dot(x_ref[...], wa[...], preferred_element_type=F32),
                                  jnp.dot(x_ref[...], wg[...], preferred_element_type=F32)], axis=1)
            for s_i in range(tm // rows_per_seq):
                base = s_i * (rows_per_seq + HALO) + HALO
                rsl = slice(s_i * rows_per_seq, (s_i + 1) * rows_per_seq)
                ext_ref[base - 2:base, :] = jnp.concatenate([pa_ref[s_i], pg_ref[s_i]], axis=1)
                ext_ref[base:base + rows_per_seq, :] = up[rsl]
                put_state(s_i, ext_ref[base + rows_per_seq - 2:base + rows_per_seq, :])
                act_ref[rsl, :] = conv_gate(base, rows_per_seq)

    @pl.when(j >= nj)
    def _():
        act_ref[...] = jnp.zeros_like(act_ref)


def _ffn_up(x, w_val, w_gate, l, conv_w, conv_b, prev, tm, rows_per_seq, seq_rows, n_act_cols, emit_w=False):
    m, d = x.shape
    f2 = conv_w.shape[1]
    wt = 2 * FFN_TN
    assert f2 % wt == 0 and n_act_cols % FFN_TN == 0
    nj = f2 // wt
    nj_pad = n_act_cols // FFN_TN
    carried = prev is None
    assert not (emit_w and carried)
    n_seq = tm // rows_per_seq
    tiles_per_seq = max(seq_rows // tm, 1)
    n_tiles = m // tm
    val = lambda j: jnp.minimum(j, nj - 1)
    gate0 = 0 if l is None else nj
    in_specs = [pl.BlockSpec((tm, d), lambda i, j: (i, 0)),
                _w_spec(w_val, l, (d, FFN_TN), lambda i, j: (0, val(j))),
                _w_spec(w_gate, l, (d, FFN_TN), lambda i, j: (0, val(j) + gate0)),
                pl.BlockSpec((CONV_W, FFN_TN), lambda i, j: (0, val(j))),
                pl.BlockSpec((CONV_W, FFN_TN), lambda i, j: (0, val(j) + nj)),
                pl.BlockSpec((1, FFN_TN), lambda i, j: (0, val(j))),
                pl.BlockSpec((1, FFN_TN), lambda i, j: (0, val(j) + nj))]
    args = [x, w_val, w_gate, conv_w, conv_w, conv_b, conv_b]
    scratch = []
    if carried:
        assert n_seq == 1
        scratch.append(pltpu.VMEM((nj, 2, wt), F32))
    else:
        in_specs += [pl.BlockSpec((n_seq, 2, FFN_TN), lambda i, j: (i, 0, val(j))),
                     pl.BlockSpec((n_seq, 2, FFN_TN), lambda i, j: (i, 0, val(j) + nj))]
        args += [prev, prev]
    scratch.append(pltpu.VMEM((n_seq * (rows_per_seq + HALO), wt), F32))
    st_shape = jax.ShapeDtypeStruct((n_tiles * n_seq, 2, f2 // 2), F32)
    st_spec = pl.BlockSpec((n_seq, 2, FFN_TN), lambda i, j: (i, 0, val(j)))
    vmem = (2 * (tm * d * 2 + d * wt * w_val.dtype.itemsize + tm * FFN_TN * 2) + tm * wt * 4
            + 8 * min(tm, FFN_SUB) * wt * 4 + (4 << 20))
    out_specs = [pl.BlockSpec((tm, FFN_TN), lambda i, j: (i, j)), st_spec, st_spec]
    out_shape = [jax.ShapeDtypeStruct((m, n_act_cols), BF16), st_shape, st_shape]
    if emit_w:
        vmem += 2 * d * wt * 2
        out_specs += [pl.BlockSpec((d, FFN_TN), lambda i, j: (0, val(j)))] * 2
        out_shape += [jax.ShapeDtypeStruct((d, f2 // 2), BF16)] * 2
    act, st_a, st_g, *w_images = pl.pallas_call(
        functools.partial(_ffn_up_body, tm=tm, rows_per_seq=rows_per_seq, tiles_per_seq=tiles_per_seq,
                          carried=carried, nj=nj, emit_w=emit_w),
        grid=(n_tiles, nj_pad),
        in_specs=in_specs,
        out_specs=out_specs,
        out_shape=out_shape,
        scratch_shapes=scratch,
        compiler_params=_cparams(vmem),
        name="ffn_up_conv_gate",
    )(*args)
    return (act, jnp.concatenate([st_a, st_g], axis=-1), *w_images)


def _pad_rows(a, n_rows):
    return jnp.pad(a, ((0, 0), (0, n_rows - a.shape[1]), (0, 0)))


def kernel(x_prompt, x_sample, cache_a_k, cache_a_v, cache_b_k, cache_b_v, cache_c_k, cache_c_v, cache_c_idx, cache_mem_k, cache_mem_v, state_ffn_conv, mem_prompt, w_in_even, w_out_even, rel_bias_a, w_in_odd, w_out_odd, w_mem_q, w_mem_kv, w_mem_o, w_up, conv_w, conv_b, w_down, ln_g, ln_b):
    bp, sp, d = x_prompt.shape
    bs, n_new, _ = x_sample.shape
    depth = w_up.shape[0]
    h_a = cache_a_k.shape[3]
    h_b = cache_b_k.shape[3]
    n_kv = cache_c_k.shape[3]
    h_c = n_kv * C_GROUP
    hw_a, hw_b, kvw = h_a * HEAD_DIM, h_b * HEAD_DIM, n_kv * HEAD_DIM
    qiw = IDX_HEADS * IDX_DIM
    a_win_s = cache_a_k.shape[2]
    a_win_p = min(A_LEFT_CHUNKS * CHUNK, sp)
    past = cache_b_k.shape[2]
    n_mem = mem_prompt.shape[1]
    mem_w = w_mem_q.shape[2]
    d_ff = w_down.shape[1]
    mp, ms = bp * sp, bs * n_new
    l_s = past + n_new
    l_s_pad = -(-l_s // LANES) * LANES
    main_odd = h_c * HEAD_DIM + 2 * kvw + qiw
    tail_pad = LANES
    ff_pad = -(-d_ff // 1024) * 1024

    wmq = w_mem_q.astype(BF16)
    wmo = w_mem_o.astype(BF16)

    xp = x_prompt.reshape(mp, d)
    xs = x_sample.reshape(ms, d)
    xp_b, xs_b = xp.astype(BF16), xs.astype(BF16)
    mem2d = mem_prompt.reshape(bp * n_mem, d).astype(BF16)
    cmk = cache_mem_k.reshape(depth, bs * n_mem, mem_w).astype(BF16)
    cmv = cache_mem_v.reshape(depth, bs * n_mem, mem_w).astype(BF16)

    outs = {k: [] for k in ("akp", "avp", "bkp", "bvp", "ckp", "cvp", "cip", "mkp", "mvp", "fp",
                            "aks", "avs", "bks", "bvs", "cks", "cvs", "cis", "fs")}

    for l in range(depth):
        i = l // 2
        if l % 2 == 0:
            w_out = w_out_even
            sf, sb, w_in_l = _matmul(xs_b, w_in_even, i, (F32, BF16), 256, 512, emit_w=True)
            pf, pb = _matmul(xp_b, w_in_l, None, (F32, BF16), 1024, 512)
            oa = _band_attn_prompt(pb, rel_bias_a[i], bp, sp, h_a, 0, h_a, 2 * h_a)
            cb0 = 3 * h_a
            ob = _stick_break_attn(pb, pb, pb, bp, sp, sp, h_b, 128, 0, cb0, cb0 + h_b, cb0 + 2 * h_b)
            mix_p = jnp.concatenate([oa, ob], axis=-1)
            pf4 = pf.reshape(bp, sp, -1)
            outs["akp"].append(pf4[:, sp - a_win_p:, hw_a:2 * hw_a].reshape(bp, a_win_p, h_a, HEAD_DIM))
            outs["avp"].append(pf4[:, sp - a_win_p:, 2 * hw_a:3 * hw_a].reshape(bp, a_win_p, h_a, HEAD_DIM))
            o0 = 3 * hw_a
            outs["bkp"].append(pf4[:, :, o0 + hw_b:o0 + 2 * hw_b].reshape(bp, sp, h_b, HEAD_DIM))
            outs["bvp"].append(pf4[:, :, o0 + 2 * hw_b:o0 + 3 * hw_b].reshape(bp, sp, h_b, HEAD_DIM))
            sf4 = sf.reshape(bs, n_new, -1)
            oa_s = _band_attn_cached(sb, cache_a_k, cache_a_v, i, sb, rel_bias_a[i], bs, n_new, 1, 2)
            kb_new = sf4[:, :, o0 + hw_b:o0 + 2 * hw_b]
            vb_new = sf4[:, :, o0 + 2 * hw_b:o0 + 3 * hw_b]
            ob_s = _stick_break_cached(sb, cache_b_k, cache_b_v, i, sb, bs, n_new, cb0, cb0 + h_b, cb0 + 2 * h_b)
            mix_s = jnp.concatenate([oa_s, ob_s], axis=-1)
            outs["aks"].append(sf4[:, :, hw_a:2 * hw_a].reshape(bs, n_new, h_a, HEAD_DIM))
            outs["avs"].append(sf4[:, :, 2 * hw_a:3 * hw_a].reshape(bs, n_new, h_a, HEAD_DIM))
            outs["bks"].append(kb_new.reshape(bs, n_new, h_b, HEAD_DIM))
            outs["bvs"].append(vb_new.reshape(bs, n_new, h_b, HEAD_DIM))
        else:
            w_out = w_out_odd
            sf, sb, w_main_l = _matmul(xs_b, w_in_odd, i, (F32, BF16), 256, 512, n_cols=main_odd, emit_w=True)
            stl, w_tail_l = _matmul(xs_b, w_in_odd, i, (F32,), 256, tail_pad, col0=main_odd, n_cols=tail_pad,
                                    emit_w=True)
            kc0 = h_c
            vc0 = h_c + n_kv
            qic = (h_c * HEAD_DIM + 2 * kvw) // qiw

            def kz_of(ki):
                z = jnp.zeros_like(ki)
                return jnp.concatenate([ki, z, z, ki], axis=-1).astype(BF16).reshape(-1, 2 * LANES)

            pf, pb = _matmul(xp_b, w_main_l, None, (F32, BF16), 1024, 512)
            (pt,) = _matmul(xp_b, w_tail_l, None, (F32,), 1024, tail_pad)
            ki_p = pt[:, :IDX_DIM].reshape(bp, sp, IDX_DIM)
            wi_p = pt[:, IDX_DIM:IDX_DIM + IDX_HEADS]
            sel_p = _dsa_index(pb, wi_p, kz_of(ki_p), bp, sp, sp, sp, 128, 512, 0, qic)
            mix_p = _dsa_attn(pb, pb, pb, sel_p, bp, sp, sp, n_kv, 256, 512, 0, kc0, vc0)
            pf4 = pf.reshape(bp, sp, -1)
            q_w = h_c * HEAD_DIM
            outs["ckp"].append(pf4[:, :, q_w:q_w + kvw].reshape(bp, sp, n_kv, HEAD_DIM))
            outs["cvp"].append(pf4[:, :, q_w + kvw:q_w + 2 * kvw].reshape(bp, sp, n_kv, HEAD_DIM))
            outs["cip"].append(ki_p)
            sf4 = sf.reshape(bs, n_new, -1)
            k_new = sf4[:, :, q_w:q_w + kvw]
            v_new = sf4[:, :, q_w + kvw:q_w + 2 * kvw]
            ki_new = stl[:, :IDX_DIM].reshape(bs, n_new, IDX_DIM)
            wi_s = stl[:, IDX_DIM:IDX_DIM + IDX_HEADS]
            ki_all = _pad_rows(jnp.concatenate([cache_c_idx[i], ki_new], axis=1), l_s_pad)
            sel_s = _dsa_index(sb, wi_s, kz_of(ki_all), bs, n_new, l_s_pad, l_s, n_new, l_s_pad, past, qic)
            mix_s = _dsa_attn(sb, sb, sb, sel_s, bs, n_new, l_s_pad, n_kv, n_new, l_s_pad, past, kc0, vc0,
                              cache=(cache_c_k, cache_c_v, i))
            outs["cks"].append(k_new.reshape(bs, n_new, n_kv, HEAD_DIM))
            outs["cvs"].append(v_new.reshape(bs, n_new, n_kv, HEAD_DIM))
            outs["cis"].append(ki_new)

        xs, xs_b, w_out_l = _matmul_residual_ln(mix_s, w_out, i, xs, ln_g[l, 0], ln_b[l, 0], 256, 512, emit_w=True)
        xp, xp_b = _matmul_residual_ln(mix_p, w_out_l, None, xp, ln_g[l, 0], ln_b[l, 0], 512, 1024)

        (mkv,) = _matmul(mem2d, w_mem_kv, l, (F32,), 512, 512)
        mk_p, mv_p = mkv[:, :mem_w], mkv[:, mem_w:]
        outs["mkp"].append(mk_p.reshape(bp, n_mem, MEM_HEADS, HEAD_DIM))
        outs["mvp"].append(mv_p.reshape(bp, n_mem, MEM_HEADS, HEAD_DIM))
        xp, xp_b = _mem_attn_ln(xp, wmq, wmo, l, mk_p, mv_p, ln_g[l, 1], ln_b[l, 1], 256, 256, sp)
        xs, xs_b = _mem_attn_ln(xs, wmq, wmo, l, cmk[l], cmv[l], ln_g[l, 1], ln_b[l, 1],
                                min(ms, 128), n_new, n_new)

        cb_l = conv_b[l][None]
        ffn_tm = min(FFN_TM, sp)
        act_s, st_s, w_val_l, w_gate_l = _ffn_up(xs_b, w_up, w_up, l, conv_w[l], cb_l, state_ffn_conv[l],
                                                 ms, n_new, n_new, ff_pad, emit_w=True)
        act_p, st_p = _ffn_up(xp_b, w_val_l, w_gate_l, None, conv_w[l], cb_l, None, ffn_tm, ffn_tm, sp, ff_pad)
        tiles_per_seq = sp // ffn_tm
        outs["fp"].append(st_p[tiles_per_seq - 1::tiles_per_seq])
        outs["fs"].append(st_s)
        xs, xs_b, w_down_l = _matmul_residual_ln(act_s, w_down, l, xs, ln_g[l, 2], ln_b[l, 2], 256, 512, emit_w=True)
        xp, xp_b = _matmul_residual_ln(act_p, w_down_l, None, xp, ln_g[l, 2], ln_b[l, 2], 512, 1024)

    st = jnp.stack
    o = outs
    assert n_new <= a_win_s
    rolled_k = jnp.concatenate([cache_a_k[:, :, n_new:], st(o["aks"])], axis=2)
    rolled_v = jnp.concatenate([cache_a_v[:, :, n_new:], st(o["avs"])], axis=2)
    return (xp.reshape(bp, sp, d), xs.reshape(bs, n_new, d),
            st(o["akp"]), st(o["avp"]), st(o["bkp"]), st(o["bvp"]), st(o["ckp"]), st(o["cvp"]), st(o["cip"]),
            st(o["mkp"]), st(o["mvp"]), st(o["fp"]),
            rolled_k, rolled_v, st(o["bks"]), st(o["bvs"]), st(o["cks"]), st(o["cvs"]), st(o["cis"]),
            st(o["fs"]))
```

```python
import functools

import numpy as np
import jax
import jax.numpy as jnp
from jax import lax
from jax.experimental import pallas as pl
from jax.experimental.pallas import tpu as pltpu

BF16 = jnp.bfloat16
F32 = jnp.float32

CHUNK = 64
HEAD_DIM = 128
A_LEFT_CHUNKS = 8
REL_CLIP = 128
C_GROUP = 4
IDX_HEADS = 32
IDX_DIM = 64
TOPK_MAX = 256
MEM_HEADS = 4
CONV_W = 3
DEPTH = 4
ALPHA = (2.0 * DEPTH) ** 0.25
LN_EPS = 1e-5
ATTN_SCALE = HEAD_DIM ** -0.5
IDX_SCALE = IDX_DIM ** -0.5
IDX_W_SCALE = IDX_HEADS ** -0.5
NEG_INF = -1e30

V7X_VMEM_BYTES = 64 * 1024 * 1024
LANES = 128
SUBLANES = 8
SB_EXIT = 104.0
INT_MIN = -(2 ** 31)
NEG_KEY = int(np.array(NEG_INF, np.float32).view(np.int32)) ^ 0x7FFFFFFF


def _cparams(vmem_bytes):
    limit = int(min(max(vmem_bytes * 1.25 + (4 << 20), 32 << 20), V7X_VMEM_BYTES - (6 << 20)))
    return pltpu.CompilerParams(vmem_limit_bytes=limit)


def _nt_dot(a, b):
    return lax.dot_general(a, b, (((1,), (1,)), ((), ())), preferred_element_type=F32)


def _w_spec(w, l, block, index):
    if l is None:
        return pl.BlockSpec(block, index)
    return pl.BlockSpec((None,) + block, lambda *grid: (l,) + tuple(index(*grid)))


def _mm_body(x_ref, w_ref, *out_refs, emit_w):
    if emit_w:
        out_refs[-1][...] = w_ref[...].astype(BF16)
        w_ref, out_refs = out_refs[-1], out_refs[:-1]
    r = jnp.dot(x_ref[...].astype(BF16), w_ref[...].astype(BF16), preferred_element_type=F32)
    for o in out_refs:
        o[...] = r.astype(o.dtype)


def _matmul(x, w, l, out_dtypes, tm, tn, col0=0, n_cols=None, emit_w=False):
    m, kd = x.shape
    n = n_cols or w.shape[-1]
    tm, tn = min(tm, m), min(tn, n)
    assert m % tm == 0 and n % tn == 0 and col0 % tn == 0, (x.shape, w.shape, tm, tn, col0)
    assert not emit_w or m == tm
    j0 = col0 // tn
    wsz = w.dtype.itemsize
    vmem = 2 * (tm * kd * x.dtype.itemsize + kd * tn * wsz) + kd * tn * 2 * (3 if emit_w else 1) + 2 * tm * tn * 4
    vmem += sum(2 * tm * tn * jnp.dtype(d).itemsize for d in out_dtypes)
    out_specs = [pl.BlockSpec((tm, tn), lambda i, j: (i, j)) for _ in out_dtypes]
    out_shape = [jax.ShapeDtypeStruct((m, n), d) for d in out_dtypes]
    if emit_w:
        out_specs.append(pl.BlockSpec((kd, tn), lambda i, j: (0, j)))
        out_shape.append(jax.ShapeDtypeStruct((kd, n), BF16))
    return pl.pallas_call(
        functools.partial(_mm_body, emit_w=emit_w),
        grid=(m // tm, n // tn),
        in_specs=[pl.BlockSpec((tm, kd), lambda i, j: (i, 0)),
                  _w_spec(w, l, (kd, tn), lambda i, j: (0, j0 + j))],
        out_specs=out_specs,
        out_shape=out_shape,
        compiler_params=_cparams(vmem),
        name="matmul",
    )(x, w)


LN_ROWS = 128


def _residual_ln_rows(y_ref, res_ref, g_ref, b_ref, o_ref, ob_ref, rows):
    g = g_ref[...]
    b = b_ref[...]

    def chunk(c, carry):
        sl = pl.ds(pl.multiple_of(c * LN_ROWS, LN_ROWS), LN_ROWS)
        y = ALPHA * res_ref[sl, :] + y_ref[sl, :]
        mu = jnp.mean(y, axis=-1, keepdims=True)
        d = y - mu
        var = jnp.mean(d * d, axis=-1, keepdims=True)
        out = d * lax.rsqrt(var + LN_EPS) * g + b
        o_ref[sl, :] = out
        ob_ref[sl, :] = out.astype(BF16)
        return carry

    lax.fori_loop(0, rows // LN_ROWS, chunk, 0)


MM_LN_CHUNK = 512


def _mm_ln_body(x_ref, w_ref, res_ref, g_ref, b_ref, o_ref, ob_ref, *rest, n_k, tm, emit_w, w_rows):
    k = pl.program_id(1)
    x = x_ref[...]
    n = o_ref.shape[1]
    chunks = [slice(c, c + MM_LN_CHUNK) for c in range(0, n, MM_LN_CHUNK)]
    if emit_w:
        tk = w_ref.shape[0]
        row = k * tk + lax.broadcasted_iota(jnp.int32, (tk, 1), 0)
        (wb_ref,) = rest
        wb_ref[...] = jnp.where(row < w_rows, w_ref[...], 0.0).astype(BF16)
        w_ref = wb_ref

    @pl.when(k == 0)
    def _():
        for sl in chunks:
            o_ref[:, sl] = jnp.dot(x, w_ref[:, sl], preferred_element_type=F32)

    @pl.when(k > 0)
    def _():
        for sl in chunks:
            o_ref[:, sl] += jnp.dot(x, w_ref[:, sl], preferred_element_type=F32)

    @pl.when(k == n_k - 1)
    def _():
        _residual_ln_rows(o_ref, res_ref, g_ref, b_ref, o_ref, ob_ref, tm)


def _matmul_residual_ln(x, w, l, res, g, b, tm, tk, emit_w=False):
    m, kd = x.shape
    w_rows, n = w.shape[-2:]
    tm, tk = min(tm, m), min(tk, kd)
    assert m % tm == 0 and kd % tk == 0 and tm % LN_ROWS == 0 and n % MM_LN_CHUNK == 0
    assert (emit_w and m == tm) or w_rows == kd
    n_k = kd // tk
    vmem = 2 * (tm * tk * 2 + tk * n * w.dtype.itemsize + tm * n * 4 + tm * n * 4 + tm * n * 2)
    out_specs = [pl.BlockSpec((tm, n), lambda i, k: (i, 0)), pl.BlockSpec((tm, n), lambda i, k: (i, 0))]
    out_shape = [jax.ShapeDtypeStruct((m, n), F32), jax.ShapeDtypeStruct((m, n), BF16)]
    if emit_w:
        vmem += 3 * tk * n * 2
        out_specs.append(pl.BlockSpec((tk, n), lambda i, k: (k, 0)))
        out_shape.append(jax.ShapeDtypeStruct((kd, n), BF16))
    return pl.pallas_call(
        functools.partial(_mm_ln_body, n_k=n_k, tm=tm, emit_w=emit_w, w_rows=w_rows),
        grid=(m // tm, n_k),
        in_specs=[pl.BlockSpec((tm, tk), lambda i, k: (i, k)),
                  _w_spec(w, l, (tk, n), lambda i, k: (k, 0)),
                  pl.BlockSpec((tm, n), lambda i, k: (i, 0)),
                  pl.BlockSpec((1, n), lambda i, k: (0, 0)),
                  pl.BlockSpec((1, n), lambda i, k: (0, 0))],
        out_specs=out_specs,
        out_shape=out_shape,
        compiler_params=_cparams(vmem),
        name="matmul_residual_ln",
    )(x, w, res, g.reshape(1, n), b.reshape(1, n))


BAND_TQ = 128
BAND_KB = 5
ATTN_HB = 4


def _band_prompt_body(q_ref, k_ref, v_ref, bias_ref, o_ref):
    t = pl.program_id(2)
    heads = [slice(h * HEAD_DIM, (h + 1) * HEAD_DIM) for h in range(ATTN_HB)]
    first = t - (BAND_KB - 1)
    keys = [pl.ds(pl.multiple_of(jnp.maximum(first + jb, 0) * BAND_TQ, BAND_TQ), BAND_TQ) for jb in range(BAND_KB)]
    s = jnp.concatenate(
        [jnp.concatenate([_nt_dot(q_ref[:, hs], k_ref[keys[jb], hs]) for hs in heads], axis=0)
         for jb in range(BAND_KB)], axis=1)
    s = s * ATTN_SCALE + bias_ref[...].reshape(ATTN_HB * BAND_TQ, BAND_KB * BAND_TQ)
    col_block = lax.broadcasted_iota(jnp.int32, (1, BAND_KB * BAND_TQ), 1) // BAND_TQ
    s = jnp.where(first + col_block >= 0, s, NEG_INF)
    m = jnp.max(s, axis=-1, keepdims=True)
    p = jnp.exp(s - m)
    l = jnp.sum(p, axis=-1, keepdims=True)
    p = p.astype(BF16)
    for h, hs in enumerate(heads):
        rows = slice(h * BAND_TQ, (h + 1) * BAND_TQ)
        acc = sum(jnp.dot(p[rows, jb * BAND_TQ:(jb + 1) * BAND_TQ], v_ref[keys[jb], hs],
                          preferred_element_type=F32) for jb in range(BAND_KB))
        o_ref[:, hs] = (acc / l[rows]).astype(o_ref.dtype)


def _toeplitz_bias(rel_bias, n_rows, n_cols, offset):
    m = np.arange(n_rows + n_cols - 1) - (n_rows - 1)
    diag = rel_bias[:, np.clip(offset - m, -REL_CLIP, REL_CLIP) + REL_CLIP]
    rows = [diag[:, n_rows - 1 - i:n_rows - 1 - i + n_cols] for i in range(n_rows)]
    return jnp.stack(rows, axis=1).astype(F32)


def _band_attn_prompt(qkv, rel_bias, batch, seq, n_heads, q_col, k_col, v_col):
    assert n_heads % ATTN_HB == 0 and q_col % ATTN_HB == 0 and k_col % ATTN_HB == 0 and v_col % ATTN_HB == 0
    nt = seq // BAND_TQ
    hw = ATTN_HB * HEAD_DIM
    bias = _toeplitz_bias(rel_bias, BAND_TQ, BAND_KB * BAND_TQ, A_LEFT_CHUNKS * CHUNK)
    qi = np.arange(BAND_TQ)[:, None] // CHUNK
    kj = np.arange(BAND_KB * BAND_TQ)[None, :] // CHUNK
    in_band = (kj >= qi) & (kj <= qi + A_LEFT_CHUNKS)
    bias = jnp.where(jnp.asarray(in_band)[None], bias, NEG_INF)
    vmem = 2 * (2 * seq * hw * 2 + ATTN_HB * BAND_TQ * BAND_KB * BAND_TQ * 4) + (8 << 20)
    return pl.pallas_call(
        _band_prompt_body,
        grid=(batch, n_heads // ATTN_HB, nt),
        in_specs=[pl.BlockSpec((BAND_TQ, hw), lambda b, h, t: (b * nt + t, q_col // ATTN_HB + h)),
                  pl.BlockSpec((seq, hw), lambda b, h, t: (b, k_col // ATTN_HB + h)),
                  pl.BlockSpec((seq, hw), lambda b, h, t: (b, v_col // ATTN_HB + h)),
                  pl.BlockSpec((ATTN_HB, BAND_TQ, BAND_KB * BAND_TQ), lambda b, h, t: (h, 0, 0))],
        out_specs=pl.BlockSpec((BAND_TQ, hw), lambda b, h, t: (b * nt + t, h)),
        out_shape=jax.ShapeDtypeStruct((batch * seq, n_heads * HEAD_DIM), BF16),
        compiler_params=_cparams(vmem),
        name="band_attn_prompt",
    )(qkv, qkv, qkv, bias)


def _band_cached_body(q_ref, kc_ref, vc_ref, kn_ref, vn_ref, bias_ref, o_ref, *, n_heads, window):
    for h in range(n_heads):
        hs = slice(h * HEAD_DIM, (h + 1) * HEAD_DIM)
        cached = pl.ds(h, window, stride=n_heads)
        k = jnp.concatenate([kc_ref[cached, :].astype(BF16), kn_ref[:, hs]], axis=0)
        v = jnp.concatenate([vc_ref[cached, :].astype(BF16), vn_ref[:, hs]], axis=0)
        s = _nt_dot(q_ref[:, hs], k) * ATTN_SCALE + bias_ref[h]
        m = jnp.max(s, axis=-1, keepdims=True)
        p = jnp.exp(s - m)
        l = jnp.sum(p, axis=-1, keepdims=True)
        acc = jnp.dot(p.astype(BF16), v, preferred_element_type=F32)
        o_ref[:, hs] = (acc / l).astype(o_ref.dtype)


def _band_attn_cached(q2d, cache_k, cache_v, layer, new2d, rel_bias, batch, n_new, k_col, v_col):
    window, n_heads = cache_k.shape[2:4]
    hw = n_heads * HEAD_DIM
    n_keys = window + n_new
    bias = _toeplitz_bias(rel_bias, n_new, n_keys, window)
    flat = lambda c: c.reshape(c.shape[0], batch, window * n_heads, HEAD_DIM)
    cache_spec = pl.BlockSpec((None, None, window * n_heads, HEAD_DIM), lambda b: (layer, b, 0, 0))
    vmem = 2 * (2 * window * hw * 4 + n_heads * n_new * n_keys * 4) + (8 << 20)
    return pl.pallas_call(
        functools.partial(_band_cached_body, n_heads=n_heads, window=window),
        grid=(batch,),
        in_specs=[pl.BlockSpec((n_new, hw), lambda b: (b, 0)),
                  cache_spec, cache_spec,
                  pl.BlockSpec((n_new, hw), lambda b: (b, k_col)),
                  pl.BlockSpec((n_new, hw), lambda b: (b, v_col)),
                  pl.BlockSpec((n_heads, n_new, n_keys), lambda b: (0, 0, 0))],
        out_specs=pl.BlockSpec((n_new, hw), lambda b: (b, 0)),
        out_shape=jax.ShapeDtypeStruct((batch * n_new, hw), BF16),
        compiler_params=_cparams(vmem),
        name="band_attn_cached",
    )(q2d, flat(cache_k), flat(cache_v), new2d, new2d, bias)


def _roll_body(c_ref, new_ref, o_ref):
    keep = o_ref.shape[0] - new_ref.shape[0]
    o_ref[0:keep, :] = c_ref[new_ref.shape[0]:, :]
    o_ref[keep:, :] = new_ref[...]


def _rolled_band(cache, new_rows):
    n_layers, batch, window, n_heads, hd = cache.shape
    n_new = new_rows.shape[2]
    assert n_new <= window and (n_new * n_heads) % SUBLANES == 0
    spec = lambda rows: pl.BlockSpec((None, None, rows * n_heads, hd), lambda l, b: (l, b, 0, 0))
    out = pl.pallas_call(
        _roll_body,
        grid=(n_layers, batch),
        in_specs=[spec(window), spec(n_new)],
        out_specs=spec(window),
        out_shape=jax.ShapeDtypeStruct((n_layers, batch, window * n_heads, hd), cache.dtype),
        name="rolled_band",
    )(cache.reshape(n_layers, batch, window * n_heads, hd), new_rows.reshape(n_layers, batch, n_new * n_heads, hd))
    return out.reshape(cache.shape)


SB_TK = 256


def _split2(x):
    hi = x.astype(BF16)
    return hi, (x - hi.astype(F32)).astype(BF16)


def _sb_window(qs, ks, vs, u, mask, carry, acc, tq):
    z = jnp.concatenate([_nt_dot(q, k) for q, k in zip(qs, ks)], axis=0) * ATTN_SCALE
    sp = jnp.maximum(z, 0.0) + jnp.log(1.0 + jnp.exp(-jnp.abs(z)))
    log_1m = jnp.where(mask, -sp, 0.0)
    suffix = sum(jnp.dot(piece, u, preferred_element_type=F32) for piece in _split2(log_1m))
    w = jnp.where(mask, jnp.exp(z - sp + suffix + carry), 0.0).astype(BF16)
    acc = acc + jnp.concatenate(
        [jnp.dot(w[h * tq:(h + 1) * tq], v, preferred_element_type=F32) for h, v in enumerate(vs)], axis=0)
    return carry + jnp.sum(log_1m, axis=-1, keepdims=True), acc


def _sb_continue(c):
    limit, floor = c[0], c[1]
    return jnp.logical_and(limit > 0, floor > -SB_EXIT)


def _stick_break_body(q_ref, k_ref, v_ref, u_ref, o_ref, *, tq, qpos_base):
    qt = pl.program_id(2)
    qpos0 = qpos_base + qt * tq
    u = u_ref[...]
    rows = ATTN_HB * tq
    rowpos = qpos0 + lax.broadcasted_iota(jnp.int32, (rows, 1), 0) % tq
    top = (qpos0 + tq + LANES - 1) // LANES * LANES
    heads = [slice(h * HEAD_DIM, (h + 1) * HEAD_DIM) for h in range(ATTN_HB)]
    qs = [q_ref[:, hs] for hs in heads]

    def body(c):
        limit, _, carry, acc = c
        start = pl.multiple_of(jnp.maximum(limit - SB_TK, 0), LANES)
        keys = pl.ds(start, SB_TK)
        kpos = start + lax.broadcasted_iota(jnp.int32, (1, SB_TK), 1)
        mask = jnp.logical_and(kpos < rowpos, kpos < limit)
        carry, acc = _sb_window(qs, [k_ref[keys, hs] for hs in heads], [v_ref[keys, hs] for hs in heads],
                                u, mask, carry, acc, tq)
        return limit - SB_TK, jnp.max(carry), carry, acc

    init = (top, jnp.float32(0.0), jnp.zeros((rows, 1), F32), jnp.zeros((rows, HEAD_DIM), F32))
    acc = lax.while_loop(_sb_continue, body, init)[3]
    for h, hs in enumerate(heads):
        o_ref[:, hs] = acc[h * tq:(h + 1) * tq].astype(o_ref.dtype)


def _stick_break_cached_body(q_ref, kc_ref, vc_ref, kn_ref, vn_ref, u_ref, o_ref, *, n_new, past, n_heads):
    hg = pl.program_id(1)
    u = u_ref[...]
    rows = ATTN_HB * n_new
    rowpos = past + lax.broadcasted_iota(jnp.int32, (rows, 1), 0) % n_new
    heads = [slice(h * HEAD_DIM, (h + 1) * HEAD_DIM) for h in range(ATTN_HB)]
    qs = [q_ref[:, hs] for hs in heads]
    first = past - (SB_TK - LANES)
    filler = jnp.zeros((LANES - n_new, HEAD_DIM), BF16)

    def cached(ref, start, size, h):
        row0 = start * n_heads + hg * ATTN_HB + h
        return ref[pl.ds(row0, size, stride=n_heads), :].astype(BF16)

    def newest(cache_ref, new_ref):
        return [jnp.concatenate([cached(cache_ref, first, LANES, h), new_ref[:, hs], filler], axis=0)
                for h, hs in enumerate(heads)]

    kpos = first + lax.broadcasted_iota(jnp.int32, (1, SB_TK), 1)
    carry, acc = _sb_window(qs, newest(kc_ref, kn_ref), newest(vc_ref, vn_ref), u, kpos < rowpos,
                            jnp.zeros((rows, 1), F32), jnp.zeros((rows, HEAD_DIM), F32), n_new)

    def body(c):
        limit, _, carry, acc = c
        start = jnp.maximum(limit - SB_TK, 0)
        mask = jnp.broadcast_to(start + lax.broadcasted_iota(jnp.int32, (1, SB_TK), 1) < limit, (rows, SB_TK))
        carry, acc = _sb_window(qs, [cached(kc_ref, start, SB_TK, h) for h in range(ATTN_HB)],
                                [cached(vc_ref, start, SB_TK, h) for h in range(ATTN_HB)],
                                u, mask, carry, acc, n_new)
        return limit - SB_TK, jnp.max(carry), carry, acc

    acc = lax.while_loop(_sb_continue, body, (jnp.int32(first), jnp.max(carry), carry, acc))[3]
    for h, hs in enumerate(heads):
        o_ref[:, hs] = acc[h * n_new:(h + 1) * n_new].astype(o_ref.dtype)


def _stick_break_cached(q2d, cache_k, cache_v, layer, new2d, batch, n_new, q_col, k_col, v_col):
    past, n_heads = cache_k.shape[2:4]
    assert past % LANES == 0 and past >= SB_TK and n_new <= LANES and n_heads % ATTN_HB == 0
    assert q_col % ATTN_HB == 0 and k_col % ATTN_HB == 0 and v_col % ATTN_HB == 0
    hw = ATTN_HB * HEAD_DIM
    flat = lambda c: c.reshape(c.shape[0], batch, past * n_heads, HEAD_DIM)
    j = np.arange(SB_TK)
    u = jnp.asarray(j[:, None] > j[None, :], BF16)
    cache_spec = pl.BlockSpec((None, None, past * n_heads, HEAD_DIM), lambda b, h: (layer, b, 0, 0))
    vmem = 2 * (2 * past * n_heads * HEAD_DIM * 4) + (8 << 20)
    return pl.pallas_call(
        functools.partial(_stick_break_cached_body, n_new=n_new, past=past, n_heads=n_heads),
        grid=(batch, n_heads // ATTN_HB),
        in_specs=[pl.BlockSpec((n_new, hw), lambda b, h: (b, q_col // ATTN_HB + h)),
                  cache_spec, cache_spec,
                  pl.BlockSpec((n_new, hw), lambda b, h: (b, k_col // ATTN_HB + h)),
                  pl.BlockSpec((n_new, hw), lambda b, h: (b, v_col // ATTN_HB + h)),
                  pl.BlockSpec((SB_TK, SB_TK), lambda b, h: (0, 0))],
        out_specs=pl.BlockSpec((n_new, hw), lambda b, h: (b, h)),
        out_shape=jax.ShapeDtypeStruct((batch * n_new, n_heads * HEAD_DIM), BF16),
        compiler_params=_cparams(vmem),
        name="stick_break_cached",
    )(q2d, flat(cache_k), flat(cache_v), new2d, new2d, u)


def _stick_break_attn(q2d, k2d, v2d, batch, n_q, n_k, n_heads, tq, qpos_base, q_col, k_col, v_col):
    assert n_q % tq == 0 and n_k % LANES == 0 and n_k >= SB_TK and n_heads % ATTN_HB == 0
    assert q_col % ATTN_HB == 0 and k_col % ATTN_HB == 0 and v_col % ATTN_HB == 0
    assert -(-(qpos_base + n_q) // LANES) * LANES <= n_k
    nt = n_q // tq
    hw = ATTN_HB * HEAD_DIM
    j = np.arange(SB_TK)
    u = jnp.asarray(j[:, None] > j[None, :], BF16)
    vmem = 2 * (2 * n_k * hw * 2) + (8 << 20)
    return pl.pallas_call(
        functools.partial(_stick_break_body, tq=tq, qpos_base=qpos_base),
        grid=(batch, n_heads // ATTN_HB, nt),
        in_specs=[pl.BlockSpec((tq, hw), lambda b, h, t: (b * nt + t, q_col // ATTN_HB + h)),
                  pl.BlockSpec((n_k, hw), lambda b, h, t: (b, k_col // ATTN_HB + h)),
                  pl.BlockSpec((n_k, hw), lambda b, h, t: (b, v_col // ATTN_HB + h)),
                  pl.BlockSpec((SB_TK, SB_TK), lambda b, h, t: (0, 0))],
        out_specs=pl.BlockSpec((tq, hw), lambda b, h, t: (b * nt + t, h)),
        out_shape=jax.ShapeDtypeStruct((batch * n_q, n_heads * HEAD_DIM), BF16),
        compiler_params=_cparams(vmem),
        name="stick_break_attn",
    )(q2d, k2d, v2d, u)


def _sortable(x):
    bits = pltpu.bitcast(x, jnp.int32)
    return jnp.where(bits < 0, bits ^ jnp.int32(0x7FFFFFFF), bits)


def _dsa_index_body(qi_ref, wi_ref, kz_ref, bias_ref, key_ref, *, tq, tk, n_real, qpos_base, topk):
    n_pad = key_ref.shape[1]
    qt = pl.program_id(1)
    qpos0 = qpos_base + qt * tq
    rowpos = qpos0 + lax.broadcasted_iota(jnp.int32, (tq, 1), 0)
    qchunk = rowpos // CHUNK
    n_kb = jnp.minimum((((qpos0 + tq - 1) // CHUNK + 1) * CHUNK + tk - 1) // tk, n_pad // tk)
    n_tail = jnp.maximum(n_real - n_kb * tk, 0)
    tail0 = n_kb * tk

    w = wi_ref[...] * (IDX_SCALE * IDX_W_SCALE)

    def block_pos(kb):
        start = pl.multiple_of(kb * tk, tk)
        return start, start + lax.broadcasted_iota(jnp.int32, (1, tk), 1)

    def score_block(kb, carry):
        start, kpos = block_pos(kb)
        kz = kz_ref[pl.ds(start, tk), :]
        acc = jnp.zeros((tq, tk), F32)
        for pair in range(IDX_HEADS // 2):
            qp = qi_ref[:, pair * LANES:(pair + 1) * LANES]
            for half in range(2):
                h = 2 * pair + half
                sc = _nt_dot(qp, kz[:, half * LANES:(half + 1) * LANES])
                acc = acc + jnp.maximum(sc, 0.0) * w[:, h:h + 1]
        key = _sortable(jnp.where(kpos // CHUNK <= qchunk, acc, NEG_INF))
        key_ref[:, pl.ds(start, tk)] = jnp.where(kpos < n_real, key, jnp.int32(INT_MIN))
        return carry

    lax.fori_loop(0, n_kb, score_block, 0)

    def count(pred):
        def blk(kb, acc):
            start, kpos = block_pos(kb)
            hit = pred(key_ref[:, pl.ds(start, tk)], kpos).astype(F32)
            return acc + sum(hit[:, c:c + LANES] for c in range(0, tk, LANES))
        lanes = lax.fori_loop(0, n_kb, blk, jnp.zeros((tq, LANES), F32))
        return jnp.sum(lanes, axis=-1, keepdims=True)

    kf = jnp.float32(topk)
    tail_f = n_tail.astype(F32)
    neg_key = jnp.int32(NEG_KEY)

    def thr_step(i, ans_u):
        cand_u = ans_u | lax.shift_left(jnp.int32(1), 31 - i)
        cand = cand_u ^ jnp.int32(INT_MIN)
        cnt = count(lambda key, kpos: key >= cand) + jnp.where(neg_key >= cand, tail_f, 0.0)
        return jnp.where(cnt >= kf, cand_u, ans_u)

    thr = lax.fori_loop(0, 32, thr_step, jnp.zeros((tq, 1), jnp.int32)) ^ jnp.int32(INT_MIN)
    tail_gt = jnp.where(neg_key > thr, tail_f, 0.0)
    n_gt = count(lambda key, kpos: key > thr) + tail_gt
    n_eq = count(lambda key, kpos: key == thr) + jnp.where(neg_key == thr, tail_f, 0.0)
    need = kf - n_gt

    n_bits = int(np.ceil(np.log2(n_pad)))

    def tie_search():
        def pos_step(i, lo):
            cand = lo + lax.shift_left(jnp.int32(1), n_bits - 1 - i)
            in_tail = jnp.clip(cand - tail0, 0, n_tail).astype(F32)
            cnt = (count(lambda key, kpos: jnp.logical_and(key == thr, kpos < cand))
                   + jnp.where(neg_key == thr, in_tail, 0.0))
            return jnp.where(cnt < need, cand, lo)
        return lax.fori_loop(0, n_bits, pos_step, jnp.zeros((tq, 1), jnp.int32))

    has_ties = jnp.max(jnp.where(n_eq > need, 1.0, 0.0)) > 0.0
    last = lax.cond(has_ties, tie_search, lambda: jnp.full((tq, 1), n_pad, jnp.int32))

    bias_ref[0] = jnp.full(bias_ref.shape[1:], NEG_INF, bias_ref.dtype)

    def write_block(kb, carry):
        start, kpos = block_pos(kb)
        key = key_ref[:, pl.ds(start, tk)]
        sel = jnp.logical_or(key > thr, jnp.logical_and(key == thr, kpos <= last))
        ok = jnp.logical_and(jnp.logical_and(sel, kpos // CHUNK <= qchunk), kpos < n_real)
        bias_ref[0, :, pl.ds(start, tk)] = jnp.where(ok, 0.0, NEG_INF).astype(bias_ref.dtype)
        return carry

    lax.fori_loop(0, n_kb, write_block, 0)


def _dsa_index(qi2d, wi, kz, batch, n_q, n_k, n_real, tq, tk, qpos_base, qi_col):
    nt = n_q // tq
    topk = min(TOPK_MAX, n_real // 4)
    vmem = 2 * (tq * 2048 * 2 + n_k * 256 * 2 + tq * n_k * 2) + tq * n_k * 4 + 8 * tq * tk * 4 + (4 << 20)
    return pl.pallas_call(
        functools.partial(_dsa_index_body, tq=tq, tk=tk, n_real=n_real, qpos_base=qpos_base, topk=topk),
        grid=(batch, nt),
        in_specs=[pl.BlockSpec((tq, IDX_HEADS * IDX_DIM), lambda b, t: (b * nt + t, qi_col)),
                  pl.BlockSpec((tq, IDX_HEADS), lambda b, t: (b * nt + t, 0)),
                  pl.BlockSpec((n_k, 2 * LANES), lambda b, t: (b, 0))],
        out_specs=pl.BlockSpec((1, tq, n_k), lambda b, t: (b, t, 0)),
        out_shape=jax.ShapeDtypeStruct((batch, n_q, n_k), BF16),
        scratch_shapes=[pltpu.VMEM((tq, n_k), jnp.int32)],
        compiler_params=_cparams(vmem),
        name="dsa_index",
    )(qi2d, wi, kz)


def _dsa_attn_body(*refs, tq, tk, qpos_base, cached):
    if cached:
        (q_ref, kc_ref, vc_ref, kn_ref, vn_ref, bias_ref, slope_ref, o_ref,
         s0_ref, s1_ref, p0_ref, p1_ref, k_ref, v_ref) = refs
        past, n_kv = cached
        n_new = kn_ref.shape[0]
        own = pl.ds(pl.program_id(1), past, stride=n_kv)
        for full_ref, cache_ref, new_ref in ((k_ref, kc_ref, kn_ref), (v_ref, vc_ref, vn_ref)):
            full_ref[0:past, :] = cache_ref[own, :].astype(BF16)
            full_ref[past:past + n_new, :] = new_ref[...]
            full_ref[past + n_new:, :] = jnp.zeros((full_ref.shape[0] - past - n_new, HEAD_DIM), BF16)
    else:
        q_ref, k_ref, v_ref, bias_ref, slope_ref, o_ref, s0_ref, s1_ref, p0_ref, p1_ref = refs
    s_refs, p_refs = (s0_ref, s1_ref), (p0_ref, p1_ref)
    n_pad = k_ref.shape[0]
    qt = pl.program_id(2)
    qpos0 = qpos_base + qt * tq
    tile_end = qpos0 + tq
    rows = C_GROUP * tq
    rowpos = qpos0 + lax.broadcasted_iota(jnp.int32, (tq, 1), 0)
    n_kb = jnp.minimum((((qpos0 + tq - 1) // CHUNK + 1) * CHUNK + tk - 1) // tk, n_pad // tk)
    qs = jnp.concatenate([q_ref[:, g * HEAD_DIM:(g + 1) * HEAD_DIM] for g in range(C_GROUP)], axis=0)

    def key_rows(kb):
        return pl.ds(pl.multiple_of(kb * tk, tk), tk)

    def scores(kb, slot):
        s_refs[slot][...] = _nt_dot(qs, k_ref[key_rows(kb), :])

    def weighted_values(kb, slot):
        return jnp.dot(p_refs[slot][...], v_ref[key_rows(jnp.maximum(kb, 0)), :], preferred_element_type=F32)

    slope = slope_ref[0]

    def softmax(kb, slot, m, l, has_later_keys):
        kpos = kb * tk + lax.broadcasted_iota(jnp.int32, (1, tk), 1)
        back = (tile_end - kpos).astype(F32)
        s = (s_refs[slot][...] * ATTN_SCALE - slope * back).reshape(C_GROUP, tq, tk)
        s = s + bias_ref[0, :, key_rows(kb)].astype(F32)[None]
        if has_later_keys:
            ahead = jnp.where(kpos > rowpos, (rowpos - kpos).astype(F32), 0.0)
            s = s + (2.0 * slope).reshape(C_GROUP, tq, 1) * ahead[None]
        s = s.reshape(rows, tk)
        m_new = jnp.maximum(m, jnp.max(s, axis=-1, keepdims=True))
        p = jnp.exp(s - m_new)
        p_refs[slot][...] = p.astype(BF16)
        a = jnp.exp(m - m_new)
        return m_new, a * l + jnp.sum(p, axis=-1, keepdims=True), a

    def step(kb, slot, state):
        m, l, a_prev, acc = state
        acc = a_prev * acc + weighted_values(kb - 1, 1 - slot)
        m, l, a = softmax(kb, slot, m, l, False)
        scores(kb + 1, 1 - slot)
        return m, l, a, acc

    def finish(kb, slot, state):
        m, l, a_prev, acc = state
        acc = a_prev * acc + weighted_values(kb - 1, 1 - slot)
        m, l, a = softmax(kb, slot, m, l, True)
        return a * acc + weighted_values(kb, slot), l

    p_refs[1][...] = jnp.zeros(p_refs[1].shape, BF16)
    scores(0, 0)
    init = (jnp.full((rows, 1), 0.5 * NEG_INF, F32), jnp.zeros((rows, 1), F32),
            jnp.ones((rows, 1), F32), jnp.zeros((rows, HEAD_DIM), F32))
    last = n_kb - 1
    state = lax.fori_loop(0, last // 2, lambda j, st: step(2 * j + 1, 1, step(2 * j, 0, st)), init)
    acc, l = lax.cond(last % 2 == 1,
                      lambda st: finish(last, 1, step(last - 1, 0, st)),
                      lambda st: finish(last, 0, st), state)
    o = (acc / l).astype(o_ref.dtype)
    for g in range(C_GROUP):
        o_ref[:, g * HEAD_DIM:(g + 1) * HEAD_DIM] = o[g * tq:(g + 1) * tq]


def _dsa_attn(q2d, k2d, v2d, bias, batch, n_q, n_k, n_kv, tq, tk, qpos_base, k_col, v_col, cache=None):
    assert tk % tq == 0 and qpos_base % tq == 0 and n_k % tk == 0
    nt = n_q // tq
    gw = C_GROUP * HEAD_DIM
    n_heads = n_kv * C_GROUP
    slopes = 2.0 ** (-8.0 * np.arange(1, n_heads + 1) / n_heads)
    slope_rows = np.repeat(slopes.reshape(n_kv, C_GROUP), tq, axis=1)[..., None].astype(np.float32)
    rows = C_GROUP * tq
    vmem = (2 * (2 * n_k * HEAD_DIM * 2 + tq * n_k * 2 + 2 * tq * gw * 2 + rows * LANES * 4)
            + 2 * rows * tk * 6 + 8 * rows * tk * 4 + (4 << 20))
    scratch = [pltpu.VMEM((rows, tk), F32)] * 2 + [pltpu.VMEM((rows, tk), BF16)] * 2
    if cache is None:
        cached = None
        kv_specs = [pl.BlockSpec((n_k, HEAD_DIM), lambda b, n, t: (b, k_col + n)),
                    pl.BlockSpec((n_k, HEAD_DIM), lambda b, n, t: (b, v_col + n))]
        kv_args = [k2d, v2d]
    else:
        cache_k, cache_v, layer = cache
        past = cache_k.shape[2]
        assert nt == 1 and qpos_base == past and past + n_q <= n_k and cache_k.shape[3] == n_kv
        cached = (past, n_kv)
        flat = lambda c: c.reshape(c.shape[0], batch, past * n_kv, HEAD_DIM)
        cache_spec = pl.BlockSpec((None, None, past * n_kv, HEAD_DIM), lambda b, n, t: (layer, b, 0, 0))
        kv_specs = [cache_spec, cache_spec,
                    pl.BlockSpec((n_q, HEAD_DIM), lambda b, n, t: (b, k_col + n)),
                    pl.BlockSpec((n_q, HEAD_DIM), lambda b, n, t: (b, v_col + n))]
        kv_args = [flat(cache_k), flat(cache_v), k2d, v2d]
        scratch += [pltpu.VMEM((n_k, HEAD_DIM), BF16), pltpu.VMEM((n_k, HEAD_DIM), BF16)]
        vmem += 2 * 2 * past * n_kv * HEAD_DIM * 4
    return pl.pallas_call(
        functools.partial(_dsa_attn_body, tq=tq, tk=tk, qpos_base=qpos_base, cached=cached),
        grid=(batch, n_kv, nt),
        in_specs=[pl.BlockSpec((tq, gw), lambda b, n, t: (b * nt + t, n))] + kv_specs + [
            pl.BlockSpec((1, tq, n_k), lambda b, n, t: (b, t, 0)),
            pl.BlockSpec((1, rows, 1), lambda b, n, t: (n, 0, 0))],
        out_specs=pl.BlockSpec((tq, gw), lambda b, n, t: (b * nt + t, n)),
        out_shape=jax.ShapeDtypeStruct((batch * n_q, n_heads * HEAD_DIM), BF16),
        scratch_shapes=scratch,
        compiler_params=_cparams(vmem),
        name="dsa_attn",
    )(q2d, *kv_args, bias, jnp.asarray(slope_rows))


def _mem_attn_body(x_ref, wq_ref, mk_ref, mv_ref, wo_ref, g_ref, b_ref, o_ref, ob_ref, y_ref, *,
                   tm, rows_per_seq, n_mem):
    q = jnp.dot(x_ref[...].astype(BF16), wq_ref[...], preferred_element_type=F32).astype(BF16)
    n_seq = tm // rows_per_seq
    heads = [slice(h * HEAD_DIM, (h + 1) * HEAD_DIM) for h in range(MEM_HEADS)]
    s = jnp.concatenate(
        [_nt_dot(q[s_i * rows_per_seq:(s_i + 1) * rows_per_seq, hs],
                 mk_ref[s_i * n_mem:(s_i + 1) * n_mem, hs].astype(BF16))
         for hs in heads for s_i in range(n_seq)], axis=0) * ATTN_SCALE
    m = jnp.max(s, axis=-1, keepdims=True)
    p = jnp.exp(s - m)
    l = jnp.sum(p, axis=-1, keepdims=True)
    p = p.astype(BF16)
    o_heads = []
    for h, hs in enumerate(heads):
        pieces = []
        for s_i in range(n_seq):
            rows = slice(h * tm + s_i * rows_per_seq, h * tm + (s_i + 1) * rows_per_seq)
            o = jnp.dot(p[rows], mv_ref[s_i * n_mem:(s_i + 1) * n_mem, hs].astype(BF16),
                        preferred_element_type=F32)
            pieces.append((o / l[rows]).astype(BF16))
        o_heads.append(pieces[0] if n_seq == 1 else jnp.concatenate(pieces, axis=0))
    y_ref[...] = jnp.dot(jnp.concatenate(o_heads, axis=-1), wo_ref[...], preferred_element_type=F32)
    _residual_ln_rows(y_ref, x_ref, g_ref, b_ref, o_ref, ob_ref, tm)


def _mem_attn_ln(x, wq, wo, l, mk, mv, g, b, tm, rows_per_seq, seq_rows):
    m, d = x.shape
    mw = wq.shape[-1]
    n_seq_tile = tm // rows_per_seq
    n_mem = mk.shape[0] // (m // seq_rows)
    if n_seq_tile == 1:
        mem_map = lambda i: ((i * tm) // seq_rows, 0)
    else:
        assert rows_per_seq == seq_rows
        mem_map = lambda i: (i, 0)
    vmem = 2 * (tm * d * 4 * 2 + tm * d * 2 + 2 * d * mw * 2 + 2 * n_seq_tile * n_mem * mw * 4) + tm * d * 4
    out_specs = [pl.BlockSpec((tm, d), lambda i: (i, 0)), pl.BlockSpec((tm, d), lambda i: (i, 0))]
    out_shape = [jax.ShapeDtypeStruct((m, d), F32), jax.ShapeDtypeStruct((m, d), BF16)]
    return pl.pallas_call(
        functools.partial(_mem_attn_body, tm=tm, rows_per_seq=rows_per_seq, n_mem=n_mem),
        grid=(m // tm,),
        in_specs=[pl.BlockSpec((tm, d), lambda i: (i, 0)),
                  _w_spec(wq, l, (d, mw), lambda i: (0, 0)),
                  pl.BlockSpec((n_seq_tile * n_mem, mw), mem_map),
                  pl.BlockSpec((n_seq_tile * n_mem, mw), mem_map),
                  _w_spec(wo, l, (mw, d), lambda i: (0, 0)),
                  pl.BlockSpec((1, d), lambda i: (0, 0)),
                  pl.BlockSpec((1, d), lambda i: (0, 0))],
        out_specs=out_specs,
        out_shape=out_shape,
        scratch_shapes=[pltpu.VMEM((tm, d), F32)],
        compiler_params=_cparams(vmem),
        name="mem_attn_ln",
    )(x, wq, mk, mv, wo, g.reshape(1, d), b.reshape(1, d))


FFN_TN = 256
FFN_SUB = 256
FFN_TM = 2048
HALO = 8


def _gelu(x):
    return 0.5 * x * (1.0 + lax.erf(x * np.float32(2.0 ** -0.5)))


def _ffn_up_body(x_ref, wa_ref, wg_ref, cwa_ref, cwg_ref, cba_ref, cbg_ref, *rest,
                 tm, rows_per_seq, tiles_per_seq, carried, nj, emit_w):
    if carried:
        act_ref, sta_ref, stg_ref, carry_ref, ext_ref, ext2_ref = rest
    elif emit_w:
        pa_ref, pg_ref, act_ref, sta_ref, stg_ref, wab_ref, wgb_ref, ext_ref = rest
    else:
        pa_ref, pg_ref, act_ref, sta_ref, stg_ref, ext_ref = rest
    i = pl.program_id(0)
    j = pl.program_id(1)

    def conv_gate(base, rows, buf=None):
        buf = ext_ref if buf is None else buf
        cw = jnp.concatenate([cwa_ref[...], cwg_ref[...]], axis=1)
        cb = jnp.concatenate([cba_ref[...], cbg_ref[...]], axis=1)
        h = (buf[base:base + rows, :] * cw[2:3, :]
             + buf[base - 1:base - 1 + rows, :] * cw[1:2, :]
             + buf[base - 2:base - 2 + rows, :] * cw[0:1, :]
             + cb)
        return (h[:, :FFN_TN] * _gelu(h[:, FFN_TN:])).astype(act_ref.dtype)

    def put_state(s_i, last2):
        sta_ref[s_i] = last2[:, :FFN_TN]
        stg_ref[s_i] = last2[:, FFN_TN:]

    @pl.when(j < nj)
    def _():
        if carried:
            @pl.when(i % tiles_per_seq == 0)
            def _():
                carry_ref[j] = jnp.zeros((2, 2 * FFN_TN), F32)

            sub = min(FFN_SUB, tm)
            bufs = (ext_ref, ext2_ref)
            last2 = carry_ref[j]
            for n, r in enumerate(range(0, tm, sub)):
                buf = bufs[n % 2]
                buf[HALO - 2:HALO, :] = last2
                buf[HALO:HALO + sub, :FFN_TN] = jnp.dot(x_ref[r:r + sub, :], wa_ref[...],
                                                        preferred_element_type=F32)
                buf[HALO:HALO + sub, FFN_TN:] = jnp.dot(x_ref[r:r + sub, :], wg_ref[...],
                                                        preferred_element_type=F32)
                act_ref[r:r + sub, :] = conv_gate(HALO, sub, buf)
                last2 = buf[HALO + sub - 2:HALO + sub, :]
            carry_ref[j] = last2
            put_state(0, last2)
        else:
            wa, wg = wa_ref, wg_ref
            if emit_w:
                wab_ref[...] = wa_ref[...].astype(BF16)
                wgb_ref[...] = wg_ref[...].astype(BF16)
                wa, wg = wab_ref, wgb_ref
            up = jnp.concatenate([jnp.dot(x_ref[...], wa[...], preferred_element_type=F32),
                                  jnp.dot(x_ref[...], wg[...], preferred_element_type=F32)], axis=1)
            for s_i in range(tm // rows_per_seq):
                base = s_i * (rows_per_seq + HALO) + HALO
                rsl = slice(s_i * rows_per_seq, (s_i + 1) * rows_per_seq)
                ext_ref[base - 2:base, :] = jnp.concatenate([pa_ref[s_i], pg_ref[s_i]], axis=1)
                ext_ref[base:base + rows_per_seq, :] = up[rsl]
                put_state(s_i, ext_ref[base + rows_per_seq - 2:base + rows_per_seq, :])
                act_ref[rsl, :] = conv_gate(base, rows_per_seq)

    @pl.when(j >= nj)
    def _():
        act_ref[...] = jnp.zeros_like(act_ref)


def _ffn_up(x, w_val, w_gate, l, conv_w, conv_b, prev, tm, rows_per_seq, seq_rows, n_act_cols, emit_w=False):
    m, d = x.shape
    f2 = conv_w.shape[1]
    wt = 2 * FFN_TN
    assert f2 % wt == 0 and n_act_cols % FFN_TN == 0
    nj = f2 // wt
    nj_pad = n_act_cols // FFN_TN
    carried = prev is None
    assert not (emit_w and carried)
    n_seq = tm // rows_per_seq
    tiles_per_seq = max(seq_rows // tm, 1)
    n_tiles = m // tm
    val = lambda j: jnp.minimum(j, nj - 1)
    gate0 = 0 if l is None else nj
    in_specs = [pl.BlockSpec((tm, d), lambda i, j: (i, 0)),
                _w_spec(w_val, l, (d, FFN_TN), lambda i, j: (0, val(j))),
                _w_spec(w_gate, l, (d, FFN_TN), lambda i, j: (0, val(j) + gate0)),
                pl.BlockSpec((CONV_W, FFN_TN), lambda i, j: (0, val(j))),
                pl.BlockSpec((CONV_W, FFN_TN), lambda i, j: (0, val(j) + nj)),
                pl.BlockSpec((1, FFN_TN), lambda i, j: (0, val(j))),
                pl.BlockSpec((1, FFN_TN), lambda i, j: (0, val(j) + nj))]
    args = [x, w_val, w_gate, conv_w, conv_w, conv_b, conv_b]
    scratch = []
    if carried:
        assert n_seq == 1
        scratch.append(pltpu.VMEM((nj, 2, wt), F32))
        scratch += [pltpu.VMEM((min(FFN_SUB, tm) + HALO, wt), F32)] * 2
    else:
        in_specs += [pl.BlockSpec((n_seq, 2, FFN_TN), lambda i, j: (i, 0, val(j))),
                     pl.BlockSpec((n_seq, 2, FFN_TN), lambda i, j: (i, 0, val(j) + nj))]
        args += [prev, prev]
        scratch.append(pltpu.VMEM((n_seq * (rows_per_seq + HALO), wt), F32))
    st_shape = jax.ShapeDtypeStruct((n_tiles * n_seq, 2, f2 // 2), F32)
    st_spec = pl.BlockSpec((n_seq, 2, FFN_TN), lambda i, j: (i, 0, val(j)))
    vmem = (2 * (tm * d * 2 + d * wt * w_val.dtype.itemsize + tm * FFN_TN * 2) + tm * wt * 4
            + 8 * min(tm, FFN_SUB) * wt * 4 + (4 << 20))
    out_specs = [pl.BlockSpec((tm, FFN_TN), lambda i, j: (i, j)), st_spec, st_spec]
    out_shape = [jax.ShapeDtypeStruct((m, n_act_cols), BF16), st_shape, st_shape]
    if emit_w:
        vmem += 2 * d * wt * 2
        out_specs += [pl.BlockSpec((d, FFN_TN), lambda i, j: (0, val(j)))] * 2
        out_shape += [jax.ShapeDtypeStruct((d, f2 // 2), BF16)] * 2
    act, st_a, st_g, *w_images = pl.pallas_call(
        functools.partial(_ffn_up_body, tm=tm, rows_per_seq=rows_per_seq, tiles_per_seq=tiles_per_seq,
                          carried=carried, nj=nj, emit_w=emit_w),
        grid=(n_tiles, nj_pad),
        in_specs=in_specs,
        out_specs=out_specs,
        out_shape=out_shape,
        scratch_shapes=scratch,
        compiler_params=_cparams(vmem),
        name="ffn_up_conv_gate",
    )(*args)
    return (act, jnp.concatenate([st_a, st_g], axis=-1), *w_images)


def _pad_rows(a, n_rows):
    return jnp.pad(a, ((0, 0), (0, n_rows - a.shape[1]), (0, 0)))


def kernel(x_prompt, x_sample, cache_a_k, cache_a_v, cache_b_k, cache_b_v, cache_c_k, cache_c_v, cache_c_idx, cache_mem_k, cache_mem_v, state_ffn_conv, mem_prompt, w_in_even, w_out_even, rel_bias_a, w_in_odd, w_out_odd, w_mem_q, w_mem_kv, w_mem_o, w_up, conv_w, conv_b, w_down, ln_g, ln_b):
    bp, sp, d = x_prompt.shape
    bs, n_new, _ = x_sample.shape
    depth = w_up.shape[0]
    h_a = cache_a_k.shape[3]
    h_b = cache_b_k.shape[3]
    n_kv = cache_c_k.shape[3]
    h_c = n_kv * C_GROUP
    hw_a, hw_b, kvw = h_a * HEAD_DIM, h_b * HEAD_DIM, n_kv * HEAD_DIM
    qiw = IDX_HEADS * IDX_DIM
    a_win_s = cache_a_k.shape[2]
    a_win_p = min(A_LEFT_CHUNKS * CHUNK, sp)
    past = cache_b_k.shape[2]
    n_mem = mem_prompt.shape[1]
    mem_w = w_mem_q.shape[2]
    d_ff = w_down.shape[1]
    mp, ms = bp * sp, bs * n_new
    l_s = past + n_new
    l_s_pad = -(-l_s // LANES) * LANES
    main_odd = h_c * HEAD_DIM + 2 * kvw + qiw
    tail_pad = LANES
    ff_pad = -(-d_ff // 1024) * 1024

    wmq = w_mem_q.astype(BF16)
    wmo = w_mem_o.astype(BF16)

    xp = x_prompt.reshape(mp, d)
    xs = x_sample.reshape(ms, d)
    xp_b, xs_b = xp.astype(BF16), xs.astype(BF16)
    mem2d = mem_prompt.reshape(bp * n_mem, d).astype(BF16)
    cmk = cache_mem_k.reshape(depth, bs * n_mem, mem_w).astype(BF16)
    cmv = cache_mem_v.reshape(depth, bs * n_mem, mem_w).astype(BF16)

    outs = {k: [] for k in ("akp", "avp", "bkp", "bvp", "ckp", "cvp", "cip", "mkp", "mvp", "fp",
                            "aks", "avs", "bks", "bvs", "cks", "cvs", "cis", "fs")}

    for l in range(depth):
        i = l // 2
        if l % 2 == 0:
            w_out = w_out_even
            sf, sb, w_in_l = _matmul(xs_b, w_in_even, i, (F32, BF16), 256, 512, emit_w=True)
            pf, pb = _matmul(xp_b, w_in_l, None, (F32, BF16), 1024, 512)
            oa = _band_attn_prompt(pb, rel_bias_a[i], bp, sp, h_a, 0, h_a, 2 * h_a)
            cb0 = 3 * h_a
            ob = _stick_break_attn(pb, pb, pb, bp, sp, sp, h_b, 128, 0, cb0, cb0 + h_b, cb0 + 2 * h_b)
            mix_p = jnp.concatenate([oa, ob], axis=-1)
            pf4 = pf.reshape(bp, sp, -1)
            outs["akp"].append(pf4[:, sp - a_win_p:, hw_a:2 * hw_a].reshape(bp, a_win_p, h_a, HEAD_DIM))
            outs["avp"].append(pf4[:, sp - a_win_p:, 2 * hw_a:3 * hw_a].reshape(bp, a_win_p, h_a, HEAD_DIM))
            o0 = 3 * hw_a
            outs["bkp"].append(pf4[:, :, o0 + hw_b:o0 + 2 * hw_b].reshape(bp, sp, h_b, HEAD_DIM))
            outs["bvp"].append(pf4[:, :, o0 + 2 * hw_b:o0 + 3 * hw_b].reshape(bp, sp, h_b, HEAD_DIM))
            sf4 = sf.reshape(bs, n_new, -1)
            oa_s = _band_attn_cached(sb, cache_a_k, cache_a_v, i, sb, rel_bias_a[i], bs, n_new, 1, 2)
            kb_new = sf4[:, :, o0 + hw_b:o0 + 2 * hw_b]
            vb_new = sf4[:, :, o0 + 2 * hw_b:o0 + 3 * hw_b]
            ob_s = _stick_break_cached(sb, cache_b_k, cache_b_v, i, sb, bs, n_new, cb0, cb0 + h_b, cb0 + 2 * h_b)
            mix_s = jnp.concatenate([oa_s, ob_s], axis=-1)
            outs["aks"].append(sf4[:, :, hw_a:2 * hw_a].reshape(bs, n_new, h_a, HEAD_DIM))
            outs["avs"].append(sf4[:, :, 2 * hw_a:3 * hw_a].reshape(bs, n_new, h_a, HEAD_DIM))
            outs["bks"].append(kb_new.reshape(bs, n_new, h_b, HEAD_DIM))
            outs["bvs"].append(vb_new.reshape(bs, n_new, h_b, HEAD_DIM))
        else:
            w_out = w_out_odd
            sf, sb, w_main_l = _matmul(xs_b, w_in_odd, i, (F32, BF16), 256, 512, n_cols=main_odd, emit_w=True)
            stl, w_tail_l = _matmul(xs_b, w_in_odd, i, (F32,), 256, tail_pad, col0=main_odd, n_cols=tail_pad,
                                    emit_w=True)
            kc0 = h_c
            vc0 = h_c + n_kv
            qic = (h_c * HEAD_DIM + 2 * kvw) // qiw

            def kz_of(ki):
                z = jnp.zeros_like(ki)
                return jnp.concatenate([ki, z, z, ki], axis=-1).astype(BF16).reshape(-1, 2 * LANES)

            pf, pb = _matmul(xp_b, w_main_l, None, (F32, BF16), 1024, 512)
            (pt,) = _matmul(xp_b, w_tail_l, None, (F32,), 1024, tail_pad)
            ki_p = pt[:, :IDX_DIM].reshape(bp, sp, IDX_DIM)
            wi_p = pt[:, IDX_DIM:IDX_DIM + IDX_HEADS]
            sel_p = _dsa_index(pb, wi_p, kz_of(ki_p), bp, sp, sp, sp, 128, 512, 0, qic)
            mix_p = _dsa_attn(pb, pb, pb, sel_p, bp, sp, sp, n_kv, 256, 512, 0, kc0, vc0)
            pf4 = pf.reshape(bp, sp, -1)
            q_w = h_c * HEAD_DIM
            outs["ckp"].append(pf4[:, :, q_w:q_w + kvw].reshape(bp, sp, n_kv, HEAD_DIM))
            outs["cvp"].append(pf4[:, :, q_w + kvw:q_w + 2 * kvw].reshape(bp, sp, n_kv, HEAD_DIM))
            outs["cip"].append(ki_p)
            sf4 = sf.reshape(bs, n_new, -1)
            k_new = sf4[:, :, q_w:q_w + kvw]
            v_new = sf4[:, :, q_w + kvw:q_w + 2 * kvw]
            ki_new = stl[:, :IDX_DIM].reshape(bs, n_new, IDX_DIM)
            wi_s = stl[:, IDX_DIM:IDX_DIM + IDX_HEADS]
            ki_all = _pad_rows(jnp.concatenate([cache_c_idx[i], ki_new], axis=1), l_s_pad)
            sel_s = _dsa_index(sb, wi_s, kz_of(ki_all), bs, n_new, l_s_pad, l_s, n_new, l_s_pad, past, qic)
            mix_s = _dsa_attn(sb, sb, sb, sel_s, bs, n_new, l_s_pad, n_kv, n_new, l_s_pad, past, kc0, vc0,
                              cache=(cache_c_k, cache_c_v, i))
            outs["cks"].append(k_new.reshape(bs, n_new, n_kv, HEAD_DIM))
            outs["cvs"].append(v_new.reshape(bs, n_new, n_kv, HEAD_DIM))
            outs["cis"].append(ki_new)

        xs, xs_b, w_out_l = _matmul_residual_ln(mix_s, w_out, i, xs, ln_g[l, 0], ln_b[l, 0], 256, 512, emit_w=True)
        xp, xp_b = _matmul_residual_ln(mix_p, w_out_l, None, xp, ln_g[l, 0], ln_b[l, 0], 512, 512)

        (mkv,) = _matmul(mem2d, w_mem_kv, l, (F32,), 512, 512)
        mk_p, mv_p = mkv[:, :mem_w], mkv[:, mem_w:]
        outs["mkp"].append(mk_p.reshape(bp, n_mem, MEM_HEADS, HEAD_DIM))
        outs["mvp"].append(mv_p.reshape(bp, n_mem, MEM_HEADS, HEAD_DIM))
        xp, xp_b = _mem_attn_ln(xp, wmq, wmo, l, mk_p, mv_p, ln_g[l, 1], ln_b[l, 1], 256, 256, sp)
        xs, xs_b = _mem_attn_ln(xs, wmq, wmo, l, cmk[l], cmv[l], ln_g[l, 1], ln_b[l, 1],
                                min(ms, 128), n_new, n_new)

        cb_l = conv_b[l][None]
        ffn_tm = min(FFN_TM, sp)
        act_s, st_s, w_val_l, w_gate_l = _ffn_up(xs_b, w_up, w_up, l, conv_w[l], cb_l, state_ffn_conv[l],
                                                 ms, n_new, n_new, ff_pad, emit_w=True)
        act_p, st_p = _ffn_up(xp_b, w_val_l, w_gate_l, None, conv_w[l], cb_l, None, ffn_tm, ffn_tm, sp, ff_pad)
        tiles_per_seq = sp // ffn_tm
        outs["fp"].append(st_p[tiles_per_seq - 1::tiles_per_seq])
        outs["fs"].append(st_s)
        xs, xs_b, w_down_l = _matmul_residual_ln(act_s, w_down, l, xs, ln_g[l, 2], ln_b[l, 2], 256, 512, emit_w=True)
        xp, xp_b = _matmul_residual_ln(act_p, w_down_l, None, xp, ln_g[l, 2], ln_b[l, 2], 512, 512)

    st = jnp.stack
    o = outs
    rolled_k = _rolled_band(cache_a_k, st(o["aks"]))
    rolled_v = _rolled_band(cache_a_v, st(o["avs"]))
    return (xp.reshape(bp, sp, d), xs.reshape(bs, n_new, d),
            st(o["akp"]), st(o["avp"]), st(o["bkp"]), st(o["bvp"]), st(o["ckp"]), st(o["cvp"]), st(o["cip"]),
            st(o["mkp"]), st(o["mvp"]), st(o["fp"]),
            rolled_k, rolled_v, st(o["bks"]), st(o["bvs"]), st(o["cks"]), st(o["cvs"]), st(o["cis"]),
            st(o["fs"]))
```

```python
import functools

import numpy as np
import jax
import jax.numpy as jnp
from jax import lax
from jax.experimental import pallas as pl
from jax.experimental.pallas import tpu as pltpu

BF16 = jnp.bfloat16
F32 = jnp.float32

CHUNK = 64
HEAD_DIM = 128
A_LEFT_CHUNKS = 8
REL_CLIP = 128
C_GROUP = 4
IDX_HEADS = 32
IDX_DIM = 64
TOPK_MAX = 256
MEM_HEADS = 4
CONV_W = 3
DEPTH = 4
ALPHA = (2.0 * DEPTH) ** 0.25
LN_EPS = 1e-5
ATTN_SCALE = HEAD_DIM ** -0.5
IDX_SCALE = IDX_DIM ** -0.5
IDX_W_SCALE = IDX_HEADS ** -0.5
NEG_INF = -1e30

V7X_VMEM_BYTES = 64 * 1024 * 1024
LANES = 128
SUBLANES = 8
SB_EXIT = 104.0
INT_MIN = -(2 ** 31)
NEG_KEY = int(np.array(NEG_INF, np.float32).view(np.int32)) ^ 0x7FFFFFFF


def _cparams(vmem_bytes):
    limit = int(min(max(vmem_bytes * 1.25 + (4 << 20), 32 << 20), V7X_VMEM_BYTES - (6 << 20)))
    return pltpu.CompilerParams(vmem_limit_bytes=limit)


def _nt_dot(a, b):
    return lax.dot_general(a, b, (((1,), (1,)), ((), ())), preferred_element_type=F32)


def _w_spec(w, l, block, index):
    if l is None:
        return pl.BlockSpec(block, index)
    return pl.BlockSpec((None,) + block, lambda *grid: (l,) + tuple(index(*grid)))


def _mm_body(x_ref, w_ref, *out_refs, emit_w):
    if emit_w:
        out_refs[-1][...] = w_ref[...].astype(BF16)
        w_ref, out_refs = out_refs[-1], out_refs[:-1]
    r = jnp.dot(x_ref[...].astype(BF16), w_ref[...].astype(BF16), preferred_element_type=F32)
    for o in out_refs:
        o[...] = r.astype(o.dtype)


def _matmul(x, w, l, out_dtypes, tm, tn, col0=0, n_cols=None, emit_w=False):
    m, kd = x.shape
    n = n_cols or w.shape[-1]
    tm, tn = min(tm, m), min(tn, n)
    assert m % tm == 0 and n % tn == 0 and col0 % tn == 0, (x.shape, w.shape, tm, tn, col0)
    assert not emit_w or m == tm
    j0 = col0 // tn
    wsz = w.dtype.itemsize
    vmem = 2 * (tm * kd * x.dtype.itemsize + kd * tn * wsz) + kd * tn * 2 * (3 if emit_w else 1) + 2 * tm * tn * 4
    vmem += sum(2 * tm * tn * jnp.dtype(d).itemsize for d in out_dtypes)
    out_specs = [pl.BlockSpec((tm, tn), lambda i, j: (i, j)) for _ in out_dtypes]
    out_shape = [jax.ShapeDtypeStruct((m, n), d) for d in out_dtypes]
    if emit_w:
        out_specs.append(pl.BlockSpec((kd, tn), lambda i, j: (0, j)))
        out_shape.append(jax.ShapeDtypeStruct((kd, n), BF16))
    return pl.pallas_call(
        functools.partial(_mm_body, emit_w=emit_w),
        grid=(m // tm, n // tn),
        in_specs=[pl.BlockSpec((tm, kd), lambda i, j: (i, 0)),
                  _w_spec(w, l, (kd, tn), lambda i, j: (0, j0 + j))],
        out_specs=out_specs,
        out_shape=out_shape,
        compiler_params=_cparams(vmem),
        name="matmul",
    )(x, w)


LN_ROWS = 128


def _residual_ln_rows(y_ref, res_ref, g_ref, b_ref, o_ref, ob_ref, rows):
    g = g_ref[...]
    b = b_ref[...]

    def chunk(c, carry):
        sl = pl.ds(pl.multiple_of(c * LN_ROWS, LN_ROWS), LN_ROWS)
        y = ALPHA * res_ref[sl, :] + y_ref[sl, :]
        mu = jnp.mean(y, axis=-1, keepdims=True)
        d = y - mu
        var = jnp.mean(d * d, axis=-1, keepdims=True)
        out = d * lax.rsqrt(var + LN_EPS) * g + b
        o_ref[sl, :] = out
        ob_ref[sl, :] = out.astype(BF16)
        return carry

    lax.fori_loop(0, rows // LN_ROWS, chunk, 0)


MM_LN_CHUNK = 512


def _mm_ln_body(x_ref, w_ref, res_ref, g_ref, b_ref, o_ref, ob_ref, *rest, n_k, tm, emit_w, w_rows):
    k = pl.program_id(1)
    x = x_ref[...]
    n = o_ref.shape[1]
    chunks = [slice(c, c + MM_LN_CHUNK) for c in range(0, n, MM_LN_CHUNK)]
    if emit_w:
        tk = w_ref.shape[0]
        row = k * tk + lax.broadcasted_iota(jnp.int32, (tk, 1), 0)
        (wb_ref,) = rest
        wb_ref[...] = jnp.where(row < w_rows, w_ref[...], 0.0).astype(BF16)
        w_ref = wb_ref

    @pl.when(k == 0)
    def _():
        for sl in chunks:
            o_ref[:, sl] = jnp.dot(x, w_ref[:, sl], preferred_element_type=F32)

    @pl.when(k > 0)
    def _():
        for sl in chunks:
            o_ref[:, sl] += jnp.dot(x, w_ref[:, sl], preferred_element_type=F32)

    @pl.when(k == n_k - 1)
    def _():
        _residual_ln_rows(o_ref, res_ref, g_ref, b_ref, o_ref, ob_ref, tm)


def _matmul_residual_ln(x, w, l, res, g, b, tm, tk, emit_w=False):
    m, kd = x.shape
    w_rows, n = w.shape[-2:]
    tm, tk = min(tm, m), min(tk, kd)
    assert m % tm == 0 and kd % tk == 0 and tm % LN_ROWS == 0 and n % MM_LN_CHUNK == 0
    assert (emit_w and m == tm) or w_rows == kd
    n_k = kd // tk
    vmem = 2 * (tm * tk * 2 + tk * n * w.dtype.itemsize + tm * n * 4 + tm * n * 4 + tm * n * 2)
    out_specs = [pl.BlockSpec((tm, n), lambda i, k: (i, 0)), pl.BlockSpec((tm, n), lambda i, k: (i, 0))]
    out_shape = [jax.ShapeDtypeStruct((m, n), F32), jax.ShapeDtypeStruct((m, n), BF16)]
    if emit_w:
        vmem += 3 * tk * n * 2
        out_specs.append(pl.BlockSpec((tk, n), lambda i, k: (k, 0)))
        out_shape.append(jax.ShapeDtypeStruct((kd, n), BF16))
    return pl.pallas_call(
        functools.partial(_mm_ln_body, n_k=n_k, tm=tm, emit_w=emit_w, w_rows=w_rows),
        grid=(m // tm, n_k),
        in_specs=[pl.BlockSpec((tm, tk), lambda i, k: (i, k)),
                  _w_spec(w, l, (tk, n), lambda i, k: (k, 0)),
                  pl.BlockSpec((tm, n), lambda i, k: (i, 0)),
                  pl.BlockSpec((1, n), lambda i, k: (0, 0)),
                  pl.BlockSpec((1, n), lambda i, k: (0, 0))],
        out_specs=out_specs,
        out_shape=out_shape,
        compiler_params=_cparams(vmem),
        name="matmul_residual_ln",
    )(x, w, res, g.reshape(1, n), b.reshape(1, n))


BAND_TQ = 128
BAND_KB = 5
ATTN_HB = 8


def _band_prompt_body(q_ref, k_ref, v_ref, bias_ref, o_ref):
    t = pl.program_id(2)
    heads = [slice(h * HEAD_DIM, (h + 1) * HEAD_DIM) for h in range(ATTN_HB)]
    first = t - (BAND_KB - 1)
    keys = [pl.ds(pl.multiple_of(jnp.maximum(first + jb, 0) * BAND_TQ, BAND_TQ), BAND_TQ) for jb in range(BAND_KB)]
    s = jnp.concatenate(
        [jnp.concatenate([_nt_dot(q_ref[:, hs], k_ref[keys[jb], hs]) for hs in heads], axis=0)
         for jb in range(BAND_KB)], axis=1)
    s = s * ATTN_SCALE + bias_ref[...].reshape(ATTN_HB * BAND_TQ, BAND_KB * BAND_TQ)
    col_block = lax.broadcasted_iota(jnp.int32, (1, BAND_KB * BAND_TQ), 1) // BAND_TQ
    s = jnp.where(first + col_block >= 0, s, NEG_INF)
    m = jnp.max(s, axis=-1, keepdims=True)
    p = jnp.exp(s - m)
    l = jnp.sum(p, axis=-1, keepdims=True)
    p = p.astype(BF16)
    for h, hs in enumerate(heads):
        rows = slice(h * BAND_TQ, (h + 1) * BAND_TQ)
        acc = sum(jnp.dot(p[rows, jb * BAND_TQ:(jb + 1) * BAND_TQ], v_ref[keys[jb], hs],
                          preferred_element_type=F32) for jb in range(BAND_KB))
        o_ref[:, hs] = (acc / l[rows]).astype(o_ref.dtype)


def _toeplitz_bias(rel_bias, n_rows, n_cols, offset):
    m = np.arange(n_rows + n_cols - 1) - (n_rows - 1)
    diag = rel_bias[:, np.clip(offset - m, -REL_CLIP, REL_CLIP) + REL_CLIP]
    rows = [diag[:, n_rows - 1 - i:n_rows - 1 - i + n_cols] for i in range(n_rows)]
    return jnp.stack(rows, axis=1).astype(F32)


def _band_attn_prompt(qkv, rel_bias, batch, seq, n_heads, q_col, k_col, v_col):
    assert n_heads % ATTN_HB == 0 and q_col % ATTN_HB == 0 and k_col % ATTN_HB == 0 and v_col % ATTN_HB == 0
    nt = seq // BAND_TQ
    hw = ATTN_HB * HEAD_DIM
    bias = _toeplitz_bias(rel_bias, BAND_TQ, BAND_KB * BAND_TQ, A_LEFT_CHUNKS * CHUNK)
    qi = np.arange(BAND_TQ)[:, None] // CHUNK
    kj = np.arange(BAND_KB * BAND_TQ)[None, :] // CHUNK
    in_band = (kj >= qi) & (kj <= qi + A_LEFT_CHUNKS)
    bias = jnp.where(jnp.asarray(in_band)[None], bias, NEG_INF)
    vmem = 2 * (2 * seq * hw * 2 + ATTN_HB * BAND_TQ * BAND_KB * BAND_TQ * 4) + (8 << 20)
    return pl.pallas_call(
        _band_prompt_body,
        grid=(batch, n_heads // ATTN_HB, nt),
        in_specs=[pl.BlockSpec((BAND_TQ, hw), lambda b, h, t: (b * nt + t, q_col // ATTN_HB + h)),
                  pl.BlockSpec((seq, hw), lambda b, h, t: (b, k_col // ATTN_HB + h)),
                  pl.BlockSpec((seq, hw), lambda b, h, t: (b, v_col // ATTN_HB + h)),
                  pl.BlockSpec((ATTN_HB, BAND_TQ, BAND_KB * BAND_TQ), lambda b, h, t: (h, 0, 0))],
        out_specs=pl.BlockSpec((BAND_TQ, hw), lambda b, h, t: (b * nt + t, h)),
        out_shape=jax.ShapeDtypeStruct((batch * seq, n_heads * HEAD_DIM), BF16),
        compiler_params=_cparams(vmem),
        name="band_attn_prompt",
    )(qkv, qkv, qkv, bias)


def _band_cached_body(q_ref, kc_ref, vc_ref, kn_ref, vn_ref, bias_ref, o_ref, *, n_heads, window):
    for h in range(n_heads):
        hs = slice(h * HEAD_DIM, (h + 1) * HEAD_DIM)
        cached = pl.ds(h, window, stride=n_heads)
        k = jnp.concatenate([kc_ref[cached, :].astype(BF16), kn_ref[:, hs]], axis=0)
        v = jnp.concatenate([vc_ref[cached, :].astype(BF16), vn_ref[:, hs]], axis=0)
        s = _nt_dot(q_ref[:, hs], k) * ATTN_SCALE + bias_ref[h]
        m = jnp.max(s, axis=-1, keepdims=True)
        p = jnp.exp(s - m)
        l = jnp.sum(p, axis=-1, keepdims=True)
        acc = jnp.dot(p.astype(BF16), v, preferred_element_type=F32)
        o_ref[:, hs] = (acc / l).astype(o_ref.dtype)


def _band_attn_cached(q2d, cache_k, cache_v, layer, new2d, rel_bias, batch, n_new, k_col, v_col):
    window, n_heads = cache_k.shape[2:4]
    hw = n_heads * HEAD_DIM
    n_keys = window + n_new
    bias = _toeplitz_bias(rel_bias, n_new, n_keys, window)
    flat = lambda c: c.reshape(c.shape[0], batch, window * n_heads, HEAD_DIM)
    cache_spec = pl.BlockSpec((None, None, window * n_heads, HEAD_DIM), lambda b: (layer, b, 0, 0))
    vmem = 2 * (2 * window * hw * 4 + n_heads * n_new * n_keys * 4) + (8 << 20)
    return pl.pallas_call(
        functools.partial(_band_cached_body, n_heads=n_heads, window=window),
        grid=(batch,),
        in_specs=[pl.BlockSpec((n_new, hw), lambda b: (b, 0)),
                  cache_spec, cache_spec,
                  pl.BlockSpec((n_new, hw), lambda b: (b, k_col)),
                  pl.BlockSpec((n_new, hw), lambda b: (b, v_col)),
                  pl.BlockSpec((n_heads, n_new, n_keys), lambda b: (0, 0, 0))],
        out_specs=pl.BlockSpec((n_new, hw), lambda b: (b, 0)),
        out_shape=jax.ShapeDtypeStruct((batch * n_new, hw), BF16),
        compiler_params=_cparams(vmem),
        name="band_attn_cached",
    )(q2d, flat(cache_k), flat(cache_v), new2d, new2d, bias)


def _roll_body(c_ref, new_ref, o_ref):
    keep = o_ref.shape[0] - new_ref.shape[0]
    o_ref[0:keep, :] = c_ref[new_ref.shape[0]:, :]
    o_ref[keep:, :] = new_ref[...]


def _rolled_band(cache, new_rows):
    n_layers, batch, window, n_heads, hd = cache.shape
    n_new = new_rows.shape[2]
    assert n_new <= window and (n_new * n_heads) % SUBLANES == 0
    spec = lambda rows: pl.BlockSpec((None, None, rows * n_heads, hd), lambda l, b: (l, b, 0, 0))
    out = pl.pallas_call(
        _roll_body,
        grid=(n_layers, batch),
        in_specs=[spec(window), spec(n_new)],
        out_specs=spec(window),
        out_shape=jax.ShapeDtypeStruct((n_layers, batch, window * n_heads, hd), cache.dtype),
        name="rolled_band",
    )(cache.reshape(n_layers, batch, window * n_heads, hd), new_rows.reshape(n_layers, batch, n_new * n_heads, hd))
    return out.reshape(cache.shape)


SB_TK = 256


def _split2(x):
    hi = x.astype(BF16)
    return hi, (x - hi.astype(F32)).astype(BF16)


def _sb_window(qs, ks, vs, u, mask, carry, acc, tq):
    z = jnp.concatenate([_nt_dot(q, k) for q, k in zip(qs, ks)], axis=0) * ATTN_SCALE
    sp = jnp.maximum(z, 0.0) + jnp.log(1.0 + jnp.exp(-jnp.abs(z)))
    log_1m = jnp.where(mask, -sp, 0.0)
    suffix = sum(jnp.dot(piece, u, preferred_element_type=F32) for piece in _split2(log_1m))
    w = jnp.where(mask, jnp.exp(z - sp + suffix + carry), 0.0).astype(BF16)
    acc = acc + jnp.concatenate(
        [jnp.dot(w[h * tq:(h + 1) * tq], v, preferred_element_type=F32) for h, v in enumerate(vs)], axis=0)
    return carry + jnp.sum(log_1m, axis=-1, keepdims=True), acc


def _sb_continue(c):
    limit, floor = c[0], c[1]
    return jnp.logical_and(limit > 0, floor > -SB_EXIT)


def _stick_break_body(q_ref, k_ref, v_ref, u_ref, o_ref, *, tq, qpos_base):
    qt = pl.program_id(2)
    qpos0 = qpos_base + qt * tq
    u = u_ref[...]
    rows = ATTN_HB * tq
    rowpos = qpos0 + lax.broadcasted_iota(jnp.int32, (rows, 1), 0) % tq
    top = (qpos0 + tq + LANES - 1) // LANES * LANES
    heads = [slice(h * HEAD_DIM, (h + 1) * HEAD_DIM) for h in range(ATTN_HB)]
    qs = [q_ref[:, hs] for hs in heads]

    def body(c):
        limit, _, carry, acc = c
        start = pl.multiple_of(jnp.maximum(limit - SB_TK, 0), LANES)
        keys = pl.ds(start, SB_TK)
        kpos = start + lax.broadcasted_iota(jnp.int32, (1, SB_TK), 1)
        mask = jnp.logical_and(kpos < rowpos, kpos < limit)
        carry, acc = _sb_window(qs, [k_ref[keys, hs] for hs in heads], [v_ref[keys, hs] for hs in heads],
                                u, mask, carry, acc, tq)
        return limit - SB_TK, jnp.max(carry), carry, acc

    init = (top, jnp.float32(0.0), jnp.zeros((rows, 1), F32), jnp.zeros((rows, HEAD_DIM), F32))
    acc = lax.while_loop(_sb_continue, body, init)[3]
    for h, hs in enumerate(heads):
        o_ref[:, hs] = acc[h * tq:(h + 1) * tq].astype(o_ref.dtype)


def _stick_break_cached_body(q_ref, kc_ref, vc_ref, kn_ref, vn_ref, u_ref, o_ref, *, n_new, past, n_heads):
    hg = pl.program_id(1)
    u = u_ref[...]
    rows = ATTN_HB * n_new
    rowpos = past + lax.broadcasted_iota(jnp.int32, (rows, 1), 0) % n_new
    heads = [slice(h * HEAD_DIM, (h + 1) * HEAD_DIM) for h in range(ATTN_HB)]
    qs = [q_ref[:, hs] for hs in heads]
    first = past - (SB_TK - LANES)
    filler = jnp.zeros((LANES - n_new, HEAD_DIM), BF16)

    def cached(ref, start, size, h):
        row0 = start * n_heads + hg * ATTN_HB + h
        return ref[pl.ds(row0, size, stride=n_heads), :].astype(BF16)

    def newest(cache_ref, new_ref):
        return [jnp.concatenate([cached(cache_ref, first, LANES, h), new_ref[:, hs], filler], axis=0)
                for h, hs in enumerate(heads)]

    kpos = first + lax.broadcasted_iota(jnp.int32, (1, SB_TK), 1)
    carry, acc = _sb_window(qs, newest(kc_ref, kn_ref), newest(vc_ref, vn_ref), u, kpos < rowpos,
                            jnp.zeros((rows, 1), F32), jnp.zeros((rows, HEAD_DIM), F32), n_new)

    def body(c):
        limit, _, carry, acc = c
        start = jnp.maximum(limit - SB_TK, 0)
        mask = jnp.broadcast_to(start + lax.broadcasted_iota(jnp.int32, (1, SB_TK), 1) < limit, (rows, SB_TK))
        carry, acc = _sb_window(qs, [cached(kc_ref, start, SB_TK, h) for h in range(ATTN_HB)],
                                [cached(vc_ref, start, SB_TK, h) for h in range(ATTN_HB)],
                                u, mask, carry, acc, n_new)
        return limit - SB_TK, jnp.max(carry), carry, acc

    acc = lax.while_loop(_sb_continue, body, (jnp.int32(first), jnp.max(carry), carry, acc))[3]
    for h, hs in enumerate(heads):
        o_ref[:, hs] = acc[h * n_new:(h + 1) * n_new].astype(o_ref.dtype)


def _stick_break_cached(q2d, cache_k, cache_v, layer, new2d, batch, n_new, q_col, k_col, v_col):
    past, n_heads = cache_k.shape[2:4]
    assert past % LANES == 0 and past >= SB_TK and n_new <= LANES and n_heads % ATTN_HB == 0
    assert q_col % ATTN_HB == 0 and k_col % ATTN_HB == 0 and v_col % ATTN_HB == 0
    hw = ATTN_HB * HEAD_DIM
    flat = lambda c: c.reshape(c.shape[0], batch, past * n_heads, HEAD_DIM)
    j = np.arange(SB_TK)
    u = jnp.asarray(j[:, None] > j[None, :], BF16)
    cache_spec = pl.BlockSpec((None, None, past * n_heads, HEAD_DIM), lambda b, h: (layer, b, 0, 0))
    vmem = 2 * (2 * past * n_heads * HEAD_DIM * 4) + (8 << 20)
    return pl.pallas_call(
        functools.partial(_stick_break_cached_body, n_new=n_new, past=past, n_heads=n_heads),
        grid=(batch, n_heads // ATTN_HB),
        in_specs=[pl.BlockSpec((n_new, hw), lambda b, h: (b, q_col // ATTN_HB + h)),
                  cache_spec, cache_spec,
                  pl.BlockSpec((n_new, hw), lambda b, h: (b, k_col // ATTN_HB + h)),
                  pl.BlockSpec((n_new, hw), lambda b, h: (b, v_col // ATTN_HB + h)),
                  pl.BlockSpec((SB_TK, SB_TK), lambda b, h: (0, 0))],
        out_specs=pl.BlockSpec((n_new, hw), lambda b, h: (b, h)),
        out_shape=jax.ShapeDtypeStruct((batch * n_new, n_heads * HEAD_DIM), BF16),
        compiler_params=_cparams(vmem),
        name="stick_break_cached",
    )(q2d, flat(cache_k), flat(cache_v), new2d, new2d, u)


def _stick_break_attn(q2d, k2d, v2d, batch, n_q, n_k, n_heads, tq, qpos_base, q_col, k_col, v_col):
    assert n_q % tq == 0 and n_k % LANES == 0 and n_k >= SB_TK and n_heads % ATTN_HB == 0
    assert q_col % ATTN_HB == 0 and k_col % ATTN_HB == 0 and v_col % ATTN_HB == 0
    assert -(-(qpos_base + n_q) // LANES) * LANES <= n_k
    nt = n_q // tq
    hw = ATTN_HB * HEAD_DIM
    j = np.arange(SB_TK)
    u = jnp.asarray(j[:, None] > j[None, :], BF16)
    vmem = 2 * (2 * n_k * hw * 2) + (8 << 20)
    return pl.pallas_call(
        functools.partial(_stick_break_body, tq=tq, qpos_base=qpos_base),
        grid=(batch, n_heads // ATTN_HB, nt),
        in_specs=[pl.BlockSpec((tq, hw), lambda b, h, t: (b * nt + t, q_col // ATTN_HB + h)),
                  pl.BlockSpec((n_k, hw), lambda b, h, t: (b, k_col // ATTN_HB + h)),
                  pl.BlockSpec((n_k, hw), lambda b, h, t: (b, v_col // ATTN_HB + h)),
                  pl.BlockSpec((SB_TK, SB_TK), lambda b, h, t: (0, 0))],
        out_specs=pl.BlockSpec((tq, hw), lambda b, h, t: (b * nt + t, h)),
        out_shape=jax.ShapeDtypeStruct((batch * n_q, n_heads * HEAD_DIM), BF16),
        compiler_params=_cparams(vmem),
        name="stick_break_attn",
    )(q2d, k2d, v2d, u)


def _sortable(x):
    bits = pltpu.bitcast(x, jnp.int32)
    return jnp.where(bits < 0, bits ^ jnp.int32(0x7FFFFFFF), bits)


def _dsa_index_body(qi_ref, wi_ref, kz_ref, bias_ref, key_ref, *, tq, tk, n_real, qpos_base, topk):
    n_pad = key_ref.shape[1]
    qt = pl.program_id(1)
    qpos0 = qpos_base + qt * tq
    rowpos = qpos0 + lax.broadcasted_iota(jnp.int32, (tq, 1), 0)
    qchunk = rowpos // CHUNK
    n_kb = jnp.minimum((((qpos0 + tq - 1) // CHUNK + 1) * CHUNK + tk - 1) // tk, n_pad // tk)
    n_tail = jnp.maximum(n_real - n_kb * tk, 0)
    tail0 = n_kb * tk

    w = wi_ref[...] * (IDX_SCALE * IDX_W_SCALE)

    def block_pos(kb):
        start = pl.multiple_of(kb * tk, tk)
        return start, start + lax.broadcasted_iota(jnp.int32, (1, tk), 1)

    def score_block(kb, carry):
        start, kpos = block_pos(kb)
        kz = kz_ref[pl.ds(start, tk), :]
        acc = jnp.zeros((tq, tk), F32)
        for pair in range(IDX_HEADS // 2):
            qp = qi_ref[:, pair * LANES:(pair + 1) * LANES]
            for half in range(2):
                h = 2 * pair + half
                sc = _nt_dot(qp, kz[:, half * LANES:(half + 1) * LANES])
                acc = acc + jnp.maximum(sc, 0.0) * w[:, h:h + 1]
        key = _sortable(jnp.where(kpos // CHUNK <= qchunk, acc, NEG_INF))
        key_ref[:, pl.ds(start, tk)] = jnp.where(kpos < n_real, key, jnp.int32(INT_MIN))
        return carry

    lax.fori_loop(0, n_kb, score_block, 0)

    def count(pred):
        def blk(kb, acc):
            start, kpos = block_pos(kb)
            hit = pred(key_ref[:, pl.ds(start, tk)], kpos).astype(F32)
            return acc + sum(hit[:, c:c + LANES] for c in range(0, tk, LANES))
        lanes = lax.fori_loop(0, n_kb, blk, jnp.zeros((tq, LANES), F32))
        return jnp.sum(lanes, axis=-1, keepdims=True)

    kf = jnp.float32(topk)
    tail_f = n_tail.astype(F32)
    neg_key = jnp.int32(NEG_KEY)

    def thr_step(i, ans_u):
        cand_u = ans_u | lax.shift_left(jnp.int32(1), 31 - i)
        cand = cand_u ^ jnp.int32(INT_MIN)
        cnt = count(lambda key, kpos: key >= cand) + jnp.where(neg_key >= cand, tail_f, 0.0)
        return jnp.where(cnt >= kf, cand_u, ans_u)

    thr = lax.fori_loop(0, 32, thr_step, jnp.zeros((tq, 1), jnp.int32)) ^ jnp.int32(INT_MIN)
    tail_gt = jnp.where(neg_key > thr, tail_f, 0.0)
    n_gt = count(lambda key, kpos: key > thr) + tail_gt
    n_eq = count(lambda key, kpos: key == thr) + jnp.where(neg_key == thr, tail_f, 0.0)
    need = kf - n_gt

    n_bits = int(np.ceil(np.log2(n_pad)))

    def tie_search():
        def pos_step(i, lo):
            cand = lo + lax.shift_left(jnp.int32(1), n_bits - 1 - i)
            in_tail = jnp.clip(cand - tail0, 0, n_tail).astype(F32)
            cnt = (count(lambda key, kpos: jnp.logical_and(key == thr, kpos < cand))
                   + jnp.where(neg_key == thr, in_tail, 0.0))
            return jnp.where(cnt < need, cand, lo)
        return lax.fori_loop(0, n_bits, pos_step, jnp.zeros((tq, 1), jnp.int32))

    has_ties = jnp.max(jnp.where(n_eq > need, 1.0, 0.0)) > 0.0
    last = lax.cond(has_ties, tie_search, lambda: jnp.full((tq, 1), n_pad, jnp.int32))

    bias_ref[0] = jnp.full(bias_ref.shape[1:], NEG_INF, bias_ref.dtype)

    def write_block(kb, carry):
        start, kpos = block_pos(kb)
        key = key_ref[:, pl.ds(start, tk)]
        sel = jnp.logical_or(key > thr, jnp.logical_and(key == thr, kpos <= last))
        ok = jnp.logical_and(jnp.logical_and(sel, kpos // CHUNK <= qchunk), kpos < n_real)
        bias_ref[0, :, pl.ds(start, tk)] = jnp.where(ok, 0.0, NEG_INF).astype(bias_ref.dtype)
        return carry

    lax.fori_loop(0, n_kb, write_block, 0)


def _dsa_index(qi2d, wi, kz, batch, n_q, n_k, n_real, tq, tk, qpos_base, qi_col):
    nt = n_q // tq
    topk = min(TOPK_MAX, n_real // 4)
    vmem = 2 * (tq * 2048 * 2 + n_k * 256 * 2 + tq * n_k * 2) + tq * n_k * 4 + 8 * tq * tk * 4 + (4 << 20)
    return pl.pallas_call(
        functools.partial(_dsa_index_body, tq=tq, tk=tk, n_real=n_real, qpos_base=qpos_base, topk=topk),
        grid=(batch, nt),
        in_specs=[pl.BlockSpec((tq, IDX_HEADS * IDX_DIM), lambda b, t: (b * nt + t, qi_col)),
                  pl.BlockSpec((tq, IDX_HEADS), lambda b, t: (b * nt + t, 0)),
                  pl.BlockSpec((n_k, 2 * LANES), lambda b, t: (b, 0))],
        out_specs=pl.BlockSpec((1, tq, n_k), lambda b, t: (b, t, 0)),
        out_shape=jax.ShapeDtypeStruct((batch, n_q, n_k), BF16),
        scratch_shapes=[pltpu.VMEM((tq, n_k), jnp.int32)],
        compiler_params=_cparams(vmem),
        name="dsa_index",
    )(qi2d, wi, kz)


def _dsa_attn_body(*refs, tq, tk, qpos_base, cached):
    if cached:
        (q_ref, kc_ref, vc_ref, kn_ref, vn_ref, bias_ref, slope_ref, o_ref,
         s0_ref, s1_ref, p0_ref, p1_ref, k_ref, v_ref) = refs
        past, n_kv = cached
        n_new = kn_ref.shape[0]
        own = pl.ds(pl.program_id(1), past, stride=n_kv)
        for full_ref, cache_ref, new_ref in ((k_ref, kc_ref, kn_ref), (v_ref, vc_ref, vn_ref)):
            full_ref[0:past, :] = cache_ref[own, :].astype(BF16)
            full_ref[past:past + n_new, :] = new_ref[...]
            full_ref[past + n_new:, :] = jnp.zeros((full_ref.shape[0] - past - n_new, HEAD_DIM), BF16)
    else:
        q_ref, k_ref, v_ref, bias_ref, slope_ref, o_ref, s0_ref, s1_ref, p0_ref, p1_ref = refs
    s_refs, p_refs = (s0_ref, s1_ref), (p0_ref, p1_ref)
    n_pad = k_ref.shape[0]
    qt = pl.program_id(2)
    qpos0 = qpos_base + qt * tq
    tile_end = qpos0 + tq
    rows = C_GROUP * tq
    rowpos = qpos0 + lax.broadcasted_iota(jnp.int32, (tq, 1), 0)
    n_kb = jnp.minimum((((qpos0 + tq - 1) // CHUNK + 1) * CHUNK + tk - 1) // tk, n_pad // tk)
    qs = jnp.concatenate([q_ref[:, g * HEAD_DIM:(g + 1) * HEAD_DIM] for g in range(C_GROUP)], axis=0)

    def key_rows(kb):
        return pl.ds(pl.multiple_of(kb * tk, tk), tk)

    def scores(kb, slot):
        s_refs[slot][...] = _nt_dot(qs, k_ref[key_rows(kb), :])

    def weighted_values(kb, slot):
        return jnp.dot(p_refs[slot][...], v_ref[key_rows(jnp.maximum(kb, 0)), :], preferred_element_type=F32)

    slope = slope_ref[0]

    def softmax(kb, slot, m, l, has_later_keys):
        kpos = kb * tk + lax.broadcasted_iota(jnp.int32, (1, tk), 1)
        back = (tile_end - kpos).astype(F32)
        s = (s_refs[slot][...] * ATTN_SCALE - slope * back).reshape(C_GROUP, tq, tk)
        s = s + bias_ref[0, :, key_rows(kb)].astype(F32)[None]
        if has_later_keys:
            ahead = jnp.where(kpos > rowpos, (rowpos - kpos).astype(F32), 0.0)
            s = s + (2.0 * slope).reshape(C_GROUP, tq, 1) * ahead[None]
        s = s.reshape(rows, tk)
        m_new = jnp.maximum(m, jnp.max(s, axis=-1, keepdims=True))
        p = jnp.exp(s - m_new)
        p_refs[slot][...] = p.astype(BF16)
        a = jnp.exp(m - m_new)
        return m_new, a * l + jnp.sum(p, axis=-1, keepdims=True), a

    def step(kb, slot, state):
        m, l, a_prev, acc = state
        acc = a_prev * acc + weighted_values(kb - 1, 1 - slot)
        m, l, a = softmax(kb, slot, m, l, False)
        scores(kb + 1, 1 - slot)
        return m, l, a, acc

    def finish(kb, slot, state):
        m, l, a_prev, acc = state
        acc = a_prev * acc + weighted_values(kb - 1, 1 - slot)
        m, l, a = softmax(kb, slot, m, l, True)
        return a * acc + weighted_values(kb, slot), l

    p_refs[1][...] = jnp.zeros(p_refs[1].shape, BF16)
    scores(0, 0)
    init = (jnp.full((rows, 1), 0.5 * NEG_INF, F32), jnp.zeros((rows, 1), F32),
            jnp.ones((rows, 1), F32), jnp.zeros((rows, HEAD_DIM), F32))
    last = n_kb - 1
    state = lax.fori_loop(0, last // 2, lambda j, st: step(2 * j + 1, 1, step(2 * j, 0, st)), init)
    acc, l = lax.cond(last % 2 == 1,
                      lambda st: finish(last, 1, step(last - 1, 0, st)),
                      lambda st: finish(last, 0, st), state)
    o = (acc / l).astype(o_ref.dtype)
    for g in range(C_GROUP):
        o_ref[:, g * HEAD_DIM:(g + 1) * HEAD_DIM] = o[g * tq:(g + 1) * tq]


def _dsa_attn(q2d, k2d, v2d, bias, batch, n_q, n_k, n_kv, tq, tk, qpos_base, k_col, v_col, cache=None):
    assert tk % tq == 0 and qpos_base % tq == 0 and n_k % tk == 0
    nt = n_q // tq
    gw = C_GROUP * HEAD_DIM
    n_heads = n_kv * C_GROUP
    slopes = 2.0 ** (-8.0 * np.arange(1, n_heads + 1) / n_heads)
    slope_rows = np.repeat(slopes.reshape(n_kv, C_GROUP), tq, axis=1)[..., None].astype(np.float32)
    rows = C_GROUP * tq
    vmem = (2 * (2 * n_k * HEAD_DIM * 2 + tq * n_k * 2 + 2 * tq * gw * 2 + rows * LANES * 4)
            + 2 * rows * tk * 6 + 8 * rows * tk * 4 + (4 << 20))
    scratch = [pltpu.VMEM((rows, tk), F32)] * 2 + [pltpu.VMEM((rows, tk), BF16)] * 2
    if cache is None:
        cached = None
        kv_specs = [pl.BlockSpec((n_k, HEAD_DIM), lambda b, n, t: (b, k_col + n)),
                    pl.BlockSpec((n_k, HEAD_DIM), lambda b, n, t: (b, v_col + n))]
        kv_args = [k2d, v2d]
    else:
        cache_k, cache_v, layer = cache
        past = cache_k.shape[2]
        assert nt == 1 and qpos_base == past and past + n_q <= n_k and cache_k.shape[3] == n_kv
        cached = (past, n_kv)
        flat = lambda c: c.reshape(c.shape[0], batch, past * n_kv, HEAD_DIM)
        cache_spec = pl.BlockSpec((None, None, past * n_kv, HEAD_DIM), lambda b, n, t: (layer, b, 0, 0))
        kv_specs = [cache_spec, cache_spec,
                    pl.BlockSpec((n_q, HEAD_DIM), lambda b, n, t: (b, k_col + n)),
                    pl.BlockSpec((n_q, HEAD_DIM), lambda b, n, t: (b, v_col + n))]
        kv_args = [flat(cache_k), flat(cache_v), k2d, v2d]
        scratch += [pltpu.VMEM((n_k, HEAD_DIM), BF16), pltpu.VMEM((n_k, HEAD_DIM), BF16)]
        vmem += 2 * 2 * past * n_kv * HEAD_DIM * 4
    return pl.pallas_call(
        functools.partial(_dsa_attn_body, tq=tq, tk=tk, qpos_base=qpos_base, cached=cached),
        grid=(batch, n_kv, nt),
        in_specs=[pl.BlockSpec((tq, gw), lambda b, n, t: (b * nt + t, n))] + kv_specs + [
            pl.BlockSpec((1, tq, n_k), lambda b, n, t: (b, t, 0)),
            pl.BlockSpec((1, rows, 1), lambda b, n, t: (n, 0, 0))],
        out_specs=pl.BlockSpec((tq, gw), lambda b, n, t: (b * nt + t, n)),
        out_shape=jax.ShapeDtypeStruct((batch * n_q, n_heads * HEAD_DIM), BF16),
        scratch_shapes=scratch,
        compiler_params=_cparams(vmem),
        name="dsa_attn",
    )(q2d, *kv_args, bias, jnp.asarray(slope_rows))


def _mem_attn_body(x_ref, wq_ref, mk_ref, mv_ref, wo_ref, g_ref, b_ref, o_ref, ob_ref, y_ref, *,
                   tm, rows_per_seq, n_mem):
    q = jnp.dot(x_ref[...].astype(BF16), wq_ref[...], preferred_element_type=F32).astype(BF16)
    n_seq = tm // rows_per_seq
    heads = [slice(h * HEAD_DIM, (h + 1) * HEAD_DIM) for h in range(MEM_HEADS)]
    s = jnp.concatenate(
        [_nt_dot(q[s_i * rows_per_seq:(s_i + 1) * rows_per_seq, hs],
                 mk_ref[s_i * n_mem:(s_i + 1) * n_mem, hs].astype(BF16))
         for hs in heads for s_i in range(n_seq)], axis=0) * ATTN_SCALE
    m = jnp.max(s, axis=-1, keepdims=True)
    p = jnp.exp(s - m)
    l = jnp.sum(p, axis=-1, keepdims=True)
    p = p.astype(BF16)
    o_heads = []
    for h, hs in enumerate(heads):
        pieces = []
        for s_i in range(n_seq):
            rows = slice(h * tm + s_i * rows_per_seq, h * tm + (s_i + 1) * rows_per_seq)
            o = jnp.dot(p[rows], mv_ref[s_i * n_mem:(s_i + 1) * n_mem, hs].astype(BF16),
                        preferred_element_type=F32)
            pieces.append((o / l[rows]).astype(BF16))
        o_heads.append(pieces[0] if n_seq == 1 else jnp.concatenate(pieces, axis=0))
    y_ref[...] = jnp.dot(jnp.concatenate(o_heads, axis=-1), wo_ref[...], preferred_element_type=F32)
    _residual_ln_rows(y_ref, x_ref, g_ref, b_ref, o_ref, ob_ref, tm)


def _mem_attn_ln(x, wq, wo, l, mk, mv, g, b, tm, rows_per_seq, seq_rows):
    m, d = x.shape
    mw = wq.shape[-1]
    n_seq_tile = tm // rows_per_seq
    n_mem = mk.shape[0] // (m // seq_rows)
    if n_seq_tile == 1:
        mem_map = lambda i: ((i * tm) // seq_rows, 0)
    else:
        assert rows_per_seq == seq_rows
        mem_map = lambda i: (i, 0)
    vmem = 2 * (tm * d * 4 * 2 + tm * d * 2 + 2 * d * mw * 2 + 2 * n_seq_tile * n_mem * mw * 4) + tm * d * 4
    out_specs = [pl.BlockSpec((tm, d), lambda i: (i, 0)), pl.BlockSpec((tm, d), lambda i: (i, 0))]
    out_shape = [jax.ShapeDtypeStruct((m, d), F32), jax.ShapeDtypeStruct((m, d), BF16)]
    return pl.pallas_call(
        functools.partial(_mem_attn_body, tm=tm, rows_per_seq=rows_per_seq, n_mem=n_mem),
        grid=(m // tm,),
        in_specs=[pl.BlockSpec((tm, d), lambda i: (i, 0)),
                  _w_spec(wq, l, (d, mw), lambda i: (0, 0)),
                  pl.BlockSpec((n_seq_tile * n_mem, mw), mem_map),
                  pl.BlockSpec((n_seq_tile * n_mem, mw), mem_map),
                  _w_spec(wo, l, (mw, d), lambda i: (0, 0)),
                  pl.BlockSpec((1, d), lambda i: (0, 0)),
                  pl.BlockSpec((1, d), lambda i: (0, 0))],
        out_specs=out_specs,
        out_shape=out_shape,
        scratch_shapes=[pltpu.VMEM((tm, d), F32)],
        compiler_params=_cparams(vmem),
        name="mem_attn_ln",
    )(x, wq, mk, mv, wo, g.reshape(1, d), b.reshape(1, d))


FFN_TN = 256
FFN_SUB = 256
FFN_TM = 2048
HALO = 8


def _gelu(x):
    return 0.5 * x * (1.0 + lax.erf(x * np.float32(2.0 ** -0.5)))


def _ffn_up_body(x_ref, wa_ref, wg_ref, cwa_ref, cwg_ref, cba_ref, cbg_ref, *rest,
                 tm, rows_per_seq, tiles_per_seq, carried, nj, emit_w):
    if carried:
        act_ref, sta_ref, stg_ref, carry_ref, ext_ref, ext2_ref = rest
    elif emit_w:
        pa_ref, pg_ref, act_ref, sta_ref, stg_ref, wab_ref, wgb_ref, ext_ref = rest
    else:
        pa_ref, pg_ref, act_ref, sta_ref, stg_ref, ext_ref = rest
    i = pl.program_id(0)
    j = pl.program_id(1)

    def conv_gate(base, rows, buf=None):
        buf = ext_ref if buf is None else buf
        cw = jnp.concatenate([cwa_ref[...], cwg_ref[...]], axis=1)
        cb = jnp.concatenate([cba_ref[...], cbg_ref[...]], axis=1)
        h = (buf[base:base + rows, :] * cw[2:3, :]
             + buf[base - 1:base - 1 + rows, :] * cw[1:2, :]
             + buf[base - 2:base - 2 + rows, :] * cw[0:1, :]
             + cb)
        return (h[:, :FFN_TN] * _gelu(h[:, FFN_TN:])).astype(act_ref.dtype)

    def put_state(s_i, last2):
        sta_ref[s_i] = last2[:, :FFN_TN]
        stg_ref[s_i] = last2[:, FFN_TN:]

    @pl.when(j < nj)
    def _():
        if carried:
            @pl.when(i % tiles_per_seq == 0)
            def _():
                carry_ref[j] = jnp.zeros((2, 2 * FFN_TN), F32)

            sub = min(FFN_SUB, tm)
            bufs = (ext_ref, ext2_ref)
            last2 = carry_ref[j]
            for n, r in enumerate(range(0, tm, sub)):
                buf = bufs[n % 2]
                buf[HALO - 2:HALO, :] = last2
                buf[HALO:HALO + sub, :FFN_TN] = jnp.dot(x_ref[r:r + sub, :], wa_ref[...],
                                                        preferred_element_type=F32)
                buf[HALO:HALO + sub, FFN_TN:] = jnp.dot(x_ref[r:r + sub, :], wg_ref[...],
                                                        preferred_element_type=F32)
                act_ref[r:r + sub, :] = conv_gate(HALO, sub, buf)
                last2 = buf[HALO + sub - 2:HALO + sub, :]
            carry_ref[j] = last2
            put_state(0, last2)
        else:
            wa, wg = wa_ref, wg_ref
            if emit_w:
                wab_ref[...] = wa_ref[...].astype(BF16)
                wgb_ref[...] = wg_ref[...].astype(BF16)
                wa, wg = wab_ref, wgb_ref
            up = jnp.concatenate([jnp.dot(x_ref[...], wa[...], preferred_element_type=F32),
                                  jnp.dot(x_ref[...], wg[...], preferred_element_type=F32)], axis=1)
            for s_i in range(tm // rows_per_seq):
                base = s_i * (rows_per_seq + HALO) + HALO
                rsl = slice(s_i * rows_per_seq, (s_i + 1) * rows_per_seq)
                ext_ref[base - 2:base, :] = jnp.concatenate([pa_ref[s_i], pg_ref[s_i]], axis=1)
                ext_ref[base:base + rows_per_seq, :] = up[rsl]
                put_state(s_i, ext_ref[base + rows_per_seq - 2:base + rows_per_seq, :])
                act_ref[rsl, :] = conv_gate(base, rows_per_seq)

    @pl.when(j >= nj)
    def _():
        act_ref[...] = jnp.zeros_like(act_ref)


def _ffn_up(x, w_val, w_gate, l, conv_w, conv_b, prev, tm, rows_per_seq, seq_rows, n_act_cols, emit_w=False):
    m, d = x.shape
    f2 = conv_w.shape[1]
    wt = 2 * FFN_TN
    assert f2 % wt == 0 and n_act_cols % FFN_TN == 0
    nj = f2 // wt
    nj_pad = n_act_cols // FFN_TN
    carried = prev is None
    assert not (emit_w and carried)
    n_seq = tm // rows_per_seq
    tiles_per_seq = max(seq_rows // tm, 1)
    n_tiles = m // tm
    val = lambda j: jnp.minimum(j, nj - 1)
    gate0 = 0 if l is None else nj
    in_specs = [pl.BlockSpec((tm, d), lambda i, j: (i, 0)),
                _w_spec(w_val, l, (d, FFN_TN), lambda i, j: (0, val(j))),
                _w_spec(w_gate, l, (d, FFN_TN), lambda i, j: (0, val(j) + gate0)),
                pl.BlockSpec((CONV_W, FFN_TN), lambda i, j: (0, val(j))),
                pl.BlockSpec((CONV_W, FFN_TN), lambda i, j: (0, val(j) + nj)),
                pl.BlockSpec((1, FFN_TN), lambda i, j: (0, val(j))),
                pl.BlockSpec((1, FFN_TN), lambda i, j: (0, val(j) + nj))]
    args = [x, w_val, w_gate, conv_w, conv_w, conv_b, conv_b]
    scratch = []
    if carried:
        assert n_seq == 1
        scratch.append(pltpu.VMEM((nj, 2, wt), F32))
        scratch += [pltpu.VMEM((min(FFN_SUB, tm) + HALO, wt), F32)] * 2
    else:
        in_specs += [pl.BlockSpec((n_seq, 2, FFN_TN), lambda i, j: (i, 0, val(j))),
                     pl.BlockSpec((n_seq, 2, FFN_TN), lambda i, j: (i, 0, val(j) + nj))]
        args += [prev, prev]
        scratch.append(pltpu.VMEM((n_seq * (rows_per_seq + HALO), wt), F32))
    st_shape = jax.ShapeDtypeStruct((n_tiles * n_seq, 2, f2 // 2), F32)
    st_spec = pl.BlockSpec((n_seq, 2, FFN_TN), lambda i, j: (i, 0, val(j)))
    vmem = (2 * (tm * d * 2 + d * wt * w_val.dtype.itemsize + tm * FFN_TN * 2) + tm * wt * 4
            + 8 * min(tm, FFN_SUB) * wt * 4 + (4 << 20))
    out_specs = [pl.BlockSpec((tm, FFN_TN), lambda i, j: (i, j)), st_spec, st_spec]
    out_shape = [jax.ShapeDtypeStruct((m, n_act_cols), BF16), st_shape, st_shape]
    if emit_w:
        vmem += 2 * d * wt * 2
        out_specs += [pl.BlockSpec((d, FFN_TN), lambda i, j: (0, val(j)))] * 2
        out_shape += [jax.ShapeDtypeStruct((d, f2 // 2), BF16)] * 2
    act, st_a, st_g, *w_images = pl.pallas_call(
        functools.partial(_ffn_up_body, tm=tm, rows_per_seq=rows_per_seq, tiles_per_seq=tiles_per_seq,
                          carried=carried, nj=nj, emit_w=emit_w),
        grid=(n_tiles, nj_pad),
        in_specs=in_specs,
        out_specs=out_specs,
        out_shape=out_shape,
        scratch_shapes=scratch,
        compiler_params=_cparams(vmem),
        name="ffn_up_conv_gate",
    )(*args)
    return (act, jnp.concatenate([st_a, st_g], axis=-1), *w_images)


def _pad_rows(a, n_rows):
    return jnp.pad(a, ((0, 0), (0, n_rows - a.shape[1]), (0, 0)))


def kernel(x_prompt, x_sample, cache_a_k, cache_a_v, cache_b_k, cache_b_v, cache_c_k, cache_c_v, cache_c_idx, cache_mem_k, cache_mem_v, state_ffn_conv, mem_prompt, w_in_even, w_out_even, rel_bias_a, w_in_odd, w_out_odd, w_mem_q, w_mem_kv, w_mem_o, w_up, conv_w, conv_b, w_down, ln_g, ln_b):
    bp, sp, d = x_prompt.shape
    bs, n_new, _ = x_sample.shape
    depth = w_up.shape[0]
    h_a = cache_a_k.shape[3]
    h_b = cache_b_k.shape[3]
    n_kv = cache_c_k.shape[3]
    h_c = n_kv * C_GROUP
    hw_a, hw_b, kvw = h_a * HEAD_DIM, h_b * HEAD_DIM, n_kv * HEAD_DIM
    qiw = IDX_HEADS * IDX_DIM
    a_win_s = cache_a_k.shape[2]
    a_win_p = min(A_LEFT_CHUNKS * CHUNK, sp)
    past = cache_b_k.shape[2]
    n_mem = mem_prompt.shape[1]
    mem_w = w_mem_q.shape[2]
    d_ff = w_down.shape[1]
    mp, ms = bp * sp, bs * n_new
    l_s = past + n_new
    l_s_pad = -(-l_s // LANES) * LANES
    main_odd = h_c * HEAD_DIM + 2 * kvw + qiw
    tail_pad = LANES
    ff_pad = -(-d_ff // 1024) * 1024

    wmq = w_mem_q.astype(BF16)
    wmo = w_mem_o.astype(BF16)

    xp = x_prompt.reshape(mp, d)
    xs = x_sample.reshape(ms, d)
    xp_b, xs_b = xp.astype(BF16), xs.astype(BF16)
    mem2d = mem_prompt.reshape(bp * n_mem, d).astype(BF16)
    cmk = cache_mem_k.reshape(depth, bs * n_mem, mem_w).astype(BF16)
    cmv = cache_mem_v.reshape(depth, bs * n_mem, mem_w).astype(BF16)

    outs = {k: [] for k in ("akp", "avp", "bkp", "bvp", "ckp", "cvp", "cip", "mkp", "mvp", "fp",
                            "aks", "avs", "bks", "bvs", "cks", "cvs", "cis", "fs")}

    for l in range(depth):
        i = l // 2
        if l % 2 == 0:
            w_out = w_out_even
            sf, sb, w_in_l = _matmul(xs_b, w_in_even, i, (F32, BF16), 256, 512, emit_w=True)
            pf, pb = _matmul(xp_b, w_in_l, None, (F32, BF16), 1024, 512)
            oa = _band_attn_prompt(pb, rel_bias_a[i], bp, sp, h_a, 0, h_a, 2 * h_a)
            cb0 = 3 * h_a
            ob = _stick_break_attn(pb, pb, pb, bp, sp, sp, h_b, 128, 0, cb0, cb0 + h_b, cb0 + 2 * h_b)
            mix_p = jnp.concatenate([oa, ob], axis=-1)
            pf4 = pf.reshape(bp, sp, -1)
            outs["akp"].append(pf4[:, sp - a_win_p:, hw_a:2 * hw_a].reshape(bp, a_win_p, h_a, HEAD_DIM))
            outs["avp"].append(pf4[:, sp - a_win_p:, 2 * hw_a:3 * hw_a].reshape(bp, a_win_p, h_a, HEAD_DIM))
            o0 = 3 * hw_a
            outs["bkp"].append(pf4[:, :, o0 + hw_b:o0 + 2 * hw_b].reshape(bp, sp, h_b, HEAD_DIM))
            outs["bvp"].append(pf4[:, :, o0 + 2 * hw_b:o0 + 3 * hw_b].reshape(bp, sp, h_b, HEAD_DIM))
            sf4 = sf.reshape(bs, n_new, -1)
            oa_s = _band_attn_cached(sb, cache_a_k, cache_a_v, i, sb, rel_bias_a[i], bs, n_new, 1, 2)
            kb_new = sf4[:, :, o0 + hw_b:o0 + 2 * hw_b]
            vb_new = sf4[:, :, o0 + 2 * hw_b:o0 + 3 * hw_b]
            ob_s = _stick_break_cached(sb, cache_b_k, cache_b_v, i, sb, bs, n_new, cb0, cb0 + h_b, cb0 + 2 * h_b)
            mix_s = jnp.concatenate([oa_s, ob_s], axis=-1)
            outs["aks"].append(sf4[:, :, hw_a:2 * hw_a].reshape(bs, n_new, h_a, HEAD_DIM))
            outs["avs"].append(sf4[:, :, 2 * hw_a:3 * hw_a].reshape(bs, n_new, h_a, HEAD_DIM))
            outs["bks"].append(kb_new.reshape(bs, n_new, h_b, HEAD_DIM))
            outs["bvs"].append(vb_new.reshape(bs, n_new, h_b, HEAD_DIM))
        else:
            w_out = w_out_odd
            sf, sb, w_main_l = _matmul(xs_b, w_in_odd, i, (F32, BF16), 256, 512, n_cols=main_odd, emit_w=True)
            stl, w_tail_l = _matmul(xs_b, w_in_odd, i, (F32,), 256, tail_pad, col0=main_odd, n_cols=tail_pad,
                                    emit_w=True)
            kc0 = h_c
            vc0 = h_c + n_kv
            qic = (h_c * HEAD_DIM + 2 * kvw) // qiw

            def kz_of(ki):
                z = jnp.zeros_like(ki)
                return jnp.concatenate([ki, z, z, ki], axis=-1).astype(BF16).reshape(-1, 2 * LANES)

            pf, pb = _matmul(xp_b, w_main_l, None, (F32, BF16), 1024, 512)
            (pt,) = _matmul(xp_b, w_tail_l, None, (F32,), 1024, tail_pad)
            ki_p = pt[:, :IDX_DIM].reshape(bp, sp, IDX_DIM)
            wi_p = pt[:, IDX_DIM:IDX_DIM + IDX_HEADS]
            sel_p = _dsa_index(pb, wi_p, kz_of(ki_p), bp, sp, sp, sp, 128, 512, 0, qic)
            mix_p = _dsa_attn(pb, pb, pb, sel_p, bp, sp, sp, n_kv, 256, 512, 0, kc0, vc0)
            pf4 = pf.reshape(bp, sp, -1)
            q_w = h_c * HEAD_DIM
            outs["ckp"].append(pf4[:, :, q_w:q_w + kvw].reshape(bp, sp, n_kv, HEAD_DIM))
            outs["cvp"].append(pf4[:, :, q_w + kvw:q_w + 2 * kvw].reshape(bp, sp, n_kv, HEAD_DIM))
            outs["cip"].append(ki_p)
            sf4 = sf.reshape(bs, n_new, -1)
            k_new = sf4[:, :, q_w:q_w + kvw]
            v_new = sf4[:, :, q_w + kvw:q_w + 2 * kvw]
            ki_new = stl[:, :IDX_DIM].reshape(bs, n_new, IDX_DIM)
            wi_s = stl[:, IDX_DIM:IDX_DIM + IDX_HEADS]
            ki_all = _pad_rows(jnp.concatenate([cache_c_idx[i], ki_new], axis=1), l_s_pad)
            sel_s = _dsa_index(sb, wi_s, kz_of(ki_all), bs, n_new, l_s_pad, l_s, n_new, l_s_pad, past, qic)
            mix_s = _dsa_attn(sb, sb, sb, sel_s, bs, n_new, l_s_pad, n_kv, n_new, l_s_pad, past, kc0, vc0,
                              cache=(cache_c_k, cache_c_v, i))
            outs["cks"].append(k_new.reshape(bs, n_new, n_kv, HEAD_DIM))
            outs["cvs"].append(v_new.reshape(bs, n_new, n_kv, HEAD_DIM))
            outs["cis"].append(ki_new)

        xs, xs_b, w_out_l = _matmul_residual_ln(mix_s, w_out, i, xs, ln_g[l, 0], ln_b[l, 0], 256, 512, emit_w=True)
        xp, xp_b = _matmul_residual_ln(mix_p, w_out_l, None, xp, ln_g[l, 0], ln_b[l, 0], 512, 512)

        (mkv,) = _matmul(mem2d, w_mem_kv, l, (F32,), 512, 512)
        mk_p, mv_p = mkv[:, :mem_w], mkv[:, mem_w:]
        outs["mkp"].append(mk_p.reshape(bp, n_mem, MEM_HEADS, HEAD_DIM))
        outs["mvp"].append(mv_p.reshape(bp, n_mem, MEM_HEADS, HEAD_DIM))
        xp, xp_b = _mem_attn_ln(xp, wmq, wmo, l, mk_p, mv_p, ln_g[l, 1], ln_b[l, 1], 256, 256, sp)
        xs, xs_b = _mem_attn_ln(xs, wmq, wmo, l, cmk[l], cmv[l], ln_g[l, 1], ln_b[l, 1],
                                min(ms, 128), n_new, n_new)

        cb_l = conv_b[l][None]
        ffn_tm = min(FFN_TM, sp)
        act_s, st_s, w_val_l, w_gate_l = _ffn_up(xs_b, w_up, w_up, l, conv_w[l], cb_l, state_ffn_conv[l],
                                                 ms, n_new, n_new, ff_pad, emit_w=True)
        act_p, st_p = _ffn_up(xp_b, w_val_l, w_gate_l, None, conv_w[l], cb_l, None, ffn_tm, ffn_tm, sp, ff_pad)
        tiles_per_seq = sp // ffn_tm
        outs["fp"].append(st_p[tiles_per_seq - 1::tiles_per_seq])
        outs["fs"].append(st_s)
        xs, xs_b, w_down_l = _matmul_residual_ln(act_s, w_down, l, xs, ln_g[l, 2], ln_b[l, 2], 256, 512, emit_w=True)
        xp, xp_b = _matmul_residual_ln(act_p, w_down_l, None, xp, ln_g[l, 2], ln_b[l, 2], 512, 512)

    st = jnp.stack
    o = outs
    rolled_k = _rolled_band(cache_a_k, st(o["aks"]))
    rolled_v = _rolled_band(cache_a_v, st(o["avs"]))
    return (xp.reshape(bp, sp, d), xs.reshape(bs, n_new, d),
            st(o["akp"]), st(o["avp"]), st(o["bkp"]), st(o["bvp"]), st(o["ckp"]), st(o["cvp"]), st(o["cip"]),
            st(o["mkp"]), st(o["mvp"]), st(o["fp"]),
            rolled_k, rolled_v, st(o["bks"]), st(o["bvs"]), st(o["cks"]), st(o["cvs"]), st(o["cis"]),
            st(o["fs"]))
```

```python
import functools

import numpy as np
import jax
import jax.numpy as jnp
from jax import lax
from jax.experimental import pallas as pl
from jax.experimental.pallas import tpu as pltpu

BF16 = jnp.bfloat16
F32 = jnp.float32

CHUNK = 64
HEAD_DIM = 128
A_LEFT_CHUNKS = 8
REL_CLIP = 128
C_GROUP = 4
IDX_HEADS = 32
IDX_DIM = 64
TOPK_MAX = 256
MEM_HEADS = 4
CONV_W = 3
DEPTH = 4
ALPHA = (2.0 * DEPTH) ** 0.25
LN_EPS = 1e-5
ATTN_SCALE = HEAD_DIM ** -0.5
IDX_SCALE = IDX_DIM ** -0.5
IDX_W_SCALE = IDX_HEADS ** -0.5
NEG_INF = -1e30

V7X_VMEM_BYTES = 64 * 1024 * 1024
LANES = 128
SUBLANES = 8
SB_EXIT = 104.0
INT_MIN = -(2 ** 31)
NEG_KEY = int(np.array(NEG_INF, np.float32).view(np.int32)) ^ 0x7FFFFFFF


def _cparams(vmem_bytes):
    limit = int(min(max(vmem_bytes * 1.25 + (4 << 20), 32 << 20), V7X_VMEM_BYTES - (6 << 20)))
    return pltpu.CompilerParams(vmem_limit_bytes=limit)


def _nt_dot(a, b):
    return lax.dot_general(a, b, (((1,), (1,)), ((), ())), preferred_element_type=F32)


def _w_spec(w, l, block, index):
    if l is None:
        return pl.BlockSpec(block, index)
    return pl.BlockSpec((None,) + block, lambda *grid: (l,) + tuple(index(*grid)))


def _mm_body(x_ref, w_ref, *out_refs, emit_w):
    if emit_w:
        out_refs[-1][...] = w_ref[...].astype(BF16)
        w_ref, out_refs = out_refs[-1], out_refs[:-1]
    r = jnp.dot(x_ref[...].astype(BF16), w_ref[...].astype(BF16), preferred_element_type=F32)
    for o in out_refs:
        o[...] = r.astype(o.dtype)


def _matmul(x, w, l, out_dtypes, tm, tn, col0=0, n_cols=None, emit_w=False):
    m, kd = x.shape
    n = n_cols or w.shape[-1]
    tm, tn = min(tm, m), min(tn, n)
    assert m % tm == 0 and n % tn == 0 and col0 % tn == 0, (x.shape, w.shape, tm, tn, col0)
    assert not emit_w or m == tm
    j0 = col0 // tn
    wsz = w.dtype.itemsize
    vmem = 2 * (tm * kd * x.dtype.itemsize + kd * tn * wsz) + kd * tn * 2 * (3 if emit_w else 1) + 2 * tm * tn * 4
    vmem += sum(2 * tm * tn * jnp.dtype(d).itemsize for d in out_dtypes)
    out_specs = [pl.BlockSpec((tm, tn), lambda i, j: (i, j)) for _ in out_dtypes]
    out_shape = [jax.ShapeDtypeStruct((m, n), d) for d in out_dtypes]
    if emit_w:
        out_specs.append(pl.BlockSpec((kd, tn), lambda i, j: (0, j)))
        out_shape.append(jax.ShapeDtypeStruct((kd, n), BF16))
    return pl.pallas_call(
        functools.partial(_mm_body, emit_w=emit_w),
        grid=(m // tm, n // tn),
        in_specs=[pl.BlockSpec((tm, kd), lambda i, j: (i, 0)),
                  _w_spec(w, l, (kd, tn), lambda i, j: (0, j0 + j))],
        out_specs=out_specs,
        out_shape=out_shape,
        compiler_params=_cparams(vmem),
        name="matmul",
    )(x, w)


LN_ROWS = 128


def _residual_ln_rows(y_ref, res_ref, g_ref, b_ref, o_ref, ob_ref, rows):
    g = g_ref[...]
    b = b_ref[...]

    def chunk(c, carry):
        sl = pl.ds(pl.multiple_of(c * LN_ROWS, LN_ROWS), LN_ROWS)
        y = ALPHA * res_ref[sl, :] + y_ref[sl, :]
        mu = jnp.mean(y, axis=-1, keepdims=True)
        d = y - mu
        var = jnp.mean(d * d, axis=-1, keepdims=True)
        out = d * lax.rsqrt(var + LN_EPS) * g + b
        o_ref[sl, :] = out
        ob_ref[sl, :] = out.astype(BF16)
        return carry

    lax.fori_loop(0, rows // LN_ROWS, chunk, 0)


MM_LN_CHUNK = 512


def _mm_ln_body(x_ref, w_ref, res_ref, g_ref, b_ref, o_ref, ob_ref, *rest, n_k, tm, emit_w, w_rows):
    k = pl.program_id(1)
    x = x_ref[...]
    n = o_ref.shape[1]
    chunks = [slice(c, c + MM_LN_CHUNK) for c in range(0, n, MM_LN_CHUNK)]
    if emit_w:
        tk = w_ref.shape[0]
        row = k * tk + lax.broadcasted_iota(jnp.int32, (tk, 1), 0)
        (wb_ref,) = rest
        wb_ref[...] = jnp.where(row < w_rows, w_ref[...], 0.0).astype(BF16)
        w_ref = wb_ref

    @pl.when(k == 0)
    def _():
        for sl in chunks:
            o_ref[:, sl] = jnp.dot(x, w_ref[:, sl], preferred_element_type=F32)

    @pl.when(k > 0)
    def _():
        for sl in chunks:
            o_ref[:, sl] += jnp.dot(x, w_ref[:, sl], preferred_element_type=F32)

    @pl.when(k == n_k - 1)
    def _():
        _residual_ln_rows(o_ref, res_ref, g_ref, b_ref, o_ref, ob_ref, tm)


def _matmul_residual_ln(x, w, l, res, g, b, tm, tk, emit_w=False):
    m, kd = x.shape
    w_rows, n = w.shape[-2:]
    tm, tk = min(tm, m), min(tk, kd)
    assert m % tm == 0 and kd % tk == 0 and tm % LN_ROWS == 0 and n % MM_LN_CHUNK == 0
    assert (emit_w and m == tm) or w_rows == kd
    n_k = kd // tk
    vmem = 2 * (tm * tk * 2 + tk * n * w.dtype.itemsize + tm * n * 4 + tm * n * 4 + tm * n * 2)
    out_specs = [pl.BlockSpec((tm, n), lambda i, k: (i, 0)), pl.BlockSpec((tm, n), lambda i, k: (i, 0))]
    out_shape = [jax.ShapeDtypeStruct((m, n), F32), jax.ShapeDtypeStruct((m, n), BF16)]
    if emit_w:
        vmem += 3 * tk * n * 2
        out_specs.append(pl.BlockSpec((tk, n), lambda i, k: (k, 0)))
        out_shape.append(jax.ShapeDtypeStruct((kd, n), BF16))
    return pl.pallas_call(
        functools.partial(_mm_ln_body, n_k=n_k, tm=tm, emit_w=emit_w, w_rows=w_rows),
        grid=(m // tm, n_k),
        in_specs=[pl.BlockSpec((tm, tk), lambda i, k: (i, k)),
                  _w_spec(w, l, (tk, n), lambda i, k: (k, 0)),
                  pl.BlockSpec((tm, n), lambda i, k: (i, 0)),
                  pl.BlockSpec((1, n), lambda i, k: (0, 0)),
                  pl.BlockSpec((1, n), lambda i, k: (0, 0))],
        out_specs=out_specs,
        out_shape=out_shape,
        compiler_params=_cparams(vmem),
        name="matmul_residual_ln",
    )(x, w, res, g.reshape(1, n), b.reshape(1, n))


BAND_TQ = 128
BAND_KB = 5
ATTN_HB = 8


def _band_prompt_body(q_ref, k_ref, v_ref, bias_ref, o_ref):
    t = pl.program_id(2)
    heads = [slice(h * HEAD_DIM, (h + 1) * HEAD_DIM) for h in range(ATTN_HB)]
    first = t - (BAND_KB - 1)
    keys = [pl.ds(pl.multiple_of(jnp.maximum(first + jb, 0) * BAND_TQ, BAND_TQ), BAND_TQ) for jb in range(BAND_KB)]
    s = jnp.concatenate(
        [jnp.concatenate([_nt_dot(q_ref[:, hs], k_ref[keys[jb], hs]) for hs in heads], axis=0)
         for jb in range(BAND_KB)], axis=1)
    s = s * ATTN_SCALE + bias_ref[...].reshape(ATTN_HB * BAND_TQ, BAND_KB * BAND_TQ)
    col_block = lax.broadcasted_iota(jnp.int32, (1, BAND_KB * BAND_TQ), 1) // BAND_TQ
    s = jnp.where(first + col_block >= 0, s, NEG_INF)
    m = jnp.max(s, axis=-1, keepdims=True)
    p = jnp.exp(s - m)
    l = jnp.sum(p, axis=-1, keepdims=True)
    p = p.astype(BF16)
    for h, hs in enumerate(heads):
        rows = slice(h * BAND_TQ, (h + 1) * BAND_TQ)
        acc = sum(jnp.dot(p[rows, jb * BAND_TQ:(jb + 1) * BAND_TQ], v_ref[keys[jb], hs],
                          preferred_element_type=F32) for jb in range(BAND_KB))
        o_ref[:, hs] = (acc / l[rows]).astype(o_ref.dtype)


def _toeplitz_bias(rel_bias, n_rows, n_cols, offset):
    m = np.arange(n_rows + n_cols - 1) - (n_rows - 1)
    diag = rel_bias[:, np.clip(offset - m, -REL_CLIP, REL_CLIP) + REL_CLIP]
    rows = [diag[:, n_rows - 1 - i:n_rows - 1 - i + n_cols] for i in range(n_rows)]
    return jnp.stack(rows, axis=1).astype(F32)


def _band_attn_prompt(qkv, rel_bias, batch, seq, n_heads, q_col, k_col, v_col):
    assert n_heads % ATTN_HB == 0 and q_col % ATTN_HB == 0 and k_col % ATTN_HB == 0 and v_col % ATTN_HB == 0
    nt = seq // BAND_TQ
    hw = ATTN_HB * HEAD_DIM
    bias = _toeplitz_bias(rel_bias, BAND_TQ, BAND_KB * BAND_TQ, A_LEFT_CHUNKS * CHUNK)
    qi = np.arange(BAND_TQ)[:, None] // CHUNK
    kj = np.arange(BAND_KB * BAND_TQ)[None, :] // CHUNK
    in_band = (kj >= qi) & (kj <= qi + A_LEFT_CHUNKS)
    bias = jnp.where(jnp.asarray(in_band)[None], bias, NEG_INF)
    vmem = 2 * (2 * seq * hw * 2 + ATTN_HB * BAND_TQ * BAND_KB * BAND_TQ * 4) + (8 << 20)
    return pl.pallas_call(
        _band_prompt_body,
        grid=(batch, n_heads // ATTN_HB, nt),
        in_specs=[pl.BlockSpec((BAND_TQ, hw), lambda b, h, t: (b * nt + t, q_col // ATTN_HB + h)),
                  pl.BlockSpec((seq, hw), lambda b, h, t: (b, k_col // ATTN_HB + h)),
                  pl.BlockSpec((seq, hw), lambda b, h, t: (b, v_col // ATTN_HB + h)),
                  pl.BlockSpec((ATTN_HB, BAND_TQ, BAND_KB * BAND_TQ), lambda b, h, t: (h, 0, 0))],
        out_specs=pl.BlockSpec((BAND_TQ, hw), lambda b, h, t: (b * nt + t, h)),
        out_shape=jax.ShapeDtypeStruct((batch * seq, n_heads * HEAD_DIM), BF16),
        compiler_params=_cparams(vmem),
        name="band_attn_prompt",
    )(qkv, qkv, qkv, bias)


def _band_cached_body(q_ref, kc_ref, vc_ref, kn_ref, vn_ref, bias_ref, o_ref, *, n_heads, window):
    for h in range(n_heads):
        hs = slice(h * HEAD_DIM, (h + 1) * HEAD_DIM)
        cached = pl.ds(h, window, stride=n_heads)
        k = jnp.concatenate([kc_ref[cached, :].astype(BF16), kn_ref[:, hs]], axis=0)
        v = jnp.concatenate([vc_ref[cached, :].astype(BF16), vn_ref[:, hs]], axis=0)
        s = _nt_dot(q_ref[:, hs], k) * ATTN_SCALE + bias_ref[h]
        m = jnp.max(s, axis=-1, keepdims=True)
        p = jnp.exp(s - m)
        l = jnp.sum(p, axis=-1, keepdims=True)
        acc = jnp.dot(p.astype(BF16), v, preferred_element_type=F32)
        o_ref[:, hs] = (acc / l).astype(o_ref.dtype)


def _band_attn_cached(q2d, cache_k, cache_v, layer, new2d, rel_bias, batch, n_new, k_col, v_col):
    window, n_heads = cache_k.shape[2:4]
    hw = n_heads * HEAD_DIM
    n_keys = window + n_new
    bias = _toeplitz_bias(rel_bias, n_new, n_keys, window)
    flat = lambda c: c.reshape(c.shape[0], batch, window * n_heads, HEAD_DIM)
    cache_spec = pl.BlockSpec((None, None, window * n_heads, HEAD_DIM), lambda b: (layer, b, 0, 0))
    vmem = 2 * (2 * window * hw * 4 + n_heads * n_new * n_keys * 4) + (8 << 20)
    return pl.pallas_call(
        functools.partial(_band_cached_body, n_heads=n_heads, window=window),
        grid=(batch,),
        in_specs=[pl.BlockSpec((n_new, hw), lambda b: (b, 0)),
                  cache_spec, cache_spec,
                  pl.BlockSpec((n_new, hw), lambda b: (b, k_col)),
                  pl.BlockSpec((n_new, hw), lambda b: (b, v_col)),
                  pl.BlockSpec((n_heads, n_new, n_keys), lambda b: (0, 0, 0))],
        out_specs=pl.BlockSpec((n_new, hw), lambda b: (b, 0)),
        out_shape=jax.ShapeDtypeStruct((batch * n_new, hw), BF16),
        compiler_params=_cparams(vmem),
        name="band_attn_cached",
    )(q2d, flat(cache_k), flat(cache_v), new2d, new2d, bias)


def _roll_body(c_ref, new_ref, o_ref):
    keep = o_ref.shape[0] - new_ref.shape[0]
    o_ref[0:keep, :] = c_ref[new_ref.shape[0]:, :]
    o_ref[keep:, :] = new_ref[...]


def _rolled_band(cache, new_rows):
    n_layers, batch, window, n_heads, hd = cache.shape
    n_new = new_rows.shape[2]
    assert n_new <= window and (n_new * n_heads) % SUBLANES == 0
    spec = lambda rows: pl.BlockSpec((None, None, rows * n_heads, hd), lambda l, b: (l, b, 0, 0))
    out = pl.pallas_call(
        _roll_body,
        grid=(n_layers, batch),
        in_specs=[spec(window), spec(n_new)],
        out_specs=spec(window),
        out_shape=jax.ShapeDtypeStruct((n_layers, batch, window * n_heads, hd), cache.dtype),
        name="rolled_band",
    )(cache.reshape(n_layers, batch, window * n_heads, hd), new_rows.reshape(n_layers, batch, n_new * n_heads, hd))
    return out.reshape(cache.shape)


SB_TK = 256


def _split2(x):
    hi = x.astype(BF16)
    return hi, (x - hi.astype(F32)).astype(BF16)


def _sb_window(qs, ks, vs, u, mask, carry, acc, tq):
    z = jnp.concatenate([_nt_dot(q, k) for q, k in zip(qs, ks)], axis=0) * ATTN_SCALE
    sp = jnp.maximum(z, 0.0) + jnp.log(1.0 + jnp.exp(-jnp.abs(z)))
    log_1m = jnp.where(mask, -sp, 0.0)
    suffix = sum(jnp.dot(piece, u, preferred_element_type=F32) for piece in _split2(log_1m))
    w = jnp.where(mask, jnp.exp(z - sp + suffix + carry), 0.0).astype(BF16)
    acc = acc + jnp.concatenate(
        [jnp.dot(w[h * tq:(h + 1) * tq], v, preferred_element_type=F32) for h, v in enumerate(vs)], axis=0)
    return carry + jnp.sum(log_1m, axis=-1, keepdims=True), acc


def _sb_continue(c):
    limit, floor = c[0], c[1]
    return jnp.logical_and(limit > 0, floor > -SB_EXIT)


def _stick_break_body(q_ref, k_ref, v_ref, u_ref, o_ref, *, tq, qpos_base):
    qt = pl.program_id(2)
    qpos0 = qpos_base + qt * tq
    u = u_ref[...]
    rows = ATTN_HB * tq
    rowpos = qpos0 + lax.broadcasted_iota(jnp.int32, (rows, 1), 0) % tq
    top = (qpos0 + tq + LANES - 1) // LANES * LANES
    heads = [slice(h * HEAD_DIM, (h + 1) * HEAD_DIM) for h in range(ATTN_HB)]
    qs = [q_ref[:, hs] for hs in heads]

    def body(c):
        limit, _, carry, acc = c
        start = pl.multiple_of(jnp.maximum(limit - SB_TK, 0), LANES)
        keys = pl.ds(start, SB_TK)
        kpos = start + lax.broadcasted_iota(jnp.int32, (1, SB_TK), 1)
        mask = jnp.logical_and(kpos < rowpos, kpos < limit)
        carry, acc = _sb_window(qs, [k_ref[keys, hs] for hs in heads], [v_ref[keys, hs] for hs in heads],
                                u, mask, carry, acc, tq)
        return limit - SB_TK, jnp.max(carry), carry, acc

    init = (top, jnp.float32(0.0), jnp.zeros((rows, 1), F32), jnp.zeros((rows, HEAD_DIM), F32))
    acc = lax.while_loop(_sb_continue, body, init)[3]
    for h, hs in enumerate(heads):
        o_ref[:, hs] = acc[h * tq:(h + 1) * tq].astype(o_ref.dtype)


def _stick_break_cached_body(q_ref, kc_ref, vc_ref, kn_ref, vn_ref, u_ref, o_ref, *, n_new, past, n_heads):
    hg = pl.program_id(1)
    u = u_ref[...]
    rows = ATTN_HB * n_new
    rowpos = past + lax.broadcasted_iota(jnp.int32, (rows, 1), 0) % n_new
    heads = [slice(h * HEAD_DIM, (h + 1) * HEAD_DIM) for h in range(ATTN_HB)]
    qs = [q_ref[:, hs] for hs in heads]
    first = past - (SB_TK - LANES)
    filler = jnp.zeros((LANES - n_new, HEAD_DIM), BF16)

    def cached(ref, start, size, h):
        row0 = start * n_heads + hg * ATTN_HB + h
        return ref[pl.ds(row0, size, stride=n_heads), :].astype(BF16)

    def newest(cache_ref, new_ref):
        return [jnp.concatenate([cached(cache_ref, first, LANES, h), new_ref[:, hs], filler], axis=0)
                for h, hs in enumerate(heads)]

    kpos = first + lax.broadcasted_iota(jnp.int32, (1, SB_TK), 1)
    carry, acc = _sb_window(qs, newest(kc_ref, kn_ref), newest(vc_ref, vn_ref), u, kpos < rowpos,
                            jnp.zeros((rows, 1), F32), jnp.zeros((rows, HEAD_DIM), F32), n_new)

    def body(c):
        limit, _, carry, acc = c
        start = jnp.maximum(limit - SB_TK, 0)
        mask = jnp.broadcast_to(start + lax.broadcasted_iota(jnp.int32, (1, SB_TK), 1) < limit, (rows, SB_TK))
        carry, acc = _sb_window(qs, [cached(kc_ref, start, SB_TK, h) for h in range(ATTN_HB)],
                                [cached(vc_ref, start, SB_TK, h) for h in range(ATTN_HB)],
                                u, mask, carry, acc, n_new)
        return limit - SB_TK, jnp.max(carry), carry, acc

    acc = lax.while_loop(_sb_continue, body, (jnp.int32(first), jnp.max(carry), carry, acc))[3]
    for h, hs in enumerate(heads):
        o_ref[:, hs] = acc[h * n_new:(h + 1) * n_new].astype(o_ref.dtype)


def _stick_break_cached(q2d, cache_k, cache_v, layer, new2d, batch, n_new, q_col, k_col, v_col):
    past, n_heads = cache_k.shape[2:4]
    assert past % LANES == 0 and past >= SB_TK and n_new <= LANES and n_heads % ATTN_HB == 0
    assert q_col % ATTN_HB == 0 and k_col % ATTN_HB == 0 and v_col % ATTN_HB == 0
    hw = ATTN_HB * HEAD_DIM
    flat = lambda c: c.reshape(c.shape[0], batch, past * n_heads, HEAD_DIM)
    j = np.arange(SB_TK)
    u = jnp.asarray(j[:, None] > j[None, :], BF16)
    cache_spec = pl.BlockSpec((None, None, past * n_heads, HEAD_DIM), lambda b, h: (layer, b, 0, 0))
    vmem = 2 * (2 * past * n_heads * HEAD_DIM * 4) + (8 << 20)
    return pl.pallas_call(
        functools.partial(_stick_break_cached_body, n_new=n_new, past=past, n_heads=n_heads),
        grid=(batch, n_heads // ATTN_HB),
        in_specs=[pl.BlockSpec((n_new, hw), lambda b, h: (b, q_col // ATTN_HB + h)),
                  cache_spec, cache_spec,
                  pl.BlockSpec((n_new, hw), lambda b, h: (b, k_col // ATTN_HB + h)),
                  pl.BlockSpec((n_new, hw), lambda b, h: (b, v_col // ATTN_HB + h)),
                  pl.BlockSpec((SB_TK, SB_TK), lambda b, h: (0, 0))],
        out_specs=pl.BlockSpec((n_new, hw), lambda b, h: (b, h)),
        out_shape=jax.ShapeDtypeStruct((batch * n_new, n_heads * HEAD_DIM), BF16),
        compiler_params=_cparams(vmem),
        name="stick_break_cached",
    )(q2d, flat(cache_k), flat(cache_v), new2d, new2d, u)


def _stick_break_attn(q2d, k2d, v2d, batch, n_q, n_k, n_heads, tq, qpos_base, q_col, k_col, v_col):
    assert n_q % tq == 0 and n_k % LANES == 0 and n_k >= SB_TK and n_heads % ATTN_HB == 0
    assert q_col % ATTN_HB == 0 and k_col % ATTN_HB == 0 and v_col % ATTN_HB == 0
    assert -(-(qpos_base + n_q) // LANES) * LANES <= n_k
    nt = n_q // tq
    hw = ATTN_HB * HEAD_DIM
    j = np.arange(SB_TK)
    u = jnp.asarray(j[:, None] > j[None, :], BF16)
    vmem = 2 * (2 * n_k * hw * 2) + (8 << 20)
    return pl.pallas_call(
        functools.partial(_stick_break_body, tq=tq, qpos_base=qpos_base),
        grid=(batch, n_heads // ATTN_HB, nt),
        in_specs=[pl.BlockSpec((tq, hw), lambda b, h, t: (b * nt + t, q_col // ATTN_HB + h)),
                  pl.BlockSpec((n_k, hw), lambda b, h, t: (b, k_col // ATTN_HB + h)),
                  pl.BlockSpec((n_k, hw), lambda b, h, t: (b, v_col // ATTN_HB + h)),
                  pl.BlockSpec((SB_TK, SB_TK), lambda b, h, t: (0, 0))],
        out_specs=pl.BlockSpec((tq, hw), lambda b, h, t: (b * nt + t, h)),
        out_shape=jax.ShapeDtypeStruct((batch * n_q, n_heads * HEAD_DIM), BF16),
        compiler_params=_cparams(vmem),
        name="stick_break_attn",
    )(q2d, k2d, v2d, u)


def _sortable(x):
    bits = pltpu.bitcast(x, jnp.int32)
    return jnp.where(bits < 0, bits ^ jnp.int32(0x7FFFFFFF), bits)


def _dsa_index_body(qi_ref, wi_ref, kz_ref, bias_ref, key_ref, *, tq, tk, n_real, qpos_base, topk):
    n_pad = key_ref.shape[1]
    qt = pl.program_id(1)
    qpos0 = qpos_base + qt * tq
    rowpos = qpos0 + lax.broadcasted_iota(jnp.int32, (tq, 1), 0)
    qchunk = rowpos // CHUNK
    n_kb = jnp.minimum((((qpos0 + tq - 1) // CHUNK + 1) * CHUNK + tk - 1) // tk, n_pad // tk)
    n_tail = jnp.maximum(n_real - n_kb * tk, 0)
    tail0 = n_kb * tk

    w = wi_ref[...] * (IDX_SCALE * IDX_W_SCALE)

    def block_pos(kb):
        start = pl.multiple_of(kb * tk, tk)
        return start, start + lax.broadcasted_iota(jnp.int32, (1, tk), 1)

    def score_block(kb, carry):
        start, kpos = block_pos(kb)
        kz = kz_ref[pl.ds(start, tk), :]
        acc = jnp.zeros((tq, tk), F32)
        for pair in range(IDX_HEADS // 2):
            qp = qi_ref[:, pair * LANES:(pair + 1) * LANES]
            for half in range(2):
                h = 2 * pair + half
                sc = _nt_dot(qp, kz[:, half * LANES:(half + 1) * LANES])
                acc = acc + jnp.maximum(sc, 0.0) * w[:, h:h + 1]
        key = _sortable(jnp.where(kpos // CHUNK <= qchunk, acc, NEG_INF))
        key_ref[:, pl.ds(start, tk)] = jnp.where(kpos < n_real, key, jnp.int32(INT_MIN))
        return carry

    lax.fori_loop(0, n_kb, score_block, 0)

    def count(pred):
        def blk(kb, acc):
            start, kpos = block_pos(kb)
            hit = pred(key_ref[:, pl.ds(start, tk)], kpos).astype(F32)
            return acc + sum(hit[:, c:c + LANES] for c in range(0, tk, LANES))
        lanes = lax.fori_loop(0, n_kb, blk, jnp.zeros((tq, LANES), F32))
        return jnp.sum(lanes, axis=-1, keepdims=True)

    kf = jnp.float32(topk)
    tail_f = n_tail.astype(F32)
    neg_key = jnp.int32(NEG_KEY)

    def thr_step(i, ans_u):
        cand_u = ans_u | lax.shift_left(jnp.int32(1), 31 - i)
        cand = cand_u ^ jnp.int32(INT_MIN)
        cnt = count(lambda key, kpos: key >= cand) + jnp.where(neg_key >= cand, tail_f, 0.0)
        return jnp.where(cnt >= kf, cand_u, ans_u)

    thr = lax.fori_loop(0, 32, thr_step, jnp.zeros((tq, 1), jnp.int32)) ^ jnp.int32(INT_MIN)
    tail_gt = jnp.where(neg_key > thr, tail_f, 0.0)
    n_gt = count(lambda key, kpos: key > thr) + tail_gt
    n_eq = count(lambda key, kpos: key == thr) + jnp.where(neg_key == thr, tail_f, 0.0)
    need = kf - n_gt

    n_bits = int(np.ceil(np.log2(n_pad)))

    def tie_search():
        def pos_step(i, lo):
            cand = lo + lax.shift_left(jnp.int32(1), n_bits - 1 - i)
            in_tail = jnp.clip(cand - tail0, 0, n_tail).astype(F32)
            cnt = (count(lambda key, kpos: jnp.logical_and(key == thr, kpos < cand))
                   + jnp.where(neg_key == thr, in_tail, 0.0))
            return jnp.where(cnt < need, cand, lo)
        return lax.fori_loop(0, n_bits, pos_step, jnp.zeros((tq, 1), jnp.int32))

    has_ties = jnp.max(jnp.where(n_eq > need, 1.0, 0.0)) > 0.0
    last = lax.cond(has_ties, tie_search, lambda: jnp.full((tq, 1), n_pad, jnp.int32))

    bias_ref[0] = jnp.full(bias_ref.shape[1:], NEG_INF, bias_ref.dtype)

    def write_block(kb, carry):
        start, kpos = block_pos(kb)
        key = key_ref[:, pl.ds(start, tk)]
        sel = jnp.logical_or(key > thr, jnp.logical_and(key == thr, kpos <= last))
        ok = jnp.logical_and(jnp.logical_and(sel, kpos // CHUNK <= qchunk), kpos < n_real)
        bias_ref[0, :, pl.ds(start, tk)] = jnp.where(ok, 0.0, NEG_INF).astype(bias_ref.dtype)
        return carry

    lax.fori_loop(0, n_kb, write_block, 0)


def _dsa_index(qi2d, wi, kz, batch, n_q, n_k, n_real, tq, tk, qpos_base, qi_col):
    nt = n_q // tq
    topk = min(TOPK_MAX, n_real // 4)
    vmem = 2 * (tq * 2048 * 2 + n_k * 256 * 2 + tq * n_k * 2) + tq * n_k * 4 + 8 * tq * tk * 4 + (4 << 20)
    return pl.pallas_call(
        functools.partial(_dsa_index_body, tq=tq, tk=tk, n_real=n_real, qpos_base=qpos_base, topk=topk),
        grid=(batch, nt),
        in_specs=[pl.BlockSpec((tq, IDX_HEADS * IDX_DIM), lambda b, t: (b * nt + t, qi_col)),
                  pl.BlockSpec((tq, IDX_HEADS), lambda b, t: (b * nt + t, 0)),
                  pl.BlockSpec((n_k, 2 * LANES), lambda b, t: (b, 0))],
        out_specs=pl.BlockSpec((1, tq, n_k), lambda b, t: (b, t, 0)),
        out_shape=jax.ShapeDtypeStruct((batch, n_q, n_k), BF16),
        scratch_shapes=[pltpu.VMEM((tq, n_k), jnp.int32)],
        compiler_params=_cparams(vmem),
        name="dsa_index",
    )(qi2d, wi, kz)


def _dsa_attn_body(*refs, tq, tk, qpos_base, cached):
    if cached:
        (q_ref, kc_ref, vc_ref, kn_ref, vn_ref, bias_ref, slope_ref, o_ref,
         s0_ref, s1_ref, p0_ref, p1_ref, k_ref, v_ref) = refs
        past, n_kv = cached
        n_new = kn_ref.shape[0]
        own = pl.ds(pl.program_id(1), past, stride=n_kv)
        for full_ref, cache_ref, new_ref in ((k_ref, kc_ref, kn_ref), (v_ref, vc_ref, vn_ref)):
            full_ref[0:past, :] = cache_ref[own, :].astype(BF16)
            full_ref[past:past + n_new, :] = new_ref[...]
            full_ref[past + n_new:, :] = jnp.zeros((full_ref.shape[0] - past - n_new, HEAD_DIM), BF16)
    else:
        q_ref, k_ref, v_ref, bias_ref, slope_ref, o_ref, s0_ref, s1_ref, p0_ref, p1_ref = refs
    s_refs, p_refs = (s0_ref, s1_ref), (p0_ref, p1_ref)
    n_pad = k_ref.shape[0]
    qt = pl.program_id(2)
    qpos0 = qpos_base + qt * tq
    tile_end = qpos0 + tq
    rows = C_GROUP * tq
    rowpos = qpos0 + lax.broadcasted_iota(jnp.int32, (tq, 1), 0)
    n_kb = jnp.minimum((((qpos0 + tq - 1) // CHUNK + 1) * CHUNK + tk - 1) // tk, n_pad // tk)
    qs = jnp.concatenate([q_ref[:, g * HEAD_DIM:(g + 1) * HEAD_DIM] for g in range(C_GROUP)], axis=0)

    def key_rows(kb):
        return pl.ds(pl.multiple_of(kb * tk, tk), tk)

    def scores(kb, slot):
        s_refs[slot][...] = _nt_dot(qs, k_ref[key_rows(kb), :])

    def weighted_values(kb, slot):
        v = v_ref[key_rows(jnp.maximum(kb, 0)), :]
        v_ones = jnp.concatenate([v, jnp.ones(v.shape, BF16)], axis=1)
        return jnp.dot(p_refs[slot][...], v_ones, preferred_element_type=F32)

    slope = slope_ref[0]

    def softmax(kb, slot, m, l, has_later_keys):
        kpos = kb * tk + lax.broadcasted_iota(jnp.int32, (1, tk), 1)
        back = (tile_end - kpos).astype(F32)
        s = (s_refs[slot][...] * ATTN_SCALE - slope * back).reshape(C_GROUP, tq, tk)
        s = s + bias_ref[0, :, key_rows(kb)].astype(F32)[None]
        if has_later_keys:
            ahead = jnp.where(kpos > rowpos, (rowpos - kpos).astype(F32), 0.0)
            s = s + (2.0 * slope).reshape(C_GROUP, tq, 1) * ahead[None]
        s = s.reshape(rows, tk)
        m_new = jnp.maximum(m, jnp.max(s, axis=-1, keepdims=True))
        p = jnp.exp(s - m_new)
        p_refs[slot][...] = p.astype(BF16)
        a = jnp.exp(m - m_new)
        return m_new, l, a

    def step(kb, slot, state):
        m, l, a_prev, acc = state
        acc = a_prev * acc + weighted_values(kb - 1, 1 - slot)
        m, l, a = softmax(kb, slot, m, l, False)
        scores(kb + 1, 1 - slot)
        return m, l, a, acc

    def finish(kb, slot, state):
        m, l, a_prev, acc = state
        acc = a_prev * acc + weighted_values(kb - 1, 1 - slot)
        m, l, a = softmax(kb, slot, m, l, True)
        return a * acc + weighted_values(kb, slot), l

    p_refs[1][...] = jnp.zeros(p_refs[1].shape, BF16)
    scores(0, 0)
    init = (jnp.full((rows, 1), 0.5 * NEG_INF, F32), jnp.zeros((rows, 1), F32),
            jnp.ones((rows, 1), F32), jnp.zeros((rows, 2 * HEAD_DIM), F32))
    last = n_kb - 1
    state = lax.fori_loop(0, last // 2, lambda j, st: step(2 * j + 1, 1, step(2 * j, 0, st)), init)
    acc, l = lax.cond(last % 2 == 1,
                      lambda st: finish(last, 1, step(last - 1, 0, st)),
                      lambda st: finish(last, 0, st), state)
    o = (acc[:, :HEAD_DIM] / acc[:, HEAD_DIM:HEAD_DIM + 1]).astype(o_ref.dtype)
    for g in range(C_GROUP):
        o_ref[:, g * HEAD_DIM:(g + 1) * HEAD_DIM] = o[g * tq:(g + 1) * tq]


def _dsa_attn(q2d, k2d, v2d, bias, batch, n_q, n_k, n_kv, tq, tk, qpos_base, k_col, v_col, cache=None):
    assert tk % tq == 0 and qpos_base % tq == 0 and n_k % tk == 0
    nt = n_q // tq
    gw = C_GROUP * HEAD_DIM
    n_heads = n_kv * C_GROUP
    slopes = 2.0 ** (-8.0 * np.arange(1, n_heads + 1) / n_heads)
    slope_rows = np.repeat(slopes.reshape(n_kv, C_GROUP), tq, axis=1)[..., None].astype(np.float32)
    rows = C_GROUP * tq
    vmem = (2 * (2 * n_k * HEAD_DIM * 2 + tq * n_k * 2 + 2 * tq * gw * 2 + rows * LANES * 4)
            + 2 * rows * tk * 6 + 8 * rows * tk * 4 + (4 << 20))
    scratch = [pltpu.VMEM((rows, tk), F32)] * 2 + [pltpu.VMEM((rows, tk), BF16)] * 2
    if cache is None:
        cached = None
        kv_specs = [pl.BlockSpec((n_k, HEAD_DIM), lambda b, n, t: (b, k_col + n)),
                    pl.BlockSpec((n_k, HEAD_DIM), lambda b, n, t: (b, v_col + n))]
        kv_args = [k2d, v2d]
    else:
        cache_k, cache_v, layer = cache
        past = cache_k.shape[2]
        assert nt == 1 and qpos_base == past and past + n_q <= n_k and cache_k.shape[3] == n_kv
        cached = (past, n_kv)
        flat = lambda c: c.reshape(c.shape[0], batch, past * n_kv, HEAD_DIM)
        cache_spec = pl.BlockSpec((None, None, past * n_kv, HEAD_DIM), lambda b, n, t: (layer, b, 0, 0))
        kv_specs = [cache_spec, cache_spec,
                    pl.BlockSpec((n_q, HEAD_DIM), lambda b, n, t: (b, k_col + n)),
                    pl.BlockSpec((n_q, HEAD_DIM), lambda b, n, t: (b, v_col + n))]
        kv_args = [flat(cache_k), flat(cache_v), k2d, v2d]
        scratch += [pltpu.VMEM((n_k, HEAD_DIM), BF16), pltpu.VMEM((n_k, HEAD_DIM), BF16)]
        vmem += 2 * 2 * past * n_kv * HEAD_DIM * 4
    return pl.pallas_call(
        functools.partial(_dsa_attn_body, tq=tq, tk=tk, qpos_base=qpos_base, cached=cached),
        grid=(batch, n_kv, nt),
        in_specs=[pl.BlockSpec((tq, gw), lambda b, n, t: (b * nt + t, n))] + kv_specs + [
            pl.BlockSpec((1, tq, n_k), lambda b, n, t: (b, t, 0)),
            pl.BlockSpec((1, rows, 1), lambda b, n, t: (n, 0, 0))],
        out_specs=pl.BlockSpec((tq, gw), lambda b, n, t: (b * nt + t, n)),
        out_shape=jax.ShapeDtypeStruct((batch * n_q, n_heads * HEAD_DIM), BF16),
        scratch_shapes=scratch,
        compiler_params=_cparams(vmem),
        name="dsa_attn",
    )(q2d, *kv_args, bias, jnp.asarray(slope_rows))


def _mem_attn_body(x_ref, wq_ref, mk_ref, mv_ref, wo_ref, g_ref, b_ref, o_ref, ob_ref, y_ref, *,
                   tm, rows_per_seq, n_mem):
    q = jnp.dot(x_ref[...].astype(BF16), wq_ref[...], preferred_element_type=F32).astype(BF16)
    n_seq = tm // rows_per_seq
    heads = [slice(h * HEAD_DIM, (h + 1) * HEAD_DIM) for h in range(MEM_HEADS)]
    s = jnp.concatenate(
        [_nt_dot(q[s_i * rows_per_seq:(s_i + 1) * rows_per_seq, hs],
                 mk_ref[s_i * n_mem:(s_i + 1) * n_mem, hs].astype(BF16))
         for hs in heads for s_i in range(n_seq)], axis=0) * ATTN_SCALE
    m = jnp.max(s, axis=-1, keepdims=True)
    p = jnp.exp(s - m)
    l = jnp.sum(p, axis=-1, keepdims=True)
    p = p.astype(BF16)
    o_heads = []
    for h, hs in enumerate(heads):
        pieces = []
        for s_i in range(n_seq):
            rows = slice(h * tm + s_i * rows_per_seq, h * tm + (s_i + 1) * rows_per_seq)
            o = jnp.dot(p[rows], mv_ref[s_i * n_mem:(s_i + 1) * n_mem, hs].astype(BF16),
                        preferred_element_type=F32)
            pieces.append((o / l[rows]).astype(BF16))
        o_heads.append(pieces[0] if n_seq == 1 else jnp.concatenate(pieces, axis=0))
    y_ref[...] = jnp.dot(jnp.concatenate(o_heads, axis=-1), wo_ref[...], preferred_element_type=F32)
    _residual_ln_rows(y_ref, x_ref, g_ref, b_ref, o_ref, ob_ref, tm)


def _mem_attn_ln(x, wq, wo, l, mk, mv, g, b, tm, rows_per_seq, seq_rows):
    m, d = x.shape
    mw = wq.shape[-1]
    n_seq_tile = tm // rows_per_seq
    n_mem = mk.shape[0] // (m // seq_rows)
    if n_seq_tile == 1:
        mem_map = lambda i: ((i * tm) // seq_rows, 0)
    else:
        assert rows_per_seq == seq_rows
        mem_map = lambda i: (i, 0)
    vmem = 2 * (tm * d * 4 * 2 + tm * d * 2 + 2 * d * mw * 2 + 2 * n_seq_tile * n_mem * mw * 4) + tm * d * 4
    out_specs = [pl.BlockSpec((tm, d), lambda i: (i, 0)), pl.BlockSpec((tm, d), lambda i: (i, 0))]
    out_shape = [jax.ShapeDtypeStruct((m, d), F32), jax.ShapeDtypeStruct((m, d), BF16)]
    return pl.pallas_call(
        functools.partial(_mem_attn_body, tm=tm, rows_per_seq=rows_per_seq, n_mem=n_mem),
        grid=(m // tm,),
        in_specs=[pl.BlockSpec((tm, d), lambda i: (i, 0)),
                  _w_spec(wq, l, (d, mw), lambda i: (0, 0)),
                  pl.BlockSpec((n_seq_tile * n_mem, mw), mem_map),
                  pl.BlockSpec((n_seq_tile * n_mem, mw), mem_map),
                  _w_spec(wo, l, (mw, d), lambda i: (0, 0)),
                  pl.BlockSpec((1, d), lambda i: (0, 0)),
                  pl.BlockSpec((1, d), lambda i: (0, 0))],
        out_specs=out_specs,
        out_shape=out_shape,
        scratch_shapes=[pltpu.VMEM((tm, d), F32)],
        compiler_params=_cparams(vmem),
        name="mem_attn_ln",
    )(x, wq, mk, mv, wo, g.reshape(1, d), b.reshape(1, d))


FFN_TN = 256
FFN_SUB = 256
FFN_TM = 2048
HALO = 8


def _gelu(x):
    return 0.5 * x * (1.0 + lax.erf(x * np.float32(2.0 ** -0.5)))


def _ffn_up_body(x_ref, wa_ref, wg_ref, cwa_ref, cwg_ref, cba_ref, cbg_ref, *rest,
                 tm, rows_per_seq, tiles_per_seq, carried, nj, emit_w):
    if carried:
        act_ref, sta_ref, stg_ref, carry_ref, ext_ref, ext2_ref = rest
    elif emit_w:
        pa_ref, pg_ref, act_ref, sta_ref, stg_ref, wab_ref, wgb_ref, ext_ref = rest
    else:
        pa_ref, pg_ref, act_ref, sta_ref, stg_ref, ext_ref = rest
    i = pl.program_id(0)
    j = pl.program_id(1)

    def conv_gate(base, rows, buf=None):
        buf = ext_ref if buf is None else buf
        cw = jnp.concatenate([cwa_ref[...], cwg_ref[...]], axis=1)
        cb = jnp.concatenate([cba_ref[...], cbg_ref[...]], axis=1)
        h = (buf[base:base + rows, :] * cw[2:3, :]
             + buf[base - 1:base - 1 + rows, :] * cw[1:2, :]
             + buf[base - 2:base - 2 + rows, :] * cw[0:1, :]
             + cb)
        return (h[:, :FFN_TN] * _gelu(h[:, FFN_TN:])).astype(act_ref.dtype)

    def put_state(s_i, last2):
        sta_ref[s_i] = last2[:, :FFN_TN]
        stg_ref[s_i] = last2[:, FFN_TN:]

    @pl.when(j < nj)
    def _():
        if carried:
            @pl.when(i % tiles_per_seq == 0)
            def _():
                carry_ref[j] = jnp.zeros((2, 2 * FFN_TN), F32)

            sub = min(FFN_SUB, tm)
            bufs = (ext_ref, ext2_ref)
            last2 = carry_ref[j]
            for n, r in enumerate(range(0, tm, sub)):
                buf = bufs[n % 2]
                buf[HALO - 2:HALO, :] = last2
                buf[HALO:HALO + sub, :FFN_TN] = jnp.dot(x_ref[r:r + sub, :], wa_ref[...],
                                                        preferred_element_type=F32)
                buf[HALO:HALO + sub, FFN_TN:] = jnp.dot(x_ref[r:r + sub, :], wg_ref[...],
                                                        preferred_element_type=F32)
                act_ref[r:r + sub, :] = conv_gate(HALO, sub, buf)
                last2 = buf[HALO + sub - 2:HALO + sub, :]
            carry_ref[j] = last2
            put_state(0, last2)
        else:
            wa, wg = wa_ref, wg_ref
            if emit_w:
                wab_ref[...] = wa_ref[...].astype(BF16)
                wgb_ref[...] = wg_ref[...].astype(BF16)
                wa, wg = wab_ref, wgb_ref
            up = jnp.concatenate([jnp.dot(x_ref[...], wa[...], preferred_element_type=F32),
                                  jnp.dot(x_ref[...], wg[...], preferred_element_type=F32)], axis=1)
            for s_i in range(tm // rows_per_seq):
                base = s_i * (rows_per_seq + HALO) + HALO
                rsl = slice(s_i * rows_per_seq, (s_i + 1) * rows_per_seq)
                ext_ref[base - 2:base, :] = jnp.concatenate([pa_ref[s_i], pg_ref[s_i]], axis=1)
                ext_ref[base:base + rows_per_seq, :] = up[rsl]
                put_state(s_i, ext_ref[base + rows_per_seq - 2:base + rows_per_seq, :])
                act_ref[rsl, :] = conv_gate(base, rows_per_seq)

    @pl.when(j >= nj)
    def _():
        act_ref[...] = jnp.zeros_like(act_ref)


def _ffn_up(x, w_val, w_gate, l, conv_w, conv_b, prev, tm, rows_per_seq, seq_rows, n_act_cols, emit_w=False):
    m, d = x.shape
    f2 = conv_w.shape[1]
    wt = 2 * FFN_TN
    assert f2 % wt == 0 and n_act_cols % FFN_TN == 0
    nj = f2 // wt
    nj_pad = n_act_cols // FFN_TN
    carried = prev is None
    assert not (emit_w and carried)
    n_seq = tm // rows_per_seq
    tiles_per_seq = max(seq_rows // tm, 1)
    n_tiles = m // tm
    val = lambda j: jnp.minimum(j, nj - 1)
    gate0 = 0 if l is None else nj
    in_specs = [pl.BlockSpec((tm, d), lambda i, j: (i, 0)),
                _w_spec(w_val, l, (d, FFN_TN), lambda i, j: (0, val(j))),
                _w_spec(w_gate, l, (d, FFN_TN), lambda i, j: (0, val(j) + gate0)),
                pl.BlockSpec((CONV_W, FFN_TN), lambda i, j: (0, val(j))),
                pl.BlockSpec((CONV_W, FFN_TN), lambda i, j: (0, val(j) + nj)),
                pl.BlockSpec((1, FFN_TN), lambda i, j: (0, val(j))),
                pl.BlockSpec((1, FFN_TN), lambda i, j: (0, val(j) + nj))]
    args = [x, w_val, w_gate, conv_w, conv_w, conv_b, conv_b]
    scratch = []
    if carried:
        assert n_seq == 1
        scratch.append(pltpu.VMEM((nj, 2, wt), F32))
        scratch += [pltpu.VMEM((min(FFN_SUB, tm) + HALO, wt), F32)] * 2
    else:
        in_specs += [pl.BlockSpec((n_seq, 2, FFN_TN), lambda i, j: (i, 0, val(j))),
                     pl.BlockSpec((n_seq, 2, FFN_TN), lambda i, j: (i, 0, val(j) + nj))]
        args += [prev, prev]
        scratch.append(pltpu.VMEM((n_seq * (rows_per_seq + HALO), wt), F32))
    st_shape = jax.ShapeDtypeStruct((n_tiles * n_seq, 2, f2 // 2), F32)
    st_spec = pl.BlockSpec((n_seq, 2, FFN_TN), lambda i, j: (i, 0, val(j)))
    vmem = (2 * (tm * d * 2 + d * wt * w_val.dtype.itemsize + tm * FFN_TN * 2) + tm * wt * 4
            + 8 * min(tm, FFN_SUB) * wt * 4 + (4 << 20))
    out_specs = [pl.BlockSpec((tm, FFN_TN), lambda i, j: (i, j)), st_spec, st_spec]
    out_shape = [jax.ShapeDtypeStruct((m, n_act_cols), BF16), st_shape, st_shape]
    if emit_w:
        vmem += 2 * d * wt * 2
        out_specs += [pl.BlockSpec((d, FFN_TN), lambda i, j: (0, val(j)))] * 2
        out_shape += [jax.ShapeDtypeStruct((d, f2 // 2), BF16)] * 2
    act, st_a, st_g, *w_images = pl.pallas_call(
        functools.partial(_ffn_up_body, tm=tm, rows_per_seq=rows_per_seq, tiles_per_seq=tiles_per_seq,
                          carried=carried, nj=nj, emit_w=emit_w),
        grid=(n_tiles, nj_pad),
        in_specs=in_specs,
        out_specs=out_specs,
        out_shape=out_shape,
        scratch_shapes=scratch,
        compiler_params=_cparams(vmem),
        name="ffn_up_conv_gate",
    )(*args)
    return (act, jnp.concatenate([st_a, st_g], axis=-1), *w_images)


def _pad_rows(a, n_rows):
    return jnp.pad(a, ((0, 0), (0, n_rows - a.shape[1]), (0, 0)))


def kernel(x_prompt, x_sample, cache_a_k, cache_a_v, cache_b_k, cache_b_v, cache_c_k, cache_c_v, cache_c_idx, cache_mem_k, cache_mem_v, state_ffn_conv, mem_prompt, w_in_even, w_out_even, rel_bias_a, w_in_odd, w_out_odd, w_mem_q, w_mem_kv, w_mem_o, w_up, conv_w, conv_b, w_down, ln_g, ln_b):
    bp, sp, d = x_prompt.shape
    bs, n_new, _ = x_sample.shape
    depth = w_up.shape[0]
    h_a = cache_a_k.shape[3]
    h_b = cache_b_k.shape[3]
    n_kv = cache_c_k.shape[3]
    h_c = n_kv * C_GROUP
    hw_a, hw_b, kvw = h_a * HEAD_DIM, h_b * HEAD_DIM, n_kv * HEAD_DIM
    qiw = IDX_HEADS * IDX_DIM
    a_win_s = cache_a_k.shape[2]
    a_win_p = min(A_LEFT_CHUNKS * CHUNK, sp)
    past = cache_b_k.shape[2]
    n_mem = mem_prompt.shape[1]
    mem_w = w_mem_q.shape[2]
    d_ff = w_down.shape[1]
    mp, ms = bp * sp, bs * n_new
    l_s = past + n_new
    l_s_pad = -(-l_s // LANES) * LANES
    main_odd = h_c * HEAD_DIM + 2 * kvw + qiw
    tail_pad = LANES
    ff_pad = -(-d_ff // 1024) * 1024

    wmq = w_mem_q.astype(BF16)
    wmo = w_mem_o.astype(BF16)

    xp = x_prompt.reshape(mp, d)
    xs = x_sample.reshape(ms, d)
    xp_b, xs_b = xp.astype(BF16), xs.astype(BF16)
    mem2d = mem_prompt.reshape(bp * n_mem, d).astype(BF16)
    cmk = cache_mem_k.reshape(depth, bs * n_mem, mem_w).astype(BF16)
    cmv = cache_mem_v.reshape(depth, bs * n_mem, mem_w).astype(BF16)

    outs = {k: [] for k in ("akp", "avp", "bkp", "bvp", "ckp", "cvp", "cip", "mkp", "mvp", "fp",
                            "aks", "avs", "bks", "bvs", "cks", "cvs", "cis", "fs")}

    for l in range(depth):
        i = l // 2
        if l % 2 == 0:
            w_out = w_out_even
            sf, sb, w_in_l = _matmul(xs_b, w_in_even, i, (F32, BF16), 256, 512, emit_w=True)
            pf, pb = _matmul(xp_b, w_in_l, None, (F32, BF16), 1024, 512)
            oa = _band_attn_prompt(pb, rel_bias_a[i], bp, sp, h_a, 0, h_a, 2 * h_a)
            cb0 = 3 * h_a
            ob = _stick_break_attn(pb, pb, pb, bp, sp, sp, h_b, 128, 0, cb0, cb0 + h_b, cb0 + 2 * h_b)
            mix_p = jnp.concatenate([oa, ob], axis=-1)
            pf4 = pf.reshape(bp, sp, -1)
            outs["akp"].append(pf4[:, sp - a_win_p:, hw_a:2 * hw_a].reshape(bp, a_win_p, h_a, HEAD_DIM))
            outs["avp"].append(pf4[:, sp - a_win_p:, 2 * hw_a:3 * hw_a].reshape(bp, a_win_p, h_a, HEAD_DIM))
            o0 = 3 * hw_a
            outs["bkp"].append(pf4[:, :, o0 + hw_b:o0 + 2 * hw_b].reshape(bp, sp, h_b, HEAD_DIM))
            outs["bvp"].append(pf4[:, :, o0 + 2 * hw_b:o0 + 3 * hw_b].reshape(bp, sp, h_b, HEAD_DIM))
            sf4 = sf.reshape(bs, n_new, -1)
            oa_s = _band_attn_cached(sb, cache_a_k, cache_a_v, i, sb, rel_bias_a[i], bs, n_new, 1, 2)
            kb_new = sf4[:, :, o0 + hw_b:o0 + 2 * hw_b]
            vb_new = sf4[:, :, o0 + 2 * hw_b:o0 + 3 * hw_b]
            ob_s = _stick_break_cached(sb, cache_b_k, cache_b_v, i, sb, bs, n_new, cb0, cb0 + h_b, cb0 + 2 * h_b)
            mix_s = jnp.concatenate([oa_s, ob_s], axis=-1)
            outs["aks"].append(sf4[:, :, hw_a:2 * hw_a].reshape(bs, n_new, h_a, HEAD_DIM))
            outs["avs"].append(sf4[:, :, 2 * hw_a:3 * hw_a].reshape(bs, n_new, h_a, HEAD_DIM))
            outs["bks"].append(kb_new.reshape(bs, n_new, h_b, HEAD_DIM))
            outs["bvs"].append(vb_new.reshape(bs, n_new, h_b, HEAD_DIM))
        else:
            w_out = w_out_odd
            sf, sb, w_main_l = _matmul(xs_b, w_in_odd, i, (F32, BF16), 256, 512, n_cols=main_odd, emit_w=True)
            stl, w_tail_l = _matmul(xs_b, w_in_odd, i, (F32,), 256, tail_pad, col0=main_odd, n_cols=tail_pad,
                                    emit_w=True)
            kc0 = h_c
            vc0 = h_c + n_kv
            qic = (h_c * HEAD_DIM + 2 * kvw) // qiw

            def kz_of(ki):
                z = jnp.zeros_like(ki)
                return jnp.concatenate([ki, z, z, ki], axis=-1).astype(BF16).reshape(-1, 2 * LANES)

            pf, pb = _matmul(xp_b, w_main_l, None, (F32, BF16), 1024, 512)
            (pt,) = _matmul(xp_b, w_tail_l, None, (F32,), 1024, tail_pad)
            ki_p = pt[:, :IDX_DIM].reshape(bp, sp, IDX_DIM)
            wi_p = pt[:, IDX_DIM:IDX_DIM + IDX_HEADS]
            sel_p = _dsa_index(pb, wi_p, kz_of(ki_p), bp, sp, sp, sp, 128, 512, 0, qic)
            mix_p = _dsa_attn(pb, pb, pb, sel_p, bp, sp, sp, n_kv, 256, 512, 0, kc0, vc0)
            pf4 = pf.reshape(bp, sp, -1)
            q_w = h_c * HEAD_DIM
            outs["ckp"].append(pf4[:, :, q_w:q_w + kvw].reshape(bp, sp, n_kv, HEAD_DIM))
            outs["cvp"].append(pf4[:, :, q_w + kvw:q_w + 2 * kvw].reshape(bp, sp, n_kv, HEAD_DIM))
            outs["cip"].append(ki_p)
            sf4 = sf.reshape(bs, n_new, -1)
            k_new = sf4[:, :, q_w:q_w + kvw]
            v_new = sf4[:, :, q_w + kvw:q_w + 2 * kvw]
            ki_new = stl[:, :IDX_DIM].reshape(bs, n_new, IDX_DIM)
            wi_s = stl[:, IDX_DIM:IDX_DIM + IDX_HEADS]
            ki_all = _pad_rows(jnp.concatenate([cache_c_idx[i], ki_new], axis=1), l_s_pad)
            sel_s = _dsa_index(sb, wi_s, kz_of(ki_all), bs, n_new, l_s_pad, l_s, n_new, l_s_pad, past, qic)
            mix_s = _dsa_attn(sb, sb, sb, sel_s, bs, n_new, l_s_pad, n_kv, n_new, l_s_pad, past, kc0, vc0,
                              cache=(cache_c_k, cache_c_v, i))
            outs["cks"].append(k_new.reshape(bs, n_new, n_kv, HEAD_DIM))
            outs["cvs"].append(v_new.reshape(bs, n_new, n_kv, HEAD_DIM))
            outs["cis"].append(ki_new)

        xs, xs_b, w_out_l = _matmul_residual_ln(mix_s, w_out, i, xs, ln_g[l, 0], ln_b[l, 0], 256, 512, emit_w=True)
        xp, xp_b = _matmul_residual_ln(mix_p, w_out_l, None, xp, ln_g[l, 0], ln_b[l, 0], 512, 512)

        (mkv,) = _matmul(mem2d, w_mem_kv, l, (F32,), 512, 512)
        mk_p, mv_p = mkv[:, :mem_w], mkv[:, mem_w:]
        outs["mkp"].append(mk_p.reshape(bp, n_mem, MEM_HEADS, HEAD_DIM))
        outs["mvp"].append(mv_p.reshape(bp, n_mem, MEM_HEADS, HEAD_DIM))
        xp, xp_b = _mem_attn_ln(xp, wmq, wmo, l, mk_p, mv_p, ln_g[l, 1], ln_b[l, 1], 256, 256, sp)
        xs, xs_b = _mem_attn_ln(xs, wmq, wmo, l, cmk[l], cmv[l], ln_g[l, 1], ln_b[l, 1],
                                min(ms, 128), n_new, n_new)

        cb_l = conv_b[l][None]
        ffn_tm = min(FFN_TM, sp)
        act_s, st_s, w_val_l, w_gate_l = _ffn_up(xs_b, w_up, w_up, l, conv_w[l], cb_l, state_ffn_conv[l],
                                                 ms, n_new, n_new, ff_pad, emit_w=True)
        act_p, st_p = _ffn_up(xp_b, w_val_l, w_gate_l, None, conv_w[l], cb_l, None, ffn_tm, ffn_tm, sp, ff_pad)
        tiles_per_seq = sp // ffn_tm
        outs["fp"].append(st_p[tiles_per_seq - 1::tiles_per_seq])
        outs["fs"].append(st_s)
        xs, xs_b, w_down_l = _matmul_residual_ln(act_s, w_down, l, xs, ln_g[l, 2], ln_b[l, 2], 256, 512, emit_w=True)
        xp, xp_b = _matmul_residual_ln(act_p, w_down_l, None, xp, ln_g[l, 2], ln_b[l, 2], 512, 512)

    st = jnp.stack
    o = outs
    rolled_k = _rolled_band(cache_a_k, st(o["aks"]))
    rolled_v = _rolled_band(cache_a_v, st(o["avs"]))
    return (xp.reshape(bp, sp, d), xs.reshape(bs, n_new, d),
            st(o["akp"]), st(o["avp"]), st(o["bkp"]), st(o["bvp"]), st(o["ckp"]), st(o["cvp"]), st(o["cip"]),
            st(o["mkp"]), st(o["mvp"]), st(o["fp"]),
            rolled_k, rolled_v, st(o["bks"]), st(o["bvs"]), st(o["cks"]), st(o["cvs"]), st(o["cis"]),
            st(o["fs"]))
```
